```python
import jax
import jax.numpy as jnp
from jax import lax
import numpy as np

D_MODEL = 2048
BATCH = 8
SEQ = 2048
DEPTH = 2
DEC_BATCH = 8
DEC_SEQ = 64
PAST_LEN = 2048

CHUNK = 64
N_META = 16
N_A_LAYERS = DEPTH // 2
N_B_LAYERS = DEPTH - N_A_LAYERS
N_DENSE = (DEPTH + 1) // 2
N_MOE = DEPTH // 2
GLA_HEADS = 4
GLA_DK = D_MODEL // (2 * GLA_HEADS)
GLA_DV = D_MODEL // GLA_HEADS
GLA_RANK = 16
GLA_GATE_NORM = 16.0
HEAD_DIM = 64
SWA_HEADS = D_MODEL // HEAD_DIM
SWA_KV_HEADS = SWA_HEADS // 8
SWA_GROUP = SWA_HEADS // SWA_KV_HEADS
WINDOW = 128
WINDOW_CHUNKS = WINDOW // CHUNK
ROPE_THETA = 10000.0
FFN_DENSE = 256 * ((8 * D_MODEL // 3 + 255) // 256)
N_EXPERTS = 8
TOP_K = 2
FFN_EXPERT = 7 * D_MODEL // 2
EPS = 1e-5
NEG_INF = -1e30

kernel_name = 'yoco_gla_swa_sink_stream_step'


def rmsnorm(x, gain):
    x32 = x.astype(jnp.float32)
    y = x32 * lax.rsqrt(jnp.mean(x32 * x32, axis=-1, keepdims=True) + EPS)
    return (y * gain.astype(jnp.float32)).astype(x.dtype)


def rope(x, pos):
    half = HEAD_DIM // 2
    inv_freq = ROPE_THETA ** (-jnp.arange(half, dtype=jnp.float32) / half)
    ang = pos.astype(jnp.float32)[:, None] * inv_freq[None, :]
    cos = jnp.cos(ang)[:, None, :]
    sin = jnp.sin(ang)[:, None, :]
    x32 = x.astype(jnp.float32)
    x1, x2 = x32[..., :half], x32[..., half:]
    return jnp.concatenate([x1 * cos - x2 * sin, x2 * cos + x1 * sin], axis=-1).astype(x.dtype)


def gla_scan(q, k, v, gk, s0, block):
    bsz, length, heads, _ = q.shape
    n = length // block

    def to_blocks(t):
        return t.reshape(bsz, n, block, heads, t.shape[-1]).transpose(1, 0, 3, 2, 4)

    causal = jnp.tril(jnp.ones((block, block), dtype=bool))

    def step(S, blk):
        qc, kc, vc, gc = blk
        b = jnp.cumsum(gc, axis=2)
        b_last = b[:, :, -1:, :]
        q_dec = qc * jnp.exp(b)
        k_dec = kc * jnp.exp(-b)
        att = jnp.where(causal, jnp.einsum('bhid,bhjd->bhij', q_dec, k_dec), 0.0)
        o = jnp.einsum('bhij,bhjv->bhiv', att, vc) + jnp.einsum('bhid,bhdv->bhiv', q_dec, S)
        S = jnp.exp(b_last)[:, :, 0, :, None] * S + jnp.einsum('bhjd,bhjv->bhdv', kc * jnp.exp(b_last - b), vc)
        return S, o

    S, o = lax.scan(step, s0, (to_blocks(q), to_blocks(k), to_blocks(v), to_blocks(gk)))
    o = o.transpose(1, 0, 3, 2, 4).reshape(bsz, length, heads, v.shape[-1])
    return o, S


def gla_mixer(hn, s0, segments, w_in, w_gk, b_gk, norm_gain, w_out):
    bsz, length, _ = hn.shape
    f32 = jnp.float32
    hk = GLA_HEADS * GLA_DK
    hv = GLA_HEADS * GLA_DV
    proj = hn @ w_in
    q = proj[..., :hk].reshape(bsz, length, GLA_HEADS, GLA_DK).astype(f32) * (GLA_DK ** -0.5)
    k = proj[..., hk:2 * hk].reshape(bsz, length, GLA_HEADS, GLA_DK).astype(f32)
    v = proj[..., 2 * hk:2 * hk + hv].reshape(bsz, length, GLA_HEADS, GLA_DV).astype(f32)
    g_out = proj[..., 2 * hk + hv:2 * hk + 2 * hv].reshape(bsz, length, GLA_HEADS, GLA_DV).astype(f32)
    r = proj[..., 2 * hk + 2 * hv:]
    gk = jax.nn.log_sigmoid((r @ w_gk + b_gk).astype(f32)) / GLA_GATE_NORM
    gk = gk.reshape(bsz, length, GLA_HEADS, GLA_DK)
    s = s0.astype(f32)
    outs = []
    start = 0
    for seg_len, block in segments:
        sl = slice(start, start + seg_len)
        o_seg, s = gla_scan(q[:, sl], k[:, sl], v[:, sl], gk[:, sl], s, block)
        outs.append(o_seg)
        start += seg_len
    o = jnp.concatenate(outs, axis=1) if len(outs) > 1 else outs[0]
    o = rmsnorm(o, norm_gain) * jax.nn.silu(g_out)
    return o.reshape(bsz, length, hv).astype(hn.dtype) @ w_out, s


def shared_kv(h, pos, norm_kv, kv_w):
    bsz, length, _ = h.shape
    kv = rmsnorm(h, norm_kv) @ kv_w
    width = SWA_KV_HEADS * HEAD_DIM
    k = rope(kv[..., :width].reshape(bsz, length, SWA_KV_HEADS, HEAD_DIM), pos)
    v = kv[..., width:].reshape(bsz, length, SWA_KV_HEADS, HEAD_DIM)
    return k, v


def swa_queries(hn, pos, w_q):
    bsz, length, _ = hn.shape
    q = rope((hn @ w_q).reshape(bsz, length, SWA_HEADS, HEAD_DIM), pos)
    return q.reshape(bsz, length, SWA_KV_HEADS, SWA_GROUP, HEAD_DIM) * (HEAD_DIM ** -0.5)


def sink_softmax(s, sink):
    m = jnp.maximum(jnp.max(s, axis=-1, keepdims=True), sink)
    p = jnp.exp(s - m)
    return p / (jnp.sum(p, axis=-1, keepdims=True) + jnp.exp(sink - m))


def swa_prompt(q, k_meta, v_meta, k_fr, v_fr, sinks):
    bsz, seq = q.shape[:2]
    n = seq // CHUNK
    q = q.reshape(bsz, n, CHUNK, SWA_KV_HEADS, SWA_GROUP, HEAD_DIM)

    def gather_band(t_meta, t_fr):
        padded = jnp.pad(t_fr, ((0, 0), (WINDOW_CHUNKS * CHUNK, 0), (0, 0), (0, 0)))
        blocks = padded.reshape(bsz, n + WINDOW_CHUNKS, CHUNK, SWA_KV_HEADS, HEAD_DIM)
        band = jnp.concatenate([blocks[:, i:i + n] for i in range(WINDOW_CHUNKS + 1)], axis=2)
        meta = jnp.broadcast_to(t_meta[:, None], (bsz, n, N_META, SWA_KV_HEADS, HEAD_DIM))
        return jnp.concatenate([meta, band], axis=2)

    keys = gather_band(k_meta, k_fr)
    vals = gather_band(v_meta, v_fr)
    band_len = (WINDOW_CHUNKS + 1) * CHUNK
    src_chunk = jnp.arange(n)[:, None] - WINDOW_CHUNKS + jnp.arange(band_len)[None, :] // CHUNK
    valid = jnp.concatenate([jnp.ones((n, N_META), dtype=bool), src_chunk >= 0], axis=1)
    s = jnp.einsum('bnqhgd,bnkhd->bnhgqk', q, keys).astype(jnp.float32)
    s = jnp.where(valid[None, :, None, None, None, :], s, NEG_INF)
    sink = sinks.astype(jnp.float32).reshape(SWA_KV_HEADS, SWA_GROUP)[None, None, :, :, None, None]
    w = sink_softmax(s, sink)
    o = jnp.einsum('bnhgqk,bnkhd->bnqhgd', w.astype(vals.dtype), vals)
    return o.reshape(bsz, seq, SWA_HEADS * HEAD_DIM)


def swa_sample(q, keys, vals, sinks):
    bsz, t = q.shape[:2]
    s = jnp.einsum('bqhgd,bkhd->bhgqk', q, keys).astype(jnp.float32)
    sink = sinks.astype(jnp.float32).reshape(SWA_KV_HEADS, SWA_GROUP)[None, :, :, None, None]
    w = sink_softmax(s, sink)
    o = jnp.einsum('bhgqk,bkhd->bqhgd', w.astype(vals.dtype), vals)
    return o.reshape(bsz, t, SWA_HEADS * HEAD_DIM)


def swiglu(x, w_gu, w_d):
    g, u = jnp.split(x @ w_gu, 2, axis=-1)
    return (jax.nn.silu(g) * u) @ w_d


def moe_ffn(x, w_router, w_gu, w_d):
    shape = x.shape
    xf = x.reshape(-1, shape[-1])
    logits = (xf @ w_router).astype(jnp.float32)
    top_val, top_idx = lax.top_k(logits, TOP_K)
    gate = jax.nn.softmax(top_val, axis=-1)
    combine = jnp.sum(jax.nn.one_hot(top_idx, N_EXPERTS, dtype=jnp.float32) * gate[..., None], axis=1)
    out = jnp.zeros_like(xf)
    for e in range(N_EXPERTS):
        out = out + combine[:, e:e + 1].astype(xf.dtype) * swiglu(xf, w_gu[e], w_d[e])
    return out.reshape(shape)


def channel_mixer(h, l, norm_ffn, ffn_w_gate_up, ffn_w_down, moe_w_router, moe_w_gate_up, moe_w_down):
    hn = rmsnorm(h, norm_ffn[l])
    if l % 2 == 0:
        return swiglu(hn, ffn_w_gate_up[l // 2], ffn_w_down[l // 2])
    return moe_ffn(hn, moe_w_router[l // 2], moe_w_gate_up[l // 2], moe_w_down[l // 2])


def setup_inputs(seed: int = 0) -> dict:
    key = jax.random.key(seed)
    ks = jax.random.split(key, 32)
    f32 = jnp.float32

    def nrm(k, shape, scale=1.0):
        return jax.random.normal(k, shape, f32) * scale

    def w(k, shape, fan_in):
        return jax.random.normal(k, shape, f32) * (fan_in ** -0.5)

    def gain(k, shape):
        return 1.0 + 0.02 * jax.random.normal(k, shape, f32)

    win_len = min(WINDOW, PAST_LEN)
    gla_in_width = 2 * GLA_HEADS * GLA_DK + 2 * GLA_HEADS * GLA_DV + GLA_RANK
    kvd = (SWA_KV_HEADS, HEAD_DIM)
    return {
        'x_prompt': nrm(ks[0], (BATCH, SEQ, D_MODEL)),
        'x_sample': nrm(ks[1], (DEC_BATCH, DEC_SEQ, D_MODEL)),
        'state_gla': nrm(ks[2], (N_A_LAYERS, DEC_BATCH, GLA_HEADS, GLA_DK, GLA_DV)),
        'cache_k_meta': nrm(ks[3], (DEC_BATCH, N_META) + kvd),
        'cache_v_meta': nrm(ks[4], (DEC_BATCH, N_META) + kvd),
        'cache_k_win': nrm(ks[5], (DEC_BATCH, win_len) + kvd),
        'cache_v_win': nrm(ks[6], (DEC_BATCH, win_len) + kvd),
        'meta_tokens': nrm(ks[7], (N_META, D_MODEL)),
        'norm_mix': gain(ks[8], (DEPTH, D_MODEL)),
        'norm_ffn': gain(ks[9], (DEPTH, D_MODEL)),
        'norm_kv': gain(ks[10], (D_MODEL,)),
        'norm_final': gain(ks[11], (D_MODEL,)),
        'gla_w_in': w(ks[12], (N_A_LAYERS, D_MODEL, gla_in_width), D_MODEL),
        'gla_w_gk': w(ks[13], (N_A_LAYERS, GLA_RANK, GLA_HEADS * GLA_DK), GLA_RANK),
        'gla_b_gk': nrm(ks[14], (N_A_LAYERS, GLA_HEADS * GLA_DK), 0.1),
        'gla_norm': gain(ks[15], (N_A_LAYERS, GLA_DV)),
        'gla_w_out': w(ks[16], (N_A_LAYERS, GLA_HEADS * GLA_DV, D_MODEL), GLA_HEADS * GLA_DV),
        'kv_w': w(ks[17], (D_MODEL, 2 * SWA_KV_HEADS * HEAD_DIM), D_MODEL),
        'attn_w_q': w(ks[18], (N_B_LAYERS, D_MODEL, SWA_HEADS * HEAD_DIM), D_MODEL),
        'attn_sinks': nrm(ks[19], (N_B_LAYERS, SWA_HEADS), 0.5),
        'attn_w_out': w(ks[20], (N_B_LAYERS, SWA_HEADS * HEAD_DIM, D_MODEL), SWA_HEADS * HEAD_DIM),
        'ffn_w_gate_up': w(ks[21], (N_DENSE, D_MODEL, 2 * FFN_DENSE), D_MODEL),
        'ffn_w_down': w(ks[22], (N_DENSE, FFN_DENSE, D_MODEL), FFN_DENSE),
        'moe_w_router': w(ks[23], (N_MOE, D_MODEL, N_EXPERTS), D_MODEL),
        'moe_w_gate_up': w(ks[24], (N_MOE, N_EXPERTS, D_MODEL, 2 * FFN_EXPERT), D_MODEL),
        'moe_w_down': w(ks[25], (N_MOE, N_EXPERTS, FFN_EXPERT, D_MODEL), FFN_EXPERT),
    }


def reference(x_prompt, x_sample, state_gla, cache_k_meta, cache_v_meta, cache_k_win, cache_v_win,
              meta_tokens, norm_mix, norm_ffn, norm_kv, norm_final,
              gla_w_in, gla_w_gk, gla_b_gk, gla_norm, gla_w_out,
              kv_w, attn_w_q, attn_sinks, attn_w_out,
              ffn_w_gate_up, ffn_w_down, moe_w_router, moe_w_gate_up, moe_w_down):
    ffn_args = (norm_ffn, ffn_w_gate_up, ffn_w_down, moe_w_router, moe_w_gate_up, moe_w_down)

    bsz, seq, _ = x_prompt.shape
    h = jnp.concatenate([jnp.broadcast_to(meta_tokens[None].astype(x_prompt.dtype), (bsz, N_META, D_MODEL)),
                         x_prompt], axis=1)
    pos_p = jnp.arange(N_META + seq)
    zero_state = jnp.zeros((bsz, GLA_HEADS, GLA_DK, GLA_DV), jnp.float32)
    gla_states_p = []
    for l in range(DEPTH):
        if l < N_A_LAYERS:
            o, s_fin = gla_mixer(rmsnorm(h, norm_mix[l]), zero_state, ((N_META, N_META), (seq, CHUNK)),
                                 gla_w_in[l], gla_w_gk[l], gla_b_gk[l], gla_norm[l], gla_w_out[l])
            h = h + o
            gla_states_p.append(s_fin.astype(state_gla.dtype))
        else:
            j = l - N_A_LAYERS
            q = swa_queries(rmsnorm(h, norm_mix[l]), pos_p[N_META:], attn_w_q[j])
            h = h + swa_prompt(q, k_meta_p, v_meta_p, k_fr, v_fr, attn_sinks[j]) @ attn_w_out[j]
        h = h + channel_mixer(h, l, *ffn_args)
        if l == N_A_LAYERS - 1:
            k_sh, v_sh = shared_kv(h, pos_p, norm_kv, kv_w)
            k_meta_p, v_meta_p = k_sh[:, :N_META], v_sh[:, :N_META]
            k_fr, v_fr = k_sh[:, N_META:], v_sh[:, N_META:]
            h = h[:, N_META:]
    y_prompt = rmsnorm(h, norm_final)
    win_p = min(WINDOW, seq)
    k_win_p, v_win_p = k_fr[:, seq - win_p:], v_fr[:, seq - win_p:]

    t_new = x_sample.shape[1]
    pos_s = N_META + PAST_LEN + jnp.arange(t_new)
    h = x_sample
    gla_states_s = []
    for l in range(DEPTH):
        if l < N_A_LAYERS:
            o, s_fin = gla_mixer(rmsnorm(h, norm_mix[l]), state_gla[l], ((t_new, t_new),),
                                 gla_w_in[l], gla_w_gk[l], gla_b_gk[l], gla_norm[l], gla_w_out[l])
            h = h + o
            gla_states_s.append(s_fin.astype(state_gla.dtype))
        else:
            j = l - N_A_LAYERS
            q = swa_queries(rmsnorm(h, norm_mix[l]), pos_s, attn_w_q[j])
            h = h + swa_sample(q, k_all, v_all, attn_sinks[j]) @ attn_w_out[j]
        h = h + channel_mixer(h, l, *ffn_args)
        if l == N_A_LAYERS - 1:
            k_new, v_new = shared_kv(h, pos_s, norm_kv, kv_w)
            k_all = jnp.concatenate([cache_k_meta, cache_k_win, k_new.astype(cache_k_win.dtype)], axis=1)
            v_all = jnp.concatenate([cache_v_meta, cache_v_win, v_new.astype(cache_v_win.dtype)], axis=1)
    y_sample = rmsnorm(h, norm_final)

    return (y_prompt, y_sample, jnp.stack(gla_states_p), jnp.stack(gla_states_s),
            k_meta_p, v_meta_p, k_win_p, v_win_p, k_new, v_new)
```

```python
import functools

import numpy as np
import jax
import jax.numpy as jnp
from jax import lax
from jax.experimental import pallas as pl
from jax.experimental.pallas import tpu as pltpu

F32 = jnp.float32
BF16 = jnp.bfloat16
HIGHEST = lax.Precision.HIGHEST

D_MODEL = 2048
CHUNK = 64
N_META = 16
GLA_HEADS = 4
GLA_DK = 256
GLA_DV = 512
GLA_RANK = 16
GLA_GATE_NORM = 16.0
HEAD_DIM = 64
SWA_HEADS = 32
SWA_KV_HEADS = 4
SWA_GROUP = 8
WINDOW_CHUNKS = 2
ROPE_THETA = 10000.0
N_EXPERTS = 8
EPS = 1e-5
NEG_INF = -1e30

LANES = 128
ROW_TILE = 512
COL_TILE = 512
MIB = 2 ** 20


def _params(semantics, vmem_mib):
    return pltpu.CompilerParams(dimension_semantics=semantics, vmem_limit_bytes=vmem_mib * MIB)


def _norm_kernel(x_ref, g_ref, *o_refs):
    x = x_ref[...]
    y = x * lax.rsqrt(jnp.mean(x * x, axis=-1, keepdims=True) + EPS)
    for i, o_ref in enumerate(o_refs):
        o_ref[...] = (y * g_ref[i:i + 1, :]).astype(o_ref.dtype)


def rms_rows(x, gains):
    rows, d = x.shape
    n = gains.shape[0]
    return pl.pallas_call(
        _norm_kernel,
        grid=(rows // ROW_TILE,),
        in_specs=[pl.BlockSpec((ROW_TILE, d), lambda i: (i, 0)),
                  pl.BlockSpec((n, d), lambda i: (0, 0))],
        out_specs=[pl.BlockSpec((ROW_TILE, d), lambda i: (i, 0)) for _ in range(n)],
        out_shape=[jax.ShapeDtypeStruct((rows, d), BF16) for _ in range(n)],
        compiler_params=_params(("parallel",), 40),
        name="rms_rows",
    )(x, gains)


def _swap_halves(x):
    lane = lax.broadcasted_iota(jnp.int32, x.shape, 1)
    first_half = (lane % HEAD_DIM) < (HEAD_DIM // 2)
    return jnp.where(first_half, pltpu.roll(x, LANES - HEAD_DIM // 2, 1), pltpu.roll(x, HEAD_DIM // 2, 1))


def _mm_kernel(te_ref, nu_ref, *refs, mode, a_packed, rope_cols, scale):
    if mode == "swiglu":
        a_ref, w_ref, w2_ref, o_ref, wb_ref, wb2_ref = refs
    elif mode == "residual":
        a_ref, w_ref, res_ref, o_ref, wb_ref = refs
    elif mode == "rope":
        a_ref, w_ref, cos_ref, sin_ref, o_ref, wb_ref = refs
    else:
        a_ref, w_ref, o_ref, wb_ref = refs
    t = pl.program_id(1)

    @pl.when(t >= nu_ref[0])
    def _():
        o_ref[...] = jnp.zeros_like(o_ref)

    @pl.when(t < nu_ref[0])
    def _():
        @pl.when((t == 0) | (te_ref[t] != te_ref[jnp.maximum(t - 1, 0)]))
        def _():
            wb_ref[...] = w_ref[...].astype(BF16)
            if mode == "swiglu":
                wb2_ref[...] = w2_ref[...].astype(BF16)

        a = a_ref[...]
        if a_packed:
            lo = pltpu.bitcast(a << 16, F32)
            hi = pltpu.bitcast(a & jnp.uint32(0xFFFF0000), F32)
            a = jnp.concatenate([lo, hi], axis=1)
        a = a.astype(BF16)
        acc = jnp.dot(a, wb_ref[...], preferred_element_type=F32)
        if mode == "swiglu":
            up = jnp.dot(a, wb2_ref[...], preferred_element_type=F32)
            o_ref[...] = (acc * jax.nn.sigmoid(acc) * up).astype(o_ref.dtype)
        elif mode == "residual":
            o_ref[...] = (res_ref[...] + acc).astype(o_ref.dtype)
        elif mode == "rope":
            cos = cos_ref[...]
            sin = sin_ref[...]
            for c in range(acc.shape[1] // LANES):
                x = acc[:, c * LANES:(c + 1) * LANES]
                if c * LANES < rope_cols:
                    x = x * cos + _swap_halves(x) * sin
                o_ref[:, c * LANES:(c + 1) * LANES] = (x * scale).astype(o_ref.dtype)
        else:
            o_ref[...] = acc.astype(o_ref.dtype)


def matmul(a, w, *, n_out, out_dtype, mode="plain", tile_expert=None, n_used=None, residual=None,
           rope_tabs=None, rope_cols=0, scale=1.0, col0=0, up_col0=0, tm=ROW_TILE, tn=COL_TILE,
           a_packed=False, rows=None, vmem_mib=48, name="matmul"):
    rows = a.shape[0] if rows is None else rows
    k = w.shape[1]
    n_tiles = rows // tm
    n_col = n_out // tn
    if tile_expert is None:
        tile_expert = jnp.zeros((n_tiles,), jnp.int32)
        n_used = jnp.full((1,), n_tiles, jnp.int32)
    cb0 = col0 // tn
    ub0 = up_col0 // tn

    def row_of(t, nu):
        return jnp.minimum(t, nu[0] - 1)

    a_spec = pl.BlockSpec((tm, a.shape[1]), lambda j, t, te, nu: (row_of(t, nu), 0))
    w_spec = pl.BlockSpec((None, k, tn), lambda j, t, te, nu: (te[row_of(t, nu)], 0, cb0 + j))
    o_spec = pl.BlockSpec((tm, tn), lambda j, t, te, nu: (t, j))
    in_specs = [a_spec, w_spec]
    args = [a, w]
    scratch = [pltpu.VMEM((k, tn), BF16)]
    if mode == "swiglu":
        in_specs.append(pl.BlockSpec((None, k, tn), lambda j, t, te, nu: (te[row_of(t, nu)], 0, ub0 + j)))
        args.append(w)
        scratch.append(pltpu.VMEM((k, tn), BF16))
    elif mode == "residual":
        in_specs.append(o_spec)
        args.append(residual)
    elif mode == "rope":
        tab_spec = pl.BlockSpec((tm, LANES), lambda j, t, te, nu: (row_of(t, nu), 0))
        in_specs += [tab_spec, tab_spec]
        args += list(rope_tabs)
    kern = functools.partial(_mm_kernel, mode=mode, a_packed=a_packed, rope_cols=rope_cols, scale=scale)
    return pl.pallas_call(
        kern,
        grid_spec=pltpu.PrefetchScalarGridSpec(
            num_scalar_prefetch=2, grid=(n_col, n_tiles),
            in_specs=in_specs, out_specs=o_spec, scratch_shapes=scratch),
        out_shape=jax.ShapeDtypeStruct((rows, n_out), out_dtype),
        compiler_params=_params(("arbitrary", "arbitrary"), vmem_mib),
        name=name,
    )(tile_expert, n_used, *args)


def _gla_kernel(q_ref, k_ref, v_ref, go_ref, r_ref, wgk_ref, bgk_ref, gn_ref, s0_ref, og_in_ref,
                o_ref, sfin_ref, st_ref, *, n_chunks, n_valid):
    del og_in_ref
    st_ref[...] = s0_ref[...].T
    ri = lax.broadcasted_iota(jnp.int32, (CHUNK, CHUNK), 0)
    ci = lax.broadcasted_iota(jnp.int32, (CHUNK, CHUNK), 1)
    causal = ci <= ri
    tril = causal.astype(F32)
    row_valid = lax.broadcasted_iota(jnp.int32, (CHUNK, 1), 0) < n_valid
    wgk = wgk_ref[...]
    bgk = bgk_ref[...]
    gn = gn_ref[...]

    def body(c, carry):
        rows = pl.ds(pl.multiple_of(c * CHUNK, CHUNK), CHUNK)
        z = jnp.dot(r_ref[rows, :], wgk, precision=HIGHEST, preferred_element_type=F32) + bgk
        logsig = jnp.minimum(z, 0.0) - jnp.log(1.0 + jnp.exp(-jnp.abs(z)))
        g = jnp.where(row_valid, logsig / GLA_GATE_NORM, 0.0)
        b = jnp.dot(tril, g, precision=HIGHEST, preferred_element_type=F32)
        b_last = b[CHUNK - 1:CHUNK, :]
        q = q_ref[rows, :].astype(F32) * (GLA_DK ** -0.5)
        k = jnp.where(row_valid, k_ref[rows, :].astype(F32), 0.0)
        v = v_ref[rows, :]
        q_dec = (q * jnp.exp(b)).astype(BF16)
        k_dec = (k * jnp.exp(-b)).astype(BF16)
        k_last = (k * jnp.exp(b_last - b)).astype(BF16)
        att = lax.dot_general(q_dec, k_dec, (((1,), (1,)), ((), ())), preferred_element_type=F32)
        att = jnp.where(causal, att, 0.0).astype(BF16)
        st = st_ref[...]
        o = jnp.dot(att, v, preferred_element_type=F32)
        o = o + lax.dot_general(q_dec, st.astype(BF16), (((1,), (1,)), ((), ())), preferred_element_type=F32)
        st_ref[...] = st * jnp.exp(b_last) + lax.dot_general(
            v, k_last, (((0,), (0,)), ((), ())), preferred_element_type=F32)
        on = o * lax.rsqrt(jnp.mean(o * o, axis=-1, keepdims=True) + EPS) * gn
        go = go_ref[rows, :].astype(F32)
        o_ref[rows, :] = (on * (go * jax.nn.sigmoid(go))).astype(o_ref.dtype)
        return carry

    lax.fori_loop(0, n_chunks, body, 0)
    sfin_ref[...] = st_ref[...].T


def gla_scan(proj, r128, wgk_pad, bgk, gnorm, s0, og, *, n_seq, seq_len, row_block0, n_valid, s0_per_seq):
    n_chunks = seq_len // CHUNK
    kq = GLA_HEADS * GLA_DK // GLA_DK
    kv0 = 2 * GLA_HEADS * GLA_DK // GLA_DV
    kg0 = kv0 + GLA_HEADS
    rb = lambda b: row_block0 + b
    s0_map = (lambda b, h: (b, h, 0, 0)) if s0_per_seq else (lambda b, h: (0, h, 0, 0))
    kern = functools.partial(_gla_kernel, n_chunks=n_chunks, n_valid=n_valid)
    return pl.pallas_call(
        kern,
        grid=(n_seq, GLA_HEADS),
        in_specs=[
            pl.BlockSpec((seq_len, GLA_DK), lambda b, h: (rb(b), h)),
            pl.BlockSpec((seq_len, GLA_DK), lambda b, h: (rb(b), kq + h)),
            pl.BlockSpec((seq_len, GLA_DV), lambda b, h: (rb(b), kv0 + h)),
            pl.BlockSpec((seq_len, GLA_DV), lambda b, h: (rb(b), kg0 + h)),
            pl.BlockSpec((seq_len, LANES), lambda b, h: (rb(b), 0)),
            pl.BlockSpec((LANES, GLA_DK), lambda b, h: (0, h)),
            pl.BlockSpec((1, GLA_DK), lambda b, h: (0, h)),
            pl.BlockSpec((1, GLA_DV), lambda b, h: (0, 0)),
            pl.BlockSpec((None, None, GLA_DK, GLA_DV), s0_map),
            pl.BlockSpec(memory_space=pl.ANY),
        ],
        out_specs=[
            pl.BlockSpec((seq_len, GLA_DV), lambda b, h: (rb(b), h)),
            pl.BlockSpec((None, None, GLA_DK, GLA_DV), lambda b, h: (b, h, 0, 0)),
        ],
        out_shape=[jax.ShapeDtypeStruct(og.shape, og.dtype),
                   jax.ShapeDtypeStruct((n_seq, GLA_HEADS, GLA_DK, GLA_DV), F32)],
        scratch_shapes=[pltpu.VMEM((GLA_DV, GLA_DK), F32)],
        input_output_aliases={9: 0},
        compiler_params=_params(("parallel", "parallel"), 48),
        name="gla_scan",
    )(proj, proj, proj, proj, r128, wgk_pad, bgk, gnorm, s0, og)


def _attn_kernel(q_ref, k_ref, v_ref, km_ref, vm_ref, sink_ref, o_ref, *, chunks_per_step, chunk_offset):
    step = pl.program_id(1)
    km = km_ref[...].astype(BF16)
    vm = vm_ref[...].astype(BF16)
    n_keys = N_META + (WINDOW_CHUNKS + 1) * CHUNK
    j = lax.broadcasted_iota(jnp.int32, (1, n_keys), 1)
    rel_chunk = (j >= N_META + CHUNK).astype(jnp.int32) + (j >= N_META + 2 * CHUNK).astype(jnp.int32)
    for ci in range(chunks_per_step):
        c = step * chunks_per_step + ci + chunk_offset
        wc = jnp.maximum(c - WINDOW_CHUNKS, 0)
        ws = pl.multiple_of(wc * CHUNK, CHUNK)
        kc = jnp.concatenate([km, k_ref[pl.ds(ws, (WINDOW_CHUNKS + 1) * CHUNK), :].astype(BF16)], axis=0)
        vc = jnp.concatenate([vm, v_ref[pl.ds(ws, (WINDOW_CHUNKS + 1) * CHUNK), :].astype(BF16)], axis=0)
        valid = (j < N_META) | (wc + rel_chunk <= c)
        for h in range(SWA_KV_HEADS):
            qh = q_ref[ci * CHUNK:(ci + 1) * CHUNK, h].reshape(CHUNK * SWA_GROUP, HEAD_DIM).astype(BF16)
            kh = kc[:, h * HEAD_DIM:(h + 1) * HEAD_DIM]
            vh = vc[:, h * HEAD_DIM:(h + 1) * HEAD_DIM]
            s = lax.dot_general(qh, kh, (((1,), (1,)), ((), ())), preferred_element_type=F32)
            s = jnp.where(valid, s, NEG_INF)
            sink = sink_ref[h][:, 0:1]
            m = jnp.maximum(jnp.max(s, axis=-1, keepdims=True), sink)
            p = jnp.exp(s - m)
            den = jnp.sum(p, axis=-1, keepdims=True) + jnp.exp(sink - m)
            o = jnp.dot(p.astype(BF16), vh, preferred_element_type=F32) / den
            o_ref[ci * CHUNK:(ci + 1) * CHUNK, h] = o.reshape(CHUNK, SWA_GROUP, HEAD_DIM)


def window_attention(q4, k_arr, v_arr, km_arr, vm_arr, sink_tab, o4, *, n_seq, seq_len, key_len, row_block0,
                     k_map, v_map, km_map, vm_map, chunks_per_step, chunk_offset):
    tq = chunks_per_step * CHUNK
    steps = seq_len // tq
    q_spec = pl.BlockSpec((tq, SWA_KV_HEADS, SWA_GROUP, HEAD_DIM),
                          lambda b, s: (row_block0 + b * steps + s, 0, 0, 0))
    width = SWA_KV_HEADS * HEAD_DIM
    kern = functools.partial(_attn_kernel, chunks_per_step=chunks_per_step, chunk_offset=chunk_offset)
    in_specs = [q_spec,
                pl.BlockSpec((key_len, width), k_map),
                pl.BlockSpec((key_len, width), v_map),
                pl.BlockSpec((N_META, width), km_map),
                pl.BlockSpec((N_META, width), vm_map),
                pl.BlockSpec((SWA_KV_HEADS, CHUNK * SWA_GROUP, LANES), lambda b, s: (0, 0, 0))]
    args = [q4, k_arr, v_arr, km_arr, vm_arr, sink_tab]
    aliases = {}
    if o4 is not None:
        def kern(*refs, _inner=kern):
            _inner(*refs[:6], refs[7])
        in_specs.append(pl.BlockSpec(memory_space=pl.ANY))
        args.append(o4)
        aliases = {6: 0}
    return pl.pallas_call(
        kern,
        grid=(n_seq, steps),
        in_specs=in_specs,
        out_specs=q_spec,
        out_shape=jax.ShapeDtypeStruct(q4.shape, q4.dtype),
        input_output_aliases=aliases,
        compiler_params=_params(("parallel", "parallel"), 40),
        name="window_attention",
    )(*args)


def _router_kernel(x_ref, g_ref, wr_ref, xp_ref, rt_ref, cnt_ref, carry_ref):
    i = pl.program_id(0)

    @pl.when(i == 0)
    def _():
        carry_ref[...] = jnp.zeros_like(carry_ref)

    x = x_ref[...]
    xn = x * lax.rsqrt(jnp.mean(x * x, axis=-1, keepdims=True) + EPS) * g_ref[...]
    bits = pltpu.bitcast(xn.astype(BF16).astype(F32), jnp.uint32)
    half = D_MODEL // 2
    xp_ref[...] = (bits[:, :half] >> 16) | (bits[:, half:] & jnp.uint32(0xFFFF0000))

    tm = x.shape[0]
    logits = jnp.dot(xn, wr_ref[...], precision=HIGHEST, preferred_element_type=F32)
    lane = lax.broadcasted_iota(jnp.int32, (tm, LANES), 1).astype(F32)
    neg = jnp.float32(-jnp.inf)
    lg = jnp.where(lane < N_EXPERTS, logits, neg)
    m1 = jnp.max(lg, axis=-1, keepdims=True)
    i1 = jnp.min(jnp.where(lg == m1, lane, float(LANES)), axis=-1, keepdims=True)
    lg2 = jnp.where(lane == i1, neg, lg)
    m2 = jnp.max(lg2, axis=-1, keepdims=True)
    i2 = jnp.min(jnp.where(lg2 == m2, lane, float(LANES)), axis=-1, keepdims=True)
    e21 = jnp.exp(m2 - m1)
    g1 = 1.0 / (1.0 + e21)
    g2 = e21 / (1.0 + e21)
    oh1 = (lane == i1).astype(F32)
    oh2 = (lane == i2).astype(F32)
    oh = oh1 + oh2
    ri = lax.broadcasted_iota(jnp.int32, (tm, tm), 0)
    ci = lax.broadcasted_iota(jnp.int32, (tm, tm), 1)
    before = (ci < ri).astype(BF16)
    prefix = jnp.dot(before, oh.astype(BF16), preferred_element_type=F32) + carry_ref[0:1, :]
    rank1 = jnp.sum(prefix * oh1, axis=-1, keepdims=True)
    rank2 = jnp.sum(prefix * oh2, axis=-1, keepdims=True)
    total = carry_ref[0:1, :] + jnp.sum(oh, axis=0, keepdims=True)
    carry_ref[...] = jnp.broadcast_to(total, carry_ref.shape)
    cnt_ref[...] = jnp.broadcast_to(total, cnt_ref.shape)
    rt = jnp.where(lane == 0, i1, 0.0)
    rt = jnp.where(lane == 1, i2, rt)
    rt = jnp.where(lane == 2, rank1, rt)
    rt = jnp.where(lane == 3, rank2, rt)
    rt = jnp.where(lane == 4, g1, rt)
    rt = jnp.where(lane == 5, g2, rt)
    rt_ref[...] = rt


def route(h, gain, w_router_pad):
    rows = h.shape[0]
    return pl.pallas_call(
        _router_kernel,
        grid=(rows // ROW_TILE,),
        in_specs=[pl.BlockSpec((ROW_TILE, D_MODEL), lambda i: (i, 0)),
                  pl.BlockSpec((1, D_MODEL), lambda i: (0, 0)),
                  pl.BlockSpec((D_MODEL, LANES), lambda i: (0, 0))],
        out_specs=[pl.BlockSpec((ROW_TILE, D_MODEL // 2), lambda i: (i, 0)),
                   pl.BlockSpec((ROW_TILE, LANES), lambda i: (i, 0)),
                   pl.BlockSpec((8, LANES), lambda i: (0, 0))],
        out_shape=[jax.ShapeDtypeStruct((rows, D_MODEL // 2), jnp.uint32),
                   jax.ShapeDtypeStruct((rows, LANES), F32),
                   jax.ShapeDtypeStruct((8, LANES), F32)],
        scratch_shapes=[pltpu.VMEM((8, LANES), F32)],
        compiler_params=_params(("arbitrary",), 40),
        name="route",
    )(h, gain, w_router_pad)


def _scatter_kernel(pos_ref, xp_ref, xs_in_ref, xs_ref, sem):
    del xs_in_ref
    tm = xp_ref.shape[0]

    def row_copy(r, slot):
        return pltpu.make_async_copy(xp_ref.at[pl.ds(r, 1), :], xs_ref.at[pl.ds(slot, 1), :], sem)

    def issue(r, carry):
        row_copy(r, pos_ref[2 * r]).start()
        row_copy(r, pos_ref[2 * r + 1]).start()
        return carry

    def drain(r, carry):
        row_copy(r, pos_ref[2 * r]).wait()
        row_copy(r, pos_ref[2 * r + 1]).wait()
        return carry

    lax.fori_loop(0, tm, issue, 0)
    lax.fori_loop(0, tm, drain, 0)


def scatter_rows(pos_flat, xp, xs_init):
    rows = xp.shape[0]
    return pl.pallas_call(
        _scatter_kernel,
        grid=(rows // ROW_TILE,),
        in_specs=[pl.BlockSpec((2 * ROW_TILE,), lambda i: (i,), memory_space=pltpu.SMEM),
                  pl.BlockSpec((ROW_TILE, xp.shape[1]), lambda i: (i, 0)),
                  pl.BlockSpec(memory_space=pl.ANY)],
        out_specs=pl.BlockSpec(memory_space=pl.ANY),
        out_shape=jax.ShapeDtypeStruct(xs_init.shape, xs_init.dtype),
        scratch_shapes=[pltpu.SemaphoreType.DMA(())],
        input_output_aliases={2: 0},
        compiler_params=_params(("arbitrary",), 32),
        name="scatter_rows",
    )(pos_flat, xp, xs_init)


def _combine_kernel(pos_ref, rt_ref, h_ref, g_ref, ys_ref, yp_ref, ysm_ref, buf_ref, sem, *, n_prompt_tiles):
    i = pl.program_id(0)
    tm = h_ref.shape[0]

    def row_copy(r, k, slot):
        return pltpu.make_async_copy(ys_ref.at[pl.ds(slot, 1), :], buf_ref.at[k, pl.ds(r, 1), :], sem)

    def issue(r, carry):
        row_copy(r, 0, pos_ref[2 * r]).start()
        row_copy(r, 1, pos_ref[2 * r + 1]).start()
        return carry

    def drain(r, carry):
        row_copy(r, 0, pos_ref[2 * r]).wait()
        row_copy(r, 1, pos_ref[2 * r + 1]).wait()
        return carry

    lax.fori_loop(0, tm, issue, 0)
    lax.fori_loop(0, tm, drain, 0)
    rt = rt_ref[...]
    y = rt[:, 4:5] * buf_ref[0] + rt[:, 5:6] * buf_ref[1]
    x = h_ref[...] + y
    out = x * lax.rsqrt(jnp.mean(x * x, axis=-1, keepdims=True) + EPS) * g_ref[...]

    @pl.when(i < n_prompt_tiles)
    def _():
        yp_ref[...] = out

    @pl.when(i >= n_prompt_tiles)
    def _():
        ysm_ref[...] = out


def combine_rows(pos_flat, rt, h, gain, ys, *, n_prompt_rows):
    rows = h.shape[0]
    npt = n_prompt_rows // ROW_TILE
    kern = functools.partial(_combine_kernel, n_prompt_tiles=npt)
    return pl.pallas_call(
        kern,
        grid=(rows // ROW_TILE,),
        in_specs=[pl.BlockSpec((2 * ROW_TILE,), lambda i: (i,), memory_space=pltpu.SMEM),
                  pl.BlockSpec((ROW_TILE, LANES), lambda i: (i, 0)),
                  pl.BlockSpec((ROW_TILE, D_MODEL), lambda i: (i, 0)),
                  pl.BlockSpec((1, D_MODEL), lambda i: (0, 0)),
                  pl.BlockSpec(memory_space=pl.ANY)],
        out_specs=[pl.BlockSpec((ROW_TILE, D_MODEL), lambda i: (jnp.minimum(i, npt - 1), 0)),
                   pl.BlockSpec((ROW_TILE, D_MODEL), lambda i: (jnp.maximum(i - npt, 0), 0))],
        out_shape=[jax.ShapeDtypeStruct((n_prompt_rows, D_MODEL), F32),
                   jax.ShapeDtypeStruct((rows - n_prompt_rows, D_MODEL), F32)],
        scratch_shapes=[pltpu.VMEM((2, ROW_TILE, D_MODEL), F32), pltpu.SemaphoreType.DMA(())],
        compiler_params=_params(("arbitrary",), 48),
        name="combine_rows",
    )(pos_flat, rt, h, gain, ys)


def _rope_tables(pos):
    half = HEAD_DIM // 2
    inv_freq = ROPE_THETA ** (-jnp.arange(half, dtype=F32) / half)
    ang = jnp.asarray(pos, jnp.int32).astype(F32)[:, None] * inv_freq[None, :]
    cos = jnp.cos(ang)
    sin = jnp.sin(ang)
    reps = LANES // HEAD_DIM
    return (jnp.tile(jnp.concatenate([cos, cos], axis=1), (1, reps)),
            jnp.tile(jnp.concatenate([-sin, sin], axis=1), (1, reps)))


def kernel(x_prompt, x_sample, state_gla, cache_k_meta, cache_v_meta, cache_k_win, cache_v_win, meta_tokens,
           norm_mix, norm_ffn, norm_kv, norm_final, gla_w_in, gla_w_gk, gla_b_gk, gla_norm, gla_w_out, kv_w,
           attn_w_q, attn_sinks, attn_w_out, ffn_w_gate_up, ffn_w_down, moe_w_router, moe_w_gate_up,
           moe_w_down):
    bsz, seq, d = x_prompt.shape
    dbsz, t_new, _ = x_sample.shape
    past_len = 2048
    n_p = bsz * seq
    n_s = dbsz * t_new
    r1 = n_p + n_s
    assert r1 % ROW_TILE == 0 and seq % CHUNK == 0 and t_new == CHUNK
    r0 = -(-(r1 + CHUNK) // ROW_TILE) * ROW_TILE
    meta_row = r1
    hk = GLA_HEADS * GLA_DK
    hv = GLA_HEADS * GLA_DV
    ffn_dense = ffn_w_down.shape[1]
    ffn_expert = moe_w_down.shape[2]
    kv_width = SWA_KV_HEADS * HEAD_DIM

    x0 = jnp.concatenate([x_prompt.reshape(n_p, d), x_sample.reshape(n_s, d), meta_tokens.astype(F32),
                          jnp.zeros((r0 - r1 - N_META, d), F32)], axis=0)

    (xn0,) = rms_rows(x0, norm_mix[0:1])
    proj = matmul(xn0, gla_w_in, n_out=2 * hk + 2 * hv, out_dtype=BF16, name="gla_in_proj")
    w_r = jnp.pad(gla_w_in[:, :, 2 * hk + 2 * hv:], ((0, 0), (0, 0), (0, LANES - GLA_RANK)))
    r128 = matmul(xn0, w_r, n_out=LANES, out_dtype=F32, tn=LANES, name="gla_rank_proj")
    wgk_pad = jnp.pad(gla_w_gk[0], ((0, LANES - GLA_RANK), (0, 0)))
    bgk = gla_b_gk[0][None, :]
    gnorm = gla_norm[0][None, :]
    og = jnp.zeros((r0, hv), BF16)
    gla = functools.partial(gla_scan, proj, r128, wgk_pad, bgk, gnorm)
    og, s_meta = gla(jnp.zeros((1, GLA_HEADS, GLA_DK, GLA_DV), F32), og, n_seq=1, seq_len=CHUNK,
                     row_block0=meta_row // CHUNK, n_valid=N_META, s0_per_seq=False)
    og, s_prompt = gla(s_meta, og, n_seq=bsz, seq_len=seq, row_block0=0, n_valid=CHUNK, s0_per_seq=False)
    og, s_sample = gla(state_gla[0].astype(F32), og, n_seq=dbsz, seq_len=t_new, row_block0=n_p // t_new,
                       n_valid=CHUNK, s0_per_seq=True)
    h1 = matmul(og, gla_w_out, n_out=d, out_dtype=F32, mode="residual", residual=x0, name="gla_out_proj")

    (hn1,) = rms_rows(h1, norm_ffn[0:1])
    act = matmul(hn1, ffn_w_gate_up, n_out=ffn_dense, out_dtype=BF16, mode="swiglu", up_col0=ffn_dense,
                 name="ffn_gate_up")
    h2 = matmul(act, ffn_w_down, n_out=d, out_dtype=F32, mode="residual", residual=h1,
                vmem_mib=56, name="ffn_down")

    pos = np.concatenate([np.tile(N_META + np.arange(seq), bsz),
                          np.tile(N_META + past_len + np.arange(t_new), dbsz),
                          np.arange(N_META), np.zeros(r0 - r1 - N_META, np.int64)])
    rope_tabs = _rope_tables(pos)
    xkv, xq = rms_rows(h2, jnp.stack([norm_kv, norm_mix[1]]))
    kvf = matmul(xkv, kv_w[None], n_out=2 * kv_width, out_dtype=F32, mode="rope", rope_tabs=rope_tabs,
                 rope_cols=kv_width, name="shared_kv")
    q = matmul(xq, attn_w_q, n_out=d, out_dtype=F32, mode="rope", rope_tabs=rope_tabs, rope_cols=d,
               scale=HEAD_DIM ** -0.5, rows=r1, name="attn_q")

    q4 = q.reshape(r1, SWA_KV_HEADS, SWA_GROUP, HEAD_DIM)
    sink_tab = jnp.broadcast_to(
        jnp.tile(attn_sinks[0].astype(F32).reshape(SWA_KV_HEADS, 1, SWA_GROUP), (1, CHUNK, 1))
        .reshape(SWA_KV_HEADS, CHUNK * SWA_GROUP, 1), (SWA_KV_HEADS, CHUNK * SWA_GROUP, LANES))
    meta_blk = meta_row // N_META
    o4 = window_attention(
        q4, kvf, kvf, kvf, kvf, sink_tab, jnp.zeros_like(q4), n_seq=bsz, seq_len=seq, key_len=seq, row_block0=0,
        k_map=lambda b, s: (b, 0), v_map=lambda b, s: (b, 1),
        km_map=lambda b, s: (meta_blk, 0), vm_map=lambda b, s: (meta_blk, 1),
        chunks_per_step=4, chunk_offset=0)
    k_new = kvf[n_p:r1, :kv_width]
    v_new = kvf[n_p:r1, kv_width:]
    win = cache_k_win.shape[1]
    ks = jnp.concatenate([cache_k_win.reshape(dbsz, win, kv_width).astype(F32),
                          k_new.reshape(dbsz, t_new, kv_width)], axis=1).reshape(dbsz * (win + t_new), kv_width)
    vs = jnp.concatenate([cache_v_win.reshape(dbsz, win, kv_width).astype(F32),
                          v_new.reshape(dbsz, t_new, kv_width)], axis=1).reshape(dbsz * (win + t_new), kv_width)
    assert win + t_new == (WINDOW_CHUNKS + 1) * CHUNK
    o4 = window_attention(
        q4, ks, vs, cache_k_meta.reshape(dbsz * N_META, kv_width).astype(F32),
        cache_v_meta.reshape(dbsz * N_META, kv_width).astype(F32), sink_tab, o4,
        n_seq=dbsz, seq_len=t_new, key_len=win + t_new, row_block0=n_p // t_new,
        k_map=lambda b, s: (b, 0), v_map=lambda b, s: (b, 0),
        km_map=lambda b, s: (b, 0), vm_map=lambda b, s: (b, 0),
        chunks_per_step=1, chunk_offset=WINDOW_CHUNKS)
    h3 = matmul(o4.reshape(r1, d), attn_w_out, n_out=d, out_dtype=F32, mode="residual", residual=h2,
                rows=r1, name="attn_out_proj")

    w_router_pad = jnp.pad(moe_w_router[0], ((0, 0), (0, LANES - N_EXPERTS)))
    xp, rt, cnt = route(h3, norm_ffn[1:2], w_router_pad)
    tme = ROW_TILE
    n_tiles_max = -(-(2 * r1 + N_EXPERTS * (tme - 1)) // tme)
    counts = cnt[0, :N_EXPERTS].astype(jnp.int32)
    tiles_per = (counts + tme - 1) // tme
    tile_end = jnp.cumsum(tiles_per)
    starts = (tile_end - tiles_per) * tme
    experts = rt[:, 0:2].astype(jnp.int32)
    ranks = rt[:, 2:4].astype(jnp.int32)
    pos_flat = (starts[experts] + ranks).reshape(-1)
    n_used = tile_end[-1:].astype(jnp.int32)
    tile_expert = jnp.minimum(
        jnp.sum(jnp.arange(n_tiles_max, dtype=jnp.int32)[:, None] >= tile_end[None, :], axis=1),
        N_EXPERTS - 1).astype(jnp.int32)
    xs = scatter_rows(pos_flat, xp, jnp.zeros((n_tiles_max * tme, d // 2), jnp.uint32))
    act_e = matmul(xs, moe_w_gate_up.reshape(N_EXPERTS, d, 2 * ffn_expert), n_out=ffn_expert, out_dtype=BF16,
                   mode="swiglu", up_col0=ffn_expert, tile_expert=tile_expert, n_used=n_used, a_packed=True,
                   name="moe_gate_up")
    half_tile = tme // 2
    ys = matmul(act_e, moe_w_down.reshape(N_EXPERTS, ffn_expert, d), n_out=d, out_dtype=F32,
                tile_expert=jnp.repeat(tile_expert, 2), n_used=2 * n_used, tm=half_tile, vmem_mib=56,
                name="moe_down")
    y_prompt, y_sample = combine_rows(pos_flat, rt, h3, norm_final[None, :], ys, n_prompt_rows=n_p)

    kv_meta = kvf[meta_row:meta_row + N_META]
    k_meta_p = jnp.broadcast_to(kv_meta[None, :, :kv_width], (bsz, N_META, kv_width))
    v_meta_p = jnp.broadcast_to(kv_meta[None, :, kv_width:], (bsz, N_META, kv_width))
    win_p = min(WINDOW_CHUNKS * CHUNK, seq)
    kv_win = kvf[:n_p].reshape(bsz, seq, 2 * kv_width)[:, seq - win_p:]
    shape4 = lambda a: a.reshape(a.shape[0], a.shape[1], SWA_KV_HEADS, HEAD_DIM)
    return (y_prompt.reshape(bsz, seq, d), y_sample.reshape(dbsz, t_new, d),
            s_prompt[None].astype(state_gla.dtype), s_sample[None].astype(state_gla.dtype),
            shape4(k_meta_p), shape4(v_meta_p),
            shape4(kv_win[:, :, :kv_width]), shape4(kv_win[:, :, kv_width:]),
            shape4(k_new.reshape(dbsz, t_new, kv_width)), shape4(v_new.reshape(dbsz, t_new, kv_width)))
```

```python
import functools

import numpy as np
import jax
import jax.numpy as jnp
from jax import lax
from jax.experimental import pallas as pl
from jax.experimental.pallas import tpu as pltpu

F32 = jnp.float32
BF16 = jnp.bfloat16
HIGHEST = lax.Precision.HIGHEST

D_MODEL = 2048
CHUNK = 64
N_META = 16
GLA_HEADS = 4
GLA_DK = 256
GLA_DV = 512
GLA_RANK = 16
GLA_GATE_NORM = 16.0
HEAD_DIM = 64
SWA_HEADS = 32
SWA_KV_HEADS = 4
SWA_GROUP = 8
WINDOW_CHUNKS = 2
ROPE_THETA = 10000.0
N_EXPERTS = 8
EPS = 1e-5
NEG_INF = -1e30

LANES = 128
ROW_TILE = 512
COL_TILE = 512
MIB = 2 ** 20


def _params(semantics, vmem_mib):
    return pltpu.CompilerParams(dimension_semantics=semantics, vmem_limit_bytes=vmem_mib * MIB)


def _norm_kernel(x_ref, g_ref, *o_refs):
    x = x_ref[...]
    y = x * lax.rsqrt(jnp.mean(x * x, axis=-1, keepdims=True) + EPS)
    for i, o_ref in enumerate(o_refs):
        o_ref[...] = (y * g_ref[i:i + 1, :]).astype(o_ref.dtype)


def rms_rows(x, gains):
    rows, d = x.shape
    n = gains.shape[0]
    return pl.pallas_call(
        _norm_kernel,
        grid=(rows // ROW_TILE,),
        in_specs=[pl.BlockSpec((ROW_TILE, d), lambda i: (i, 0)),
                  pl.BlockSpec((n, d), lambda i: (0, 0))],
        out_specs=[pl.BlockSpec((ROW_TILE, d), lambda i: (i, 0)) for _ in range(n)],
        out_shape=[jax.ShapeDtypeStruct((rows, d), BF16) for _ in range(n)],
        compiler_params=_params(("parallel",), 40),
        name="rms_rows",
    )(x, gains)


def _swap_halves(x):
    lane = lax.broadcasted_iota(jnp.int32, x.shape, 1)
    first_half = (lane % HEAD_DIM) < (HEAD_DIM // 2)
    return jnp.where(first_half, pltpu.roll(x, LANES - HEAD_DIM // 2, 1), pltpu.roll(x, HEAD_DIM // 2, 1))


def _mm_kernel(te_ref, nu_ref, *refs, mode, a_packed, rope_cols, scale):
    if mode == "swiglu":
        a_ref, w_ref, w2_ref, o_ref, wb_ref, wb2_ref = refs
    elif mode == "residual":
        a_ref, w_ref, res_ref, o_ref, wb_ref = refs
    elif mode == "rope":
        a_ref, w_ref, cos_ref, sin_ref, o_ref, wb_ref = refs
    else:
        a_ref, w_ref, o_ref, wb_ref = refs
    t = pl.program_id(1)

    @pl.when(t >= nu_ref[0])
    def _():
        o_ref[...] = jnp.zeros_like(o_ref)

    @pl.when(t < nu_ref[0])
    def _():
        @pl.when((t == 0) | (te_ref[t] != te_ref[jnp.maximum(t - 1, 0)]))
        def _():
            wb_ref[...] = w_ref[...].astype(BF16)
            if mode == "swiglu":
                wb2_ref[...] = w2_ref[...].astype(BF16)

        a = a_ref[...]
        if a_packed:
            lo = pltpu.bitcast(a << 16, F32)
            hi = pltpu.bitcast(a & jnp.uint32(0xFFFF0000), F32)
            a = jnp.concatenate([lo, hi], axis=1)
        a = a.astype(BF16)
        acc = jnp.dot(a, wb_ref[...], preferred_element_type=F32)
        if mode == "swiglu":
            up = jnp.dot(a, wb2_ref[...], preferred_element_type=F32)
            o_ref[...] = (acc * jax.nn.sigmoid(acc) * up).astype(o_ref.dtype)
        elif mode == "residual":
            o_ref[...] = (res_ref[...] + acc).astype(o_ref.dtype)
        elif mode == "rope":
            cos = cos_ref[...]
            sin = sin_ref[...]
            for c in range(acc.shape[1] // LANES):
                x = acc[:, c * LANES:(c + 1) * LANES]
                if c * LANES < rope_cols:
                    x = x * cos + _swap_halves(x) * sin
                o_ref[:, c * LANES:(c + 1) * LANES] = (x * scale).astype(o_ref.dtype)
        else:
            o_ref[...] = acc.astype(o_ref.dtype)


def matmul(a, w, *, n_out, out_dtype, mode="plain", tile_expert=None, n_used=None, residual=None,
           rope_tabs=None, rope_cols=0, scale=1.0, col0=0, up_col0=0, tm=ROW_TILE, tn=COL_TILE,
           a_packed=False, rows=None, vmem_mib=48, name="matmul"):
    rows = a.shape[0] if rows is None else rows
    k = w.shape[1]
    n_tiles = rows // tm
    n_col = n_out // tn
    w_mode = pl.Buffered(1) if tile_expert is None else None
    if tile_expert is None:
        tile_expert = jnp.zeros((n_tiles,), jnp.int32)
        n_used = jnp.full((1,), n_tiles, jnp.int32)
    cb0 = col0 // tn
    ub0 = up_col0 // tn

    def row_of(t, nu):
        return jnp.minimum(t, nu[0] - 1)

    a_spec = pl.BlockSpec((tm, a.shape[1]), lambda j, t, te, nu: (row_of(t, nu), 0))
    w_spec = pl.BlockSpec((None, k, tn), lambda j, t, te, nu: (te[row_of(t, nu)], 0, cb0 + j),
                          pipeline_mode=w_mode)
    o_spec = pl.BlockSpec((tm, tn), lambda j, t, te, nu: (t, j))
    in_specs = [a_spec, w_spec]
    args = [a, w]
    scratch = [pltpu.VMEM((k, tn), BF16)]
    if mode == "swiglu":
        in_specs.append(pl.BlockSpec((None, k, tn), lambda j, t, te, nu: (te[row_of(t, nu)], 0, ub0 + j),
                                     pipeline_mode=w_mode))
        args.append(w)
        scratch.append(pltpu.VMEM((k, tn), BF16))
    elif mode == "residual":
        in_specs.append(o_spec)
        args.append(residual)
    elif mode == "rope":
        tab_spec = pl.BlockSpec((tm, LANES), lambda j, t, te, nu: (row_of(t, nu), 0))
        in_specs += [tab_spec, tab_spec]
        args += list(rope_tabs)
    kern = functools.partial(_mm_kernel, mode=mode, a_packed=a_packed, rope_cols=rope_cols, scale=scale)
    return pl.pallas_call(
        kern,
        grid_spec=pltpu.PrefetchScalarGridSpec(
            num_scalar_prefetch=2, grid=(n_col, n_tiles),
            in_specs=in_specs, out_specs=o_spec, scratch_shapes=scratch),
        out_shape=jax.ShapeDtypeStruct((rows, n_out), out_dtype),
        compiler_params=_params(("arbitrary", "arbitrary"), vmem_mib),
        name=name,
    )(tile_expert, n_used, *args)


GLA_ROW_BLOCK = 512


def _gla_kernel(q_ref, k_ref, v_ref, go_ref, r_ref, wgk_ref, bgk_ref, gn_ref, s0_ref, og_in_ref,
                o_ref, sfin_ref, st_ref, *, n_chunks, n_valid):
    del og_in_ref
    rb = pl.program_id(1)

    @pl.when(rb == 0)
    def _():
        for h in range(GLA_HEADS):
            st_ref[h] = s0_ref[h].T

    ri = lax.broadcasted_iota(jnp.int32, (CHUNK, CHUNK), 0)
    ci = lax.broadcasted_iota(jnp.int32, (CHUNK, CHUNK), 1)
    causal = ci <= ri
    tril = causal.astype(F32)
    row_valid = lax.broadcasted_iota(jnp.int32, (CHUNK, 1), 0) < n_valid
    wgk = wgk_ref[...]
    bgk = bgk_ref[...]
    gn = gn_ref[...]

    def body(c, carry):
        rows = pl.ds(pl.multiple_of(c * CHUNK, CHUNK), CHUNK)
        z = jnp.dot(r_ref[rows, :], wgk, precision=HIGHEST, preferred_element_type=F32) + bgk
        logsig = jnp.minimum(z, 0.0) - jnp.log(1.0 + jnp.exp(-jnp.abs(z)))
        g = jnp.where(row_valid, logsig / GLA_GATE_NORM, 0.0)
        b = jnp.dot(tril, g, precision=HIGHEST, preferred_element_type=F32)
        b_last = b[CHUNK - 1:CHUNK, :]
        q = q_ref[rows, :].astype(F32) * (GLA_DK ** -0.5)
        k = jnp.where(row_valid, k_ref[rows, :].astype(F32), 0.0)
        q_dec_all = (q * jnp.exp(b)).astype(BF16)
        k_dec_all = (k * jnp.exp(-b)).astype(BF16)
        k_last_all = (k * jnp.exp(b_last - b)).astype(BF16)
        decay = jnp.exp(b_last)
        for h in range(GLA_HEADS):
            ks = slice(h * GLA_DK, (h + 1) * GLA_DK)
            vs = slice(h * GLA_DV, (h + 1) * GLA_DV)
            q_dec = q_dec_all[:, ks]
            v = v_ref[rows, vs]
            att = lax.dot_general(q_dec, k_dec_all[:, ks], (((1,), (1,)), ((), ())),
                                  preferred_element_type=F32)
            att = jnp.where(causal, att, 0.0).astype(BF16)
            st = st_ref[h]
            o = jnp.dot(att, v, preferred_element_type=F32)
            o = o + lax.dot_general(q_dec, st.astype(BF16), (((1,), (1,)), ((), ())),
                                    preferred_element_type=F32)
            st_ref[h] = st * decay[:, ks] + lax.dot_general(
                v, k_last_all[:, ks], (((0,), (0,)), ((), ())), preferred_element_type=F32)
            on = o * lax.rsqrt(jnp.mean(o * o, axis=-1, keepdims=True) + EPS) * gn
            go = go_ref[rows, vs].astype(F32)
            o_ref[rows, vs] = (on * (go * jax.nn.sigmoid(go))).astype(o_ref.dtype)
        return carry

    lax.fori_loop(0, n_chunks, body, 0)

    @pl.when(rb == pl.num_programs(1) - 1)
    def _():
        for h in range(GLA_HEADS):
            sfin_ref[h] = st_ref[h].T


def gla_scan(proj, r128, wgk_pad, bgk, gnorm, s0, og, *, n_seq, seq_len, row0, n_valid, s0_per_seq):
    blk = min(seq_len, GLA_ROW_BLOCK)
    n_rb = seq_len // blk
    hk = GLA_HEADS * GLA_DK
    hv = GLA_HEADS * GLA_DV
    rb0 = row0 // blk
    row = lambda b, r: rb0 + b * n_rb + r
    s0_map = (lambda b, r: (b, 0, 0, 0)) if s0_per_seq else (lambda b, r: (0, 0, 0, 0))
    kern = functools.partial(_gla_kernel, n_chunks=blk // CHUNK, n_valid=n_valid)
    return pl.pallas_call(
        kern,
        grid=(n_seq, n_rb),
        in_specs=[
            pl.BlockSpec((blk, hk), lambda b, r: (row(b, r), 0)),
            pl.BlockSpec((blk, hk), lambda b, r: (row(b, r), 1)),
            pl.BlockSpec((blk, hv), lambda b, r: (row(b, r), 2 * hk // hv)),
            pl.BlockSpec((blk, hv), lambda b, r: (row(b, r), 2 * hk // hv + 1)),
            pl.BlockSpec((blk, LANES), lambda b, r: (row(b, r), 0)),
            pl.BlockSpec((LANES, hk), lambda b, r: (0, 0)),
            pl.BlockSpec((1, hk), lambda b, r: (0, 0)),
            pl.BlockSpec((1, GLA_DV), lambda b, r: (0, 0)),
            pl.BlockSpec((None, GLA_HEADS, GLA_DK, GLA_DV), s0_map),
            pl.BlockSpec(memory_space=pl.ANY),
        ],
        out_specs=[
            pl.BlockSpec((blk, hv), lambda b, r: (row(b, r), 0)),
            pl.BlockSpec((None, GLA_HEADS, GLA_DK, GLA_DV), lambda b, r: (b, 0, 0, 0)),
        ],
        out_shape=[jax.ShapeDtypeStruct(og.shape, og.dtype),
                   jax.ShapeDtypeStruct((n_seq, GLA_HEADS, GLA_DK, GLA_DV), F32)],
        scratch_shapes=[pltpu.VMEM((GLA_HEADS, GLA_DV, GLA_DK), F32)],
        input_output_aliases={9: 0},
        compiler_params=_params(("parallel", "arbitrary"), 48),
        name="gla_scan",
    )(proj, proj, proj, proj, r128, wgk_pad, bgk, gnorm, s0, og)


def _attn_kernel(q_ref, k_ref, v_ref, km_ref, vm_ref, sink_ref, o_ref, *, chunks_per_step, chunk_offset):
    step = pl.program_id(1)
    km = km_ref[...].astype(BF16)
    vm = vm_ref[...].astype(BF16)
    n_keys = N_META + (WINDOW_CHUNKS + 1) * CHUNK
    j = lax.broadcasted_iota(jnp.int32, (1, n_keys), 1)
    rel_chunk = (j >= N_META + CHUNK).astype(jnp.int32) + (j >= N_META + 2 * CHUNK).astype(jnp.int32)
    for ci in range(chunks_per_step):
        c = step * chunks_per_step + ci + chunk_offset
        wc = jnp.maximum(c - WINDOW_CHUNKS, 0)
        ws = pl.multiple_of(wc * CHUNK, CHUNK)
        kc = jnp.concatenate([km, k_ref[pl.ds(ws, (WINDOW_CHUNKS + 1) * CHUNK), :].astype(BF16)], axis=0)
        vc = jnp.concatenate([vm, v_ref[pl.ds(ws, (WINDOW_CHUNKS + 1) * CHUNK), :].astype(BF16)], axis=0)
        valid = (j < N_META) | (wc + rel_chunk <= c)
        for h in range(SWA_KV_HEADS):
            qh = q_ref[ci * CHUNK:(ci + 1) * CHUNK, h].reshape(CHUNK * SWA_GROUP, HEAD_DIM).astype(BF16)
            kh = kc[:, h * HEAD_DIM:(h + 1) * HEAD_DIM]
            vh = vc[:, h * HEAD_DIM:(h + 1) * HEAD_DIM]
            s = lax.dot_general(qh, kh, (((1,), (1,)), ((), ())), preferred_element_type=F32)
            s = jnp.where(valid, s, NEG_INF)
            sink = sink_ref[h][:, 0:1]
            m = jnp.maximum(jnp.max(s, axis=-1, keepdims=True), sink)
            p = jnp.exp(s - m)
            den = jnp.sum(p, axis=-1, keepdims=True) + jnp.exp(sink - m)
            o = jnp.dot(p.astype(BF16), vh, preferred_element_type=F32) / den
            o_ref[ci * CHUNK:(ci + 1) * CHUNK, h] = o.reshape(CHUNK, SWA_GROUP, HEAD_DIM)


def window_attention(q4, k_arr, v_arr, km_arr, vm_arr, sink_tab, o4, *, n_seq, seq_len, key_len, row_block0,
                     k_map, v_map, km_map, vm_map, chunks_per_step, chunk_offset):
    tq = chunks_per_step * CHUNK
    steps = seq_len // tq
    q_spec = pl.BlockSpec((tq, SWA_KV_HEADS, SWA_GROUP, HEAD_DIM),
                          lambda b, s: (row_block0 + b * steps + s, 0, 0, 0))
    width = SWA_KV_HEADS * HEAD_DIM
    kern = functools.partial(_attn_kernel, chunks_per_step=chunks_per_step, chunk_offset=chunk_offset)
    in_specs = [q_spec,
                pl.BlockSpec((key_len, width), k_map),
                pl.BlockSpec((key_len, width), v_map),
                pl.BlockSpec((N_META, width), km_map),
                pl.BlockSpec((N_META, width), vm_map),
                pl.BlockSpec((SWA_KV_HEADS, CHUNK * SWA_GROUP, LANES), lambda b, s: (0, 0, 0))]
    args = [q4, k_arr, v_arr, km_arr, vm_arr, sink_tab]
    aliases = {}
    if o4 is not None:
        def kern(*refs, _inner=kern):
            _inner(*refs[:6], refs[7])
        in_specs.append(pl.BlockSpec(memory_space=pl.ANY))
        args.append(o4)
        aliases = {6: 0}
    return pl.pallas_call(
        kern,
        grid=(n_seq, steps),
        in_specs=in_specs,
        out_specs=q_spec,
        out_shape=jax.ShapeDtypeStruct(q4.shape, q4.dtype),
        input_output_aliases=aliases,
        compiler_params=_params(("parallel", "parallel"), 40),
        name="window_attention",
    )(*args)


def _router_kernel(x_ref, g_ref, wr_ref, xp_ref, rt_ref, cnt_ref, carry_ref):
    i = pl.program_id(0)

    @pl.when(i == 0)
    def _():
        carry_ref[...] = jnp.zeros_like(carry_ref)

    x = x_ref[...]
    xn = x * lax.rsqrt(jnp.mean(x * x, axis=-1, keepdims=True) + EPS) * g_ref[...]
    bits = pltpu.bitcast(xn.astype(BF16).astype(F32), jnp.uint32)
    half = D_MODEL // 2
    xp_ref[...] = (bits[:, :half] >> 16) | (bits[:, half:] & jnp.uint32(0xFFFF0000))

    tm = x.shape[0]
    logits = jnp.dot(xn, wr_ref[...], precision=HIGHEST, preferred_element_type=F32)
    lane = lax.broadcasted_iota(jnp.int32, (tm, LANES), 1).astype(F32)
    neg = jnp.float32(-jnp.inf)
    lg = jnp.where(lane < N_EXPERTS, logits, neg)
    m1 = jnp.max(lg, axis=-1, keepdims=True)
    i1 = jnp.min(jnp.where(lg == m1, lane, float(LANES)), axis=-1, keepdims=True)
    lg2 = jnp.where(lane == i1, neg, lg)
    m2 = jnp.max(lg2, axis=-1, keepdims=True)
    i2 = jnp.min(jnp.where(lg2 == m2, lane, float(LANES)), axis=-1, keepdims=True)
    e21 = jnp.exp(m2 - m1)
    g1 = 1.0 / (1.0 + e21)
    g2 = e21 / (1.0 + e21)
    oh1 = (lane == i1).astype(F32)
    oh2 = (lane == i2).astype(F32)
    oh = oh1 + oh2
    ri = lax.broadcasted_iota(jnp.int32, (tm, tm), 0)
    ci = lax.broadcasted_iota(jnp.int32, (tm, tm), 1)
    before = (ci < ri).astype(BF16)
    prefix = jnp.dot(before, oh.astype(BF16), preferred_element_type=F32) + carry_ref[0:1, :]
    rank1 = jnp.sum(prefix * oh1, axis=-1, keepdims=True)
    rank2 = jnp.sum(prefix * oh2, axis=-1, keepdims=True)
    total = carry_ref[0:1, :] + jnp.sum(oh, axis=0, keepdims=True)
    carry_ref[...] = jnp.broadcast_to(total, carry_ref.shape)
    cnt_ref[...] = jnp.broadcast_to(total, cnt_ref.shape)
    rt = jnp.where(lane == 0, i1, 0.0)
    rt = jnp.where(lane == 1, i2, rt)
    rt = jnp.where(lane == 2, rank1, rt)
    rt = jnp.where(lane == 3, rank2, rt)
    rt = jnp.where(lane == 4, g1, rt)
    rt = jnp.where(lane == 5, g2, rt)
    rt_ref[...] = rt


def route(h, gain, w_router_pad):
    rows = h.shape[0]
    return pl.pallas_call(
        _router_kernel,
        grid=(rows // ROW_TILE,),
        in_specs=[pl.BlockSpec((ROW_TILE, D_MODEL), lambda i: (i, 0)),
                  pl.BlockSpec((1, D_MODEL), lambda i: (0, 0)),
                  pl.BlockSpec((D_MODEL, LANES), lambda i: (0, 0))],
        out_specs=[pl.BlockSpec((ROW_TILE, D_MODEL // 2), lambda i: (i, 0)),
                   pl.BlockSpec((ROW_TILE, LANES), lambda i: (i, 0)),
                   pl.BlockSpec((8, LANES), lambda i: (0, 0))],
        out_shape=[jax.ShapeDtypeStruct((rows, D_MODEL // 2), jnp.uint32),
                   jax.ShapeDtypeStruct((rows, LANES), F32),
                   jax.ShapeDtypeStruct((8, LANES), F32)],
        scratch_shapes=[pltpu.VMEM((8, LANES), F32)],
        compiler_params=_params(("arbitrary",), 40),
        name="route",
    )(h, gain, w_router_pad)


def _scatter_kernel(pos_ref, xp_ref, xs_in_ref, xs_ref, sem):
    del xs_in_ref
    tm = xp_ref.shape[0]

    def row_copy(r, slot):
        return pltpu.make_async_copy(xp_ref.at[pl.ds(r, 1), :], xs_ref.at[pl.ds(slot, 1), :], sem)

    def issue(r, carry):
        row_copy(r, pos_ref[2 * r]).start()
        row_copy(r, pos_ref[2 * r + 1]).start()
        return carry

    def drain(r, carry):
        row_copy(r, pos_ref[2 * r]).wait()
        row_copy(r, pos_ref[2 * r + 1]).wait()
        return carry

    lax.fori_loop(0, tm, issue, 0)
    lax.fori_loop(0, tm, drain, 0)


def scatter_rows(pos_flat, xp, xs_init):
    rows = xp.shape[0]
    return pl.pallas_call(
        _scatter_kernel,
        grid=(rows // ROW_TILE,),
        in_specs=[pl.BlockSpec((2 * ROW_TILE,), lambda i: (i,), memory_space=pltpu.SMEM),
                  pl.BlockSpec((ROW_TILE, xp.shape[1]), lambda i: (i, 0)),
                  pl.BlockSpec(memory_space=pl.ANY)],
        out_specs=pl.BlockSpec(memory_space=pl.ANY),
        out_shape=jax.ShapeDtypeStruct(xs_init.shape, xs_init.dtype),
        scratch_shapes=[pltpu.SemaphoreType.DMA(())],
        input_output_aliases={2: 0},
        compiler_params=_params(("arbitrary",), 32),
        name="scatter_rows",
    )(pos_flat, xp, xs_init)


def _combine_kernel(pos_ref, rt_ref, h_ref, g_ref, ys_ref, yp_ref, ysm_ref, buf_ref, sem, *, n_prompt_tiles):
    i = pl.program_id(0)
    tm = h_ref.shape[0]

    def row_copy(r, k, slot):
        return pltpu.make_async_copy(ys_ref.at[pl.ds(slot, 1), :], buf_ref.at[k, pl.ds(r, 1), :], sem)

    def issue(r, carry):
        row_copy(r, 0, pos_ref[2 * r]).start()
        row_copy(r, 1, pos_ref[2 * r + 1]).start()
        return carry

    def drain(r, carry):
        row_copy(r, 0, pos_ref[2 * r]).wait()
        row_copy(r, 1, pos_ref[2 * r + 1]).wait()
        return carry

    lax.fori_loop(0, tm, issue, 0)
    lax.fori_loop(0, tm, drain, 0)
    rt = rt_ref[...]
    y = rt[:, 4:5] * buf_ref[0] + rt[:, 5:6] * buf_ref[1]
    x = h_ref[...] + y
    out = x * lax.rsqrt(jnp.mean(x * x, axis=-1, keepdims=True) + EPS) * g_ref[...]

    @pl.when(i < n_prompt_tiles)
    def _():
        yp_ref[...] = out

    @pl.when(i >= n_prompt_tiles)
    def _():
        ysm_ref[...] = out


def combine_rows(pos_flat, rt, h, gain, ys, *, n_prompt_rows):
    rows = h.shape[0]
    npt = n_prompt_rows // ROW_TILE
    kern = functools.partial(_combine_kernel, n_prompt_tiles=npt)
    return pl.pallas_call(
        kern,
        grid=(rows // ROW_TILE,),
        in_specs=[pl.BlockSpec((2 * ROW_TILE,), lambda i: (i,), memory_space=pltpu.SMEM),
                  pl.BlockSpec((ROW_TILE, LANES), lambda i: (i, 0)),
                  pl.BlockSpec((ROW_TILE, D_MODEL), lambda i: (i, 0)),
                  pl.BlockSpec((1, D_MODEL), lambda i: (0, 0)),
                  pl.BlockSpec(memory_space=pl.ANY)],
        out_specs=[pl.BlockSpec((ROW_TILE, D_MODEL), lambda i: (jnp.minimum(i, npt - 1), 0)),
                   pl.BlockSpec((ROW_TILE, D_MODEL), lambda i: (jnp.maximum(i - npt, 0), 0))],
        out_shape=[jax.ShapeDtypeStruct((n_prompt_rows, D_MODEL), F32),
                   jax.ShapeDtypeStruct((rows - n_prompt_rows, D_MODEL), F32)],
        scratch_shapes=[pltpu.VMEM((2, ROW_TILE, D_MODEL), F32), pltpu.SemaphoreType.DMA(())],
        compiler_params=_params(("arbitrary",), 48),
        name="combine_rows",
    )(pos_flat, rt, h, gain, ys)


def _rope_tables(pos):
    half = HEAD_DIM // 2
    inv_freq = ROPE_THETA ** (-jnp.arange(half, dtype=F32) / half)
    ang = jnp.asarray(pos, jnp.int32).astype(F32)[:, None] * inv_freq[None, :]
    cos = jnp.cos(ang)
    sin = jnp.sin(ang)
    reps = LANES // HEAD_DIM
    return (jnp.tile(jnp.concatenate([cos, cos], axis=1), (1, reps)),
            jnp.tile(jnp.concatenate([-sin, sin], axis=1), (1, reps)))


def kernel(x_prompt, x_sample, state_gla, cache_k_meta, cache_v_meta, cache_k_win, cache_v_win, meta_tokens,
           norm_mix, norm_ffn, norm_kv, norm_final, gla_w_in, gla_w_gk, gla_b_gk, gla_norm, gla_w_out, kv_w,
           attn_w_q, attn_sinks, attn_w_out, ffn_w_gate_up, ffn_w_down, moe_w_router, moe_w_gate_up,
           moe_w_down):
    bsz, seq, d = x_prompt.shape
    dbsz, t_new, _ = x_sample.shape
    past_len = 2048
    n_p = bsz * seq
    n_s = dbsz * t_new
    r1 = n_p + n_s
    assert r1 % ROW_TILE == 0 and seq % CHUNK == 0 and t_new == CHUNK
    r0 = -(-(r1 + CHUNK) // ROW_TILE) * ROW_TILE
    meta_row = r1
    hk = GLA_HEADS * GLA_DK
    hv = GLA_HEADS * GLA_DV
    ffn_dense = ffn_w_down.shape[1]
    ffn_expert = moe_w_down.shape[2]
    kv_width = SWA_KV_HEADS * HEAD_DIM

    x0 = jnp.concatenate([x_prompt.reshape(n_p, d), x_sample.reshape(n_s, d), meta_tokens.astype(F32),
                          jnp.zeros((r0 - r1 - N_META, d), F32)], axis=0)

    tm0 = r0 // 16
    tm1 = r1 // 16
    (xn0,) = rms_rows(x0, norm_mix[0:1])
    proj = matmul(xn0, gla_w_in, n_out=2 * hk + 2 * hv, out_dtype=BF16, tm=tm0, name="gla_in_proj")
    w_r = jnp.pad(gla_w_in[:, :, 2 * hk + 2 * hv:], ((0, 0), (0, 0), (0, LANES - GLA_RANK)))
    r128 = matmul(xn0, w_r, n_out=LANES, out_dtype=F32, tm=tm0, tn=LANES, name="gla_rank_proj")
    wgk_pad = jnp.pad(gla_w_gk[0], ((0, LANES - GLA_RANK), (0, 0)))
    bgk = gla_b_gk[0][None, :]
    gnorm = gla_norm[0][None, :]
    og = jnp.zeros((r0, hv), BF16)
    gla = functools.partial(gla_scan, proj, r128, wgk_pad, bgk, gnorm)
    og, s_meta = gla(jnp.zeros((1, GLA_HEADS, GLA_DK, GLA_DV), F32), og, n_seq=1, seq_len=CHUNK,
                     row0=meta_row, n_valid=N_META, s0_per_seq=False)
    og, s_prompt = gla(s_meta, og, n_seq=bsz, seq_len=seq, row0=0, n_valid=CHUNK, s0_per_seq=False)
    og, s_sample = gla(state_gla[0].astype(F32), og, n_seq=dbsz, seq_len=t_new, row0=n_p,
                       n_valid=CHUNK, s0_per_seq=True)
    h1 = matmul(og, gla_w_out, n_out=d, out_dtype=F32, mode="residual", residual=x0, tm=tm0,
                name="gla_out_proj")

    (hn1,) = rms_rows(h1, norm_ffn[0:1])
    act = matmul(hn1, ffn_w_gate_up, n_out=ffn_dense, out_dtype=BF16, mode="swiglu", up_col0=ffn_dense,
                 tm=tm0, name="ffn_gate_up")
    h2 = matmul(act, ffn_w_down, n_out=d, out_dtype=F32, mode="residual", residual=h1, tm=tm0 // 2,
                vmem_mib=56, name="ffn_down")

    pos = np.concatenate([np.tile(N_META + np.arange(seq), bsz),
                          np.tile(N_META + past_len + np.arange(t_new), dbsz),
                          np.arange(N_META), np.zeros(r0 - r1 - N_META, np.int64)])
    rope_tabs = _rope_tables(pos)
    xkv, xq = rms_rows(h2, jnp.stack([norm_kv, norm_mix[1]]))
    kvf = matmul(xkv, kv_w[None], n_out=2 * kv_width, out_dtype=F32, mode="rope", rope_tabs=rope_tabs,
                 rope_cols=kv_width, tm=tm0, name="shared_kv")
    q = matmul(xq, attn_w_q, n_out=d, out_dtype=F32, mode="rope", rope_tabs=rope_tabs, rope_cols=d,
               scale=HEAD_DIM ** -0.5, rows=r1, tm=tm1, name="attn_q")

    q4 = q.reshape(r1, SWA_KV_HEADS, SWA_GROUP, HEAD_DIM)
    sink_tab = jnp.broadcast_to(
        jnp.tile(attn_sinks[0].astype(F32).reshape(SWA_KV_HEADS, 1, SWA_GROUP), (1, CHUNK, 1))
        .reshape(SWA_KV_HEADS, CHUNK * SWA_GROUP, 1), (SWA_KV_HEADS, CHUNK * SWA_GROUP, LANES))
    meta_blk = meta_row // N_META
    o4 = window_attention(
        q4, kvf, kvf, kvf, kvf, sink_tab, jnp.zeros_like(q4), n_seq=bsz, seq_len=seq, key_len=seq, row_block0=0,
        k_map=lambda b, s: (b, 0), v_map=lambda b, s: (b, 1),
        km_map=lambda b, s: (meta_blk, 0), vm_map=lambda b, s: (meta_blk, 1),
        chunks_per_step=4, chunk_offset=0)
    k_new = kvf[n_p:r1, :kv_width]
    v_new = kvf[n_p:r1, kv_width:]
    win = cache_k_win.shape[1]
    ks = jnp.concatenate([cache_k_win.reshape(dbsz, win, kv_width).astype(F32),
                          k_new.reshape(dbsz, t_new, kv_width)], axis=1).reshape(dbsz * (win + t_new), kv_width)
    vs = jnp.concatenate([cache_v_win.reshape(dbsz, win, kv_width).astype(F32),
                          v_new.reshape(dbsz, t_new, kv_width)], axis=1).reshape(dbsz * (win + t_new), kv_width)
    assert win + t_new == (WINDOW_CHUNKS + 1) * CHUNK
    o4 = window_attention(
        q4, ks, vs, cache_k_meta.reshape(dbsz * N_META, kv_width).astype(F32),
        cache_v_meta.reshape(dbsz * N_META, kv_width).astype(F32), sink_tab, o4,
        n_seq=dbsz, seq_len=t_new, key_len=win + t_new, row_block0=n_p // t_new,
        k_map=lambda b, s: (b, 0), v_map=lambda b, s: (b, 0),
        km_map=lambda b, s: (b, 0), vm_map=lambda b, s: (b, 0),
        chunks_per_step=1, chunk_offset=WINDOW_CHUNKS)
    h3 = matmul(o4.reshape(r1, d), attn_w_out, n_out=d, out_dtype=F32, mode="residual", residual=h2,
                rows=r1, tm=tm1, name="attn_out_proj")

    w_router_pad = jnp.pad(moe_w_router[0], ((0, 0), (0, LANES - N_EXPERTS)))
    xp, rt, cnt = route(h3, norm_ffn[1:2], w_router_pad)
    tme = ROW_TILE
    n_tiles_max = -(-(2 * r1 + N_EXPERTS * (tme - 1)) // tme)
    counts = cnt[0, :N_EXPERTS].astype(jnp.int32)
    tiles_per = (counts + tme - 1) // tme
    tile_end = jnp.cumsum(tiles_per)
    starts = (tile_end - tiles_per) * tme
    experts = rt[:, 0:2].astype(jnp.int32)
    ranks = rt[:, 2:4].astype(jnp.int32)
    pos_flat = (starts[experts] + ranks).reshape(-1)
    n_used = tile_end[-1:].astype(jnp.int32)
    tile_expert = jnp.minimum(
        jnp.sum(jnp.arange(n_tiles_max, dtype=jnp.int32)[:, None] >= tile_end[None, :], axis=1),
        N_EXPERTS - 1).astype(jnp.int32)
    xs = scatter_rows(pos_flat, xp, jnp.zeros((n_tiles_max * tme, d // 2), jnp.uint32))
    act_e = matmul(xs, moe_w_gate_up.reshape(N_EXPERTS, d, 2 * ffn_expert), n_out=ffn_expert, out_dtype=BF16,
                   mode="swiglu", up_col0=ffn_expert, tile_expert=tile_expert, n_used=n_used, a_packed=True,
                   name="moe_gate_up")
    half_tile = tme // 2
    ys = matmul(act_e, moe_w_down.reshape(N_EXPERTS, ffn_expert, d), n_out=d, out_dtype=F32,
                tile_expert=jnp.repeat(tile_expert, 2), n_used=2 * n_used, tm=half_tile, vmem_mib=56,
                name="moe_down")
    y_prompt, y_sample = combine_rows(pos_flat, rt, h3, norm_final[None, :], ys, n_prompt_rows=n_p)

    kv_meta = kvf[meta_row:meta_row + N_META]
    k_meta_p = jnp.broadcast_to(kv_meta[None, :, :kv_width], (bsz, N_META, kv_width))
    v_meta_p = jnp.broadcast_to(kv_meta[None, :, kv_width:], (bsz, N_META, kv_width))
    win_p = min(WINDOW_CHUNKS * CHUNK, seq)
    kv_win = kvf[:n_p].reshape(bsz, seq, 2 * kv_width)[:, seq - win_p:]
    shape4 = lambda a: a.reshape(a.shape[0], a.shape[1], SWA_KV_HEADS, HEAD_DIM)
    return (y_prompt.reshape(bsz, seq, d), y_sample.reshape(dbsz, t_new, d),
            s_prompt[None].astype(state_gla.dtype), s_sample[None].astype(state_gla.dtype),
            shape4(k_meta_p), shape4(v_meta_p),
            shape4(kv_win[:, :, :kv_width]), shape4(kv_win[:, :, kv_width:]),
            shape4(k_new.reshape(dbsz, t_new, kv_width)), shape4(v_new.reshape(dbsz, t_new, kv_width)))
```

```python
import functools

import numpy as np
import jax
import jax.numpy as jnp
from jax import lax
from jax.experimental import pallas as pl
from jax.experimental.pallas import tpu as pltpu

F32 = jnp.float32
BF16 = jnp.bfloat16
HIGHEST = lax.Precision.HIGHEST

D_MODEL = 2048
CHUNK = 64
N_META = 16
GLA_HEADS = 4
GLA_DK = 256
GLA_DV = 512
GLA_RANK = 16
GLA_GATE_NORM = 16.0
HEAD_DIM = 64
SWA_HEADS = 32
SWA_KV_HEADS = 4
SWA_GROUP = 8
WINDOW_CHUNKS = 2
ROPE_THETA = 10000.0
N_EXPERTS = 8
EPS = 1e-5
NEG_INF = -1e30

LANES = 128
ROW_TILE = 512
COL_TILE = 512
MIB = 2 ** 20


def _params(semantics, vmem_mib):
    return pltpu.CompilerParams(dimension_semantics=semantics, vmem_limit_bytes=vmem_mib * MIB)


def _norm_kernel(x_ref, g_ref, *o_refs):
    x = x_ref[...]
    y = x * lax.rsqrt(jnp.mean(x * x, axis=-1, keepdims=True) + EPS)
    for i, o_ref in enumerate(o_refs):
        o_ref[...] = (y * g_ref[i:i + 1, :]).astype(o_ref.dtype)


def rms_rows(x, gains):
    rows, d = x.shape
    n = gains.shape[0]
    return pl.pallas_call(
        _norm_kernel,
        grid=(rows // ROW_TILE,),
        in_specs=[pl.BlockSpec((ROW_TILE, d), lambda i: (i, 0)),
                  pl.BlockSpec((n, d), lambda i: (0, 0))],
        out_specs=[pl.BlockSpec((ROW_TILE, d), lambda i: (i, 0)) for _ in range(n)],
        out_shape=[jax.ShapeDtypeStruct((rows, d), BF16) for _ in range(n)],
        compiler_params=_params(("parallel",), 40),
        name="rms_rows",
    )(x, gains)


def _swap_halves(x):
    lane = lax.broadcasted_iota(jnp.int32, x.shape, 1)
    first_half = (lane % HEAD_DIM) < (HEAD_DIM // 2)
    return jnp.where(first_half, pltpu.roll(x, LANES - HEAD_DIM // 2, 1), pltpu.roll(x, HEAD_DIM // 2, 1))


def _mm_kernel(te_ref, nu_ref, *refs, mode, a_packed, rope_cols, scale):
    if mode == "swiglu":
        a_ref, w_ref, w2_ref, o_ref, wb_ref, wb2_ref = refs
    elif mode == "residual":
        a_ref, w_ref, res_ref, o_ref, wb_ref = refs
    elif mode == "rope":
        a_ref, w_ref, cos_ref, sin_ref, o_ref, wb_ref = refs
    else:
        a_ref, w_ref, o_ref, wb_ref = refs
    t = pl.program_id(1)

    @pl.when(t >= nu_ref[0])
    def _():
        o_ref[...] = jnp.zeros_like(o_ref)

    @pl.when(t < nu_ref[0])
    def _():
        @pl.when((t == 0) | (te_ref[t] != te_ref[jnp.maximum(t - 1, 0)]))
        def _():
            wb_ref[...] = w_ref[...].astype(BF16)
            if mode == "swiglu":
                wb2_ref[...] = w2_ref[...].astype(BF16)

        a = a_ref[...]
        if a_packed:
            lo = pltpu.bitcast(a << 16, F32)
            hi = pltpu.bitcast(a & jnp.uint32(0xFFFF0000), F32)
            a = jnp.concatenate([lo, hi], axis=1)
        a = a.astype(BF16)
        acc = jnp.dot(a, wb_ref[...], preferred_element_type=F32)
        if mode == "swiglu":
            up = jnp.dot(a, wb2_ref[...], preferred_element_type=F32)
            o_ref[...] = (acc * jax.nn.sigmoid(acc) * up).astype(o_ref.dtype)
        elif mode == "residual":
            o_ref[...] = (res_ref[...] + acc).astype(o_ref.dtype)
        elif mode == "rope":
            cos = cos_ref[...]
            sin = sin_ref[...]
            for c in range(acc.shape[1] // LANES):
                x = acc[:, c * LANES:(c + 1) * LANES]
                if c * LANES < rope_cols:
                    x = x * cos + _swap_halves(x) * sin
                o_ref[:, c * LANES:(c + 1) * LANES] = (x * scale).astype(o_ref.dtype)
        else:
            o_ref[...] = acc.astype(o_ref.dtype)


def matmul(a, w, *, n_out, out_dtype, mode="plain", tile_expert=None, n_used=None, residual=None,
           rope_tabs=None, rope_cols=0, scale=1.0, col0=0, up_col0=0, tm=ROW_TILE, tn=COL_TILE,
           a_packed=False, rows=None, vmem_mib=48, name="matmul"):
    rows = a.shape[0] if rows is None else rows
    k = w.shape[1]
    n_tiles = rows // tm
    n_col = n_out // tn
    w_mode = pl.Buffered(1) if tile_expert is None else None
    if tile_expert is None:
        tile_expert = jnp.zeros((n_tiles,), jnp.int32)
        n_used = jnp.full((1,), n_tiles, jnp.int32)
    cb0 = col0 // tn
    ub0 = up_col0 // tn

    def row_of(t, nu):
        return jnp.maximum(jnp.minimum(t, nu[0] - 1), 0)

    a_spec = pl.BlockSpec((tm, a.shape[1]), lambda j, t, te, nu: (row_of(t, nu), 0))
    w_spec = pl.BlockSpec((None, k, tn), lambda j, t, te, nu: (te[row_of(t, nu)], 0, cb0 + j),
                          pipeline_mode=w_mode)
    o_spec = pl.BlockSpec((tm, tn), lambda j, t, te, nu: (t, j))
    in_specs = [a_spec, w_spec]
    args = [a, w]
    scratch = [pltpu.VMEM((k, tn), BF16)]
    if mode == "swiglu":
        in_specs.append(pl.BlockSpec((None, k, tn), lambda j, t, te, nu: (te[row_of(t, nu)], 0, ub0 + j),
                                     pipeline_mode=w_mode))
        args.append(w)
        scratch.append(pltpu.VMEM((k, tn), BF16))
    elif mode == "residual":
        in_specs.append(o_spec)
        args.append(residual)
    elif mode == "rope":
        tab_spec = pl.BlockSpec((tm, LANES), lambda j, t, te, nu: (row_of(t, nu), 0))
        in_specs += [tab_spec, tab_spec]
        args += list(rope_tabs)
    kern = functools.partial(_mm_kernel, mode=mode, a_packed=a_packed, rope_cols=rope_cols, scale=scale)
    return pl.pallas_call(
        kern,
        grid_spec=pltpu.PrefetchScalarGridSpec(
            num_scalar_prefetch=2, grid=(n_col, n_tiles),
            in_specs=in_specs, out_specs=o_spec, scratch_shapes=scratch),
        out_shape=jax.ShapeDtypeStruct((rows, n_out), out_dtype),
        compiler_params=_params(("arbitrary", "arbitrary"), vmem_mib),
        name=name,
    )(tile_expert, n_used, *args)


GLA_ROW_BLOCK = 512


def _gla_kernel(q_ref, k_ref, v_ref, go_ref, r_ref, wgk_ref, bgk_ref, gn_ref, s0_ref, og_in_ref,
                o_ref, sfin_ref, st_ref, *, n_chunks, n_valid):
    del og_in_ref
    rb = pl.program_id(1)

    @pl.when(rb == 0)
    def _():
        for h in range(GLA_HEADS):
            st_ref[h] = s0_ref[h].T

    ri = lax.broadcasted_iota(jnp.int32, (CHUNK, CHUNK), 0)
    ci = lax.broadcasted_iota(jnp.int32, (CHUNK, CHUNK), 1)
    causal = ci <= ri
    tril = causal.astype(F32)
    row_valid = lax.broadcasted_iota(jnp.int32, (CHUNK, 1), 0) < n_valid
    wgk = wgk_ref[...]
    bgk = bgk_ref[...]
    gn = gn_ref[...]

    def body(c, carry):
        rows = pl.ds(pl.multiple_of(c * CHUNK, CHUNK), CHUNK)
        z = jnp.dot(r_ref[rows, :], wgk, precision=HIGHEST, preferred_element_type=F32) + bgk
        logsig = jnp.minimum(z, 0.0) - jnp.log(1.0 + jnp.exp(-jnp.abs(z)))
        g = jnp.where(row_valid, logsig / GLA_GATE_NORM, 0.0)
        b = jnp.dot(tril, g, precision=HIGHEST, preferred_element_type=F32)
        b_last = b[CHUNK - 1:CHUNK, :]
        q = q_ref[rows, :].astype(F32) * (GLA_DK ** -0.5)
        k = jnp.where(row_valid, k_ref[rows, :].astype(F32), 0.0)
        q_dec_all = (q * jnp.exp(b)).astype(BF16)
        k_dec_all = (k * jnp.exp(-b)).astype(BF16)
        k_last_all = (k * jnp.exp(b_last - b)).astype(BF16)
        decay = jnp.exp(b_last)
        for h in range(GLA_HEADS):
            ks = slice(h * GLA_DK, (h + 1) * GLA_DK)
            vs = slice(h * GLA_DV, (h + 1) * GLA_DV)
            q_dec = q_dec_all[:, ks]
            v = v_ref[rows, vs]
            att = lax.dot_general(q_dec, k_dec_all[:, ks], (((1,), (1,)), ((), ())),
                                  preferred_element_type=F32)
            att = jnp.where(causal, att, 0.0).astype(BF16)
            st = st_ref[h]
            o = jnp.dot(att, v, preferred_element_type=F32)
            o = o + lax.dot_general(q_dec, st.astype(BF16), (((1,), (1,)), ((), ())),
                                    preferred_element_type=F32)
            st_ref[h] = st * decay[:, ks] + lax.dot_general(
                v, k_last_all[:, ks], (((0,), (0,)), ((), ())), preferred_element_type=F32)
            on = o * lax.rsqrt(jnp.mean(o * o, axis=-1, keepdims=True) + EPS) * gn
            go = go_ref[rows, vs].astype(F32)
            o_ref[rows, vs] = (on * (go * jax.nn.sigmoid(go))).astype(o_ref.dtype)
        return carry

    lax.fori_loop(0, n_chunks, body, 0, unroll=2 if n_chunks % 2 == 0 else 1)

    @pl.when(rb == pl.num_programs(1) - 1)
    def _():
        for h in range(GLA_HEADS):
            sfin_ref[h] = st_ref[h].T


def gla_scan(proj, r128, wgk_pad, bgk, gnorm, s0, og, *, n_seq, seq_len, row0, n_valid, s0_per_seq):
    blk = min(seq_len, GLA_ROW_BLOCK)
    n_rb = seq_len // blk
    hk = GLA_HEADS * GLA_DK
    hv = GLA_HEADS * GLA_DV
    rb0 = row0 // blk
    row = lambda b, r: rb0 + b * n_rb + r
    s0_map = (lambda b, r: (b, 0, 0, 0)) if s0_per_seq else (lambda b, r: (0, 0, 0, 0))
    kern = functools.partial(_gla_kernel, n_chunks=blk // CHUNK, n_valid=n_valid)
    return pl.pallas_call(
        kern,
        grid=(n_seq, n_rb),
        in_specs=[
            pl.BlockSpec((blk, hk), lambda b, r: (row(b, r), 0)),
            pl.BlockSpec((blk, hk), lambda b, r: (row(b, r), 1)),
            pl.BlockSpec((blk, hv), lambda b, r: (row(b, r), 2 * hk // hv)),
            pl.BlockSpec((blk, hv), lambda b, r: (row(b, r), 2 * hk // hv + 1)),
            pl.BlockSpec((blk, LANES), lambda b, r: (row(b, r), 0)),
            pl.BlockSpec((LANES, hk), lambda b, r: (0, 0)),
            pl.BlockSpec((1, hk), lambda b, r: (0, 0)),
            pl.BlockSpec((1, GLA_DV), lambda b, r: (0, 0)),
            pl.BlockSpec((None, GLA_HEADS, GLA_DK, GLA_DV), s0_map),
            pl.BlockSpec(memory_space=pl.ANY),
        ],
        out_specs=[
            pl.BlockSpec((blk, hv), lambda b, r: (row(b, r), 0)),
            pl.BlockSpec((None, GLA_HEADS, GLA_DK, GLA_DV), lambda b, r: (b, 0, 0, 0)),
        ],
        out_shape=[jax.ShapeDtypeStruct(og.shape, og.dtype),
                   jax.ShapeDtypeStruct((n_seq, GLA_HEADS, GLA_DK, GLA_DV), F32)],
        scratch_shapes=[pltpu.VMEM((GLA_HEADS, GLA_DV, GLA_DK), F32)],
        input_output_aliases={9: 0},
        compiler_params=_params(("parallel", "arbitrary"), 48),
        name="gla_scan",
    )(proj, proj, proj, proj, r128, wgk_pad, bgk, gnorm, s0, og)


WIN_KEYS = (WINDOW_CHUNKS + 1) * CHUNK
ATTN_KEYS = 2 * LANES
ATTN_ROW_BLOCK = 512
PAIRS = SWA_GROUP // 2


def _lane_halves(x2, head_in_pair):
    lane = lax.broadcasted_iota(jnp.int32, x2.shape, 1)
    low = lane < HEAD_DIM
    swapped = pltpu.roll(x2, HEAD_DIM, 1)
    if head_in_pair == 0:
        lo, hi = jnp.where(low, x2, 0.0), jnp.where(low, 0.0, swapped)
    else:
        lo, hi = jnp.where(low, swapped, 0.0), jnp.where(low, 0.0, x2)
    return lo.astype(BF16), hi.astype(BF16)


def _attn_kernel(q_ref, k_ref, v_ref, km_ref, vm_ref, sink_ref, o_in_ref, o_ref, kb_ref, vb_ref, kmb_ref, vmb_ref,
                 *, n_chunks, chunk_offset):
    del o_in_ref
    step = pl.program_id(1)

    @pl.when(step == 0)
    def _():
        for h in range(SWA_KV_HEADS):
            cols = slice((h // 2) * LANES, (h // 2 + 1) * LANES)
            for src, dst in ((k_ref, kb_ref), (v_ref, vb_ref), (km_ref, kmb_ref), (vm_ref, vmb_ref)):
                lo, hi = _lane_halves(src[:, cols], h % 2)
                dst[h, 0] = lo
                dst[h, 1] = hi

    j = lax.broadcasted_iota(jnp.int32, (1, ATTN_KEYS), 1)
    rel_chunk = (j >= N_META + CHUNK).astype(jnp.int32) + (j >= N_META + 2 * CHUNK).astype(jnp.int32)
    in_window = (j >= N_META) & (j < N_META + WIN_KEYS)
    low_lanes = lax.broadcasted_iota(jnp.int32, (1, LANES), 1) < HEAD_DIM
    zpad = jnp.zeros((ATTN_KEYS - N_META - WIN_KEYS, LANES), BF16)
    key_row = lax.broadcasted_iota(jnp.int32, (2 * ATTN_KEYS, LANES), 0)
    key_lane = lax.broadcasted_iota(jnp.int32, (2 * ATTN_KEYS, LANES), 1)
    ones_cols = ((key_row < ATTN_KEYS) == (key_lane < HEAD_DIM)).astype(BF16)
    nt = (((1,), (1,)), ((), ()))

    def body(ci, carry):
        c = step * n_chunks + ci + chunk_offset
        wc = jnp.maximum(c - WINDOW_CHUNKS, 0)
        win = pl.ds(pl.multiple_of(wc * CHUNK, CHUNK), WIN_KEYS)
        rows = pl.ds(pl.multiple_of(ci * CHUNK, CHUNK), CHUNK)
        valid = (j < N_META) | (in_window & (wc + rel_chunk <= c))
        for h in range(SWA_KV_HEADS):
            kb = jnp.concatenate([kmb_ref[h, 0], kb_ref[h, 0, win, :], zpad,
                                  kmb_ref[h, 1], kb_ref[h, 1, win, :], zpad], axis=0)
            vb = jnp.concatenate([vmb_ref[h, 0], vb_ref[h, 0, win, :], zpad,
                                  vmb_ref[h, 1], vb_ref[h, 1, win, :], zpad], axis=0)
            col = lambda p: slice((h * PAIRS + p) * LANES, (h * PAIRS + p + 1) * LANES)
            qs = jnp.concatenate([q_ref[rows, col(p)] for p in range(PAIRS)], axis=0)
            s = lax.dot_general(qs, kb, nt, preferred_element_type=F32)
            probs, sink_terms = [], []
            for half in range(2):
                sh = jnp.where(valid, s[:, half * ATTN_KEYS:(half + 1) * ATTN_KEYS], NEG_INF)
                sink = sink_ref[h, half][:, 0:1]
                m = jnp.maximum(jnp.max(sh, axis=-1, keepdims=True), sink)
                probs.append(jnp.exp(sh - m).astype(BF16))
                sink_terms.append(jnp.exp(sink - m))
            ov = jnp.dot(jnp.concatenate(probs, axis=1), jnp.concatenate([vb, ones_cols], axis=1),
                         preferred_element_type=F32)
            den = ov[:, LANES:] + jnp.where(low_lanes, sink_terms[0], sink_terms[1])
            o = ov[:, :LANES] / den
            for p in range(PAIRS):
                o_ref[rows, col(p)] = o[p * CHUNK:(p + 1) * CHUNK].astype(o_ref.dtype)
        return carry

    lax.fori_loop(0, n_chunks, body, 0, unroll=4 if n_chunks % 4 == 0 else 1)


def window_attention(q, k_arr, v_arr, km_arr, vm_arr, sink_tab, o, *, n_seq, seq_len, key_len, row0,
                     k_map, v_map, km_map, vm_map, chunk_offset):
    blk = min(seq_len, ATTN_ROW_BLOCK)
    steps = seq_len // blk
    q_spec = pl.BlockSpec((blk, q.shape[1]), lambda b, s: (row0 // blk + b * steps + s, 0))
    width = SWA_KV_HEADS * HEAD_DIM
    kern = functools.partial(_attn_kernel, n_chunks=blk // CHUNK, chunk_offset=chunk_offset)
    return pl.pallas_call(
        kern,
        grid=(n_seq, steps),
        in_specs=[q_spec,
                  pl.BlockSpec((key_len, width), k_map),
                  pl.BlockSpec((key_len, width), v_map),
                  pl.BlockSpec((N_META, width), km_map),
                  pl.BlockSpec((N_META, width), vm_map),
                  pl.BlockSpec((SWA_KV_HEADS, 2, PAIRS * CHUNK, LANES), lambda b, s: (0, 0, 0, 0)),
                  pl.BlockSpec(memory_space=pl.ANY)],
        out_specs=q_spec,
        out_shape=jax.ShapeDtypeStruct(o.shape, o.dtype),
        scratch_shapes=[pltpu.VMEM((SWA_KV_HEADS, 2, key_len, LANES), BF16),
                        pltpu.VMEM((SWA_KV_HEADS, 2, key_len, LANES), BF16),
                        pltpu.VMEM((SWA_KV_HEADS, 2, N_META, LANES), BF16),
                        pltpu.VMEM((SWA_KV_HEADS, 2, N_META, LANES), BF16)],
        input_output_aliases={6: 0},
        compiler_params=_params(("parallel", "arbitrary"), 48),
        name="window_attention",
    )(q, k_arr, v_arr, km_arr, vm_arr, sink_tab, o)


def _router_kernel(x_ref, g_ref, wr_ref, xp_ref, rt_ref, cnt_ref, carry_ref):
    i = pl.program_id(0)

    @pl.when(i == 0)
    def _():
        carry_ref[...] = jnp.zeros_like(carry_ref)

    x = x_ref[...]
    xn = x * lax.rsqrt(jnp.mean(x * x, axis=-1, keepdims=True) + EPS) * g_ref[...]
    bits = pltpu.bitcast(xn.astype(BF16).astype(F32), jnp.uint32)
    half = D_MODEL // 2
    xp_ref[...] = (bits[:, :half] >> 16) | (bits[:, half:] & jnp.uint32(0xFFFF0000))

    tm = x.shape[0]
    logits = jnp.dot(xn, wr_ref[...], precision=HIGHEST, preferred_element_type=F32)
    lane = lax.broadcasted_iota(jnp.int32, (tm, LANES), 1).astype(F32)
    neg = jnp.float32(-jnp.inf)
    lg = jnp.where(lane < N_EXPERTS, logits, neg)
    m1 = jnp.max(lg, axis=-1, keepdims=True)
    i1 = jnp.min(jnp.where(lg == m1, lane, float(LANES)), axis=-1, keepdims=True)
    lg2 = jnp.where(lane == i1, neg, lg)
    m2 = jnp.max(lg2, axis=-1, keepdims=True)
    i2 = jnp.min(jnp.where(lg2 == m2, lane, float(LANES)), axis=-1, keepdims=True)
    e21 = jnp.exp(m2 - m1)
    g1 = 1.0 / (1.0 + e21)
    g2 = e21 / (1.0 + e21)
    oh1 = (lane == i1).astype(F32)
    oh2 = (lane == i2).astype(F32)
    oh = oh1 + oh2
    ri = lax.broadcasted_iota(jnp.int32, (tm, tm), 0)
    ci = lax.broadcasted_iota(jnp.int32, (tm, tm), 1)
    before = (ci < ri).astype(BF16)
    prefix = jnp.dot(before, oh.astype(BF16), preferred_element_type=F32) + carry_ref[0:1, :]
    rank1 = jnp.sum(prefix * oh1, axis=-1, keepdims=True)
    rank2 = jnp.sum(prefix * oh2, axis=-1, keepdims=True)
    total = carry_ref[0:1, :] + jnp.sum(oh, axis=0, keepdims=True)
    carry_ref[...] = jnp.broadcast_to(total, carry_ref.shape)
    cnt_ref[...] = jnp.broadcast_to(total, cnt_ref.shape)
    rt = jnp.where(lane == 0, i1, 0.0)
    rt = jnp.where(lane == 1, i2, rt)
    rt = jnp.where(lane == 2, rank1, rt)
    rt = jnp.where(lane == 3, rank2, rt)
    rt = jnp.where(lane == 4, g1, rt)
    rt = jnp.where(lane == 5, g2, rt)
    rt_ref[...] = rt


def route(h, gain, w_router_pad):
    rows = h.shape[0]
    return pl.pallas_call(
        _router_kernel,
        grid=(rows // ROW_TILE,),
        in_specs=[pl.BlockSpec((ROW_TILE, D_MODEL), lambda i: (i, 0)),
                  pl.BlockSpec((1, D_MODEL), lambda i: (0, 0)),
                  pl.BlockSpec((D_MODEL, LANES), lambda i: (0, 0))],
        out_specs=[pl.BlockSpec((ROW_TILE, D_MODEL // 2), lambda i: (i, 0)),
                   pl.BlockSpec((ROW_TILE, LANES), lambda i: (i, 0)),
                   pl.BlockSpec((8, LANES), lambda i: (0, 0))],
        out_shape=[jax.ShapeDtypeStruct((rows, D_MODEL // 2), jnp.uint32),
                   jax.ShapeDtypeStruct((rows, LANES), F32),
                   jax.ShapeDtypeStruct((8, LANES), F32)],
        scratch_shapes=[pltpu.VMEM((8, LANES), F32)],
        compiler_params=_params(("arbitrary",), 40),
        name="route",
    )(h, gain, w_router_pad)


def _scatter_kernel(pos_ref, xp_ref, xs_in_ref, xs_ref, sem):
    del xs_in_ref
    tm = xp_ref.shape[0]

    def row_copy(r, slot):
        return pltpu.make_async_copy(xp_ref.at[pl.ds(r, 1), :], xs_ref.at[pl.ds(slot, 1), :], sem)

    def issue(r, carry):
        row_copy(r, pos_ref[2 * r]).start()
        row_copy(r, pos_ref[2 * r + 1]).start()
        return carry

    def drain(r, carry):
        row_copy(r, pos_ref[2 * r]).wait()
        row_copy(r, pos_ref[2 * r + 1]).wait()
        return carry

    lax.fori_loop(0, tm, issue, 0)
    lax.fori_loop(0, tm, drain, 0)


def scatter_rows(pos_flat, xp, xs_init):
    rows = xp.shape[0]
    return pl.pallas_call(
        _scatter_kernel,
        grid=(rows // ROW_TILE,),
        in_specs=[pl.BlockSpec((2 * ROW_TILE,), lambda i: (i,), memory_space=pltpu.SMEM),
                  pl.BlockSpec((ROW_TILE, xp.shape[1]), lambda i: (i, 0)),
                  pl.BlockSpec(memory_space=pl.ANY)],
        out_specs=pl.BlockSpec(memory_space=pl.ANY),
        out_shape=jax.ShapeDtypeStruct(xs_init.shape, xs_init.dtype),
        scratch_shapes=[pltpu.SemaphoreType.DMA(())],
        input_output_aliases={2: 0},
        compiler_params=_params(("arbitrary",), 32),
        name="scatter_rows",
    )(pos_flat, xp, xs_init)


def _combine_kernel(pos_ref, rt_ref, h_ref, g_ref, ys_ref, yp_ref, ysm_ref, buf_ref, sem, *, n_prompt_tiles):
    i = pl.program_id(0)
    tm = h_ref.shape[0]

    def row_copy(r, k, slot):
        return pltpu.make_async_copy(ys_ref.at[pl.ds(slot, 1), :], buf_ref.at[k, pl.ds(r, 1), :], sem)

    def issue(r, carry):
        row_copy(r, 0, pos_ref[2 * r]).start()
        row_copy(r, 1, pos_ref[2 * r + 1]).start()
        return carry

    def drain(r, carry):
        row_copy(r, 0, pos_ref[2 * r]).wait()
        row_copy(r, 1, pos_ref[2 * r + 1]).wait()
        return carry

    lax.fori_loop(0, tm, issue, 0)
    lax.fori_loop(0, tm, drain, 0)
    rt = rt_ref[...]
    y = rt[:, 4:5] * buf_ref[0] + rt[:, 5:6] * buf_ref[1]
    x = h_ref[...] + y
    out = x * lax.rsqrt(jnp.mean(x * x, axis=-1, keepdims=True) + EPS) * g_ref[...]

    @pl.when(i < n_prompt_tiles)
    def _():
        yp_ref[...] = out

    @pl.when(i >= n_prompt_tiles)
    def _():
        ysm_ref[...] = out


def combine_rows(pos_flat, rt, h, gain, ys, *, n_prompt_rows):
    rows = h.shape[0]
    npt = n_prompt_rows // ROW_TILE
    kern = functools.partial(_combine_kernel, n_prompt_tiles=npt)
    return pl.pallas_call(
        kern,
        grid=(rows // ROW_TILE,),
        in_specs=[pl.BlockSpec((2 * ROW_TILE,), lambda i: (i,), memory_space=pltpu.SMEM),
                  pl.BlockSpec((ROW_TILE, LANES), lambda i: (i, 0)),
                  pl.BlockSpec((ROW_TILE, D_MODEL), lambda i: (i, 0)),
                  pl.BlockSpec((1, D_MODEL), lambda i: (0, 0)),
                  pl.BlockSpec(memory_space=pl.ANY)],
        out_specs=[pl.BlockSpec((ROW_TILE, D_MODEL), lambda i: (jnp.minimum(i, npt - 1), 0)),
                   pl.BlockSpec((ROW_TILE, D_MODEL), lambda i: (jnp.maximum(i - npt, 0), 0))],
        out_shape=[jax.ShapeDtypeStruct((n_prompt_rows, D_MODEL), F32),
                   jax.ShapeDtypeStruct((rows - n_prompt_rows, D_MODEL), F32)],
        scratch_shapes=[pltpu.VMEM((2, ROW_TILE, D_MODEL), F32), pltpu.SemaphoreType.DMA(())],
        compiler_params=_params(("arbitrary",), 48),
        name="combine_rows",
    )(pos_flat, rt, h, gain, ys)


def _rope_tables(pos):
    half = HEAD_DIM // 2
    inv_freq = ROPE_THETA ** (-jnp.arange(half, dtype=F32) / half)
    ang = jnp.asarray(pos, jnp.int32).astype(F32)[:, None] * inv_freq[None, :]
    cos = jnp.cos(ang)
    sin = jnp.sin(ang)
    reps = LANES // HEAD_DIM
    return (jnp.tile(jnp.concatenate([cos, cos], axis=1), (1, reps)),
            jnp.tile(jnp.concatenate([-sin, sin], axis=1), (1, reps)))


def kernel(x_prompt, x_sample, state_gla, cache_k_meta, cache_v_meta, cache_k_win, cache_v_win, meta_tokens,
           norm_mix, norm_ffn, norm_kv, norm_final, gla_w_in, gla_w_gk, gla_b_gk, gla_norm, gla_w_out, kv_w,
           attn_w_q, attn_sinks, attn_w_out, ffn_w_gate_up, ffn_w_down, moe_w_router, moe_w_gate_up,
           moe_w_down):
    bsz, seq, d = x_prompt.shape
    dbsz, t_new, _ = x_sample.shape
    past_len = 2048
    n_p = bsz * seq
    n_s = dbsz * t_new
    r1 = n_p + n_s
    assert r1 % ROW_TILE == 0 and seq % CHUNK == 0 and t_new == CHUNK
    r0 = -(-(r1 + CHUNK) // ROW_TILE) * ROW_TILE
    meta_row = r1
    hk = GLA_HEADS * GLA_DK
    hv = GLA_HEADS * GLA_DV
    ffn_dense = ffn_w_down.shape[1]
    ffn_expert = moe_w_down.shape[2]
    kv_width = SWA_KV_HEADS * HEAD_DIM

    x0 = jnp.concatenate([x_prompt.reshape(n_p, d), x_sample.reshape(n_s, d), meta_tokens.astype(F32),
                          jnp.zeros((r0 - r1 - N_META, d), F32)], axis=0)

    tm0 = r0 // 16
    tm1 = r1 // 16
    (xn0,) = rms_rows(x0, norm_mix[0:1])
    proj = matmul(xn0, gla_w_in, n_out=2 * hk + 2 * hv, out_dtype=BF16, tm=tm0, name="gla_in_proj")
    w_r = jnp.pad(gla_w_in[:, :, 2 * hk + 2 * hv:], ((0, 0), (0, 0), (0, LANES - GLA_RANK)))
    r128 = matmul(xn0, w_r, n_out=LANES, out_dtype=F32, tm=tm0, tn=LANES, name="gla_rank_proj")
    wgk_pad = jnp.pad(gla_w_gk[0], ((0, LANES - GLA_RANK), (0, 0)))
    bgk = gla_b_gk[0][None, :]
    gnorm = gla_norm[0][None, :]
    og = jnp.zeros((r0, hv), BF16)
    gla = functools.partial(gla_scan, proj, r128, wgk_pad, bgk, gnorm)
    og, s_meta = gla(jnp.zeros((1, GLA_HEADS, GLA_DK, GLA_DV), F32), og, n_seq=1, seq_len=CHUNK,
                     row0=meta_row, n_valid=N_META, s0_per_seq=False)
    og, s_prompt = gla(s_meta, og, n_seq=bsz, seq_len=seq, row0=0, n_valid=CHUNK, s0_per_seq=False)
    og, s_sample = gla(state_gla[0].astype(F32), og, n_seq=dbsz, seq_len=t_new, row0=n_p,
                       n_valid=CHUNK, s0_per_seq=True)
    h1 = matmul(og, gla_w_out, n_out=d, out_dtype=F32, mode="residual", residual=x0, tm=tm0,
                name="gla_out_proj")

    (hn1,) = rms_rows(h1, norm_ffn[0:1])
    act = matmul(hn1, ffn_w_gate_up, n_out=ffn_dense, out_dtype=BF16, mode="swiglu", up_col0=ffn_dense,
                 tm=tm0, name="ffn_gate_up")
    h2 = matmul(act, ffn_w_down, n_out=d, out_dtype=F32, mode="residual", residual=h1, tm=tm0 // 2,
                vmem_mib=56, name="ffn_down")

    pos = np.concatenate([np.tile(N_META + np.arange(seq), bsz),
                          np.tile(N_META + past_len + np.arange(t_new), dbsz),
                          np.arange(N_META), np.zeros(r0 - r1 - N_META, np.int64)])
    rope_tabs = _rope_tables(pos)
    xkv, xq = rms_rows(h2, jnp.stack([norm_kv, norm_mix[1]]))
    kvf = matmul(xkv, kv_w[None], n_out=2 * kv_width, out_dtype=F32, mode="rope", rope_tabs=rope_tabs,
                 rope_cols=kv_width, tm=tm0, name="shared_kv")
    q = matmul(xq, attn_w_q, n_out=d, out_dtype=BF16, mode="rope", rope_tabs=rope_tabs, rope_cols=d,
               scale=HEAD_DIM ** -0.5, rows=r1, tm=tm1, name="attn_q")

    sink_tab = jnp.broadcast_to(
        jnp.repeat(attn_sinks[0].astype(F32).reshape(SWA_KV_HEADS, PAIRS, 2).transpose(0, 2, 1), CHUNK, axis=2)
        [..., None], (SWA_KV_HEADS, 2, PAIRS * CHUNK, LANES))
    meta_blk = meta_row // N_META
    o_att = window_attention(
        q, kvf, kvf, kvf, kvf, sink_tab, jnp.zeros((r1, d), BF16), n_seq=bsz, seq_len=seq, key_len=seq, row0=0,
        k_map=lambda b, s: (b, 0), v_map=lambda b, s: (b, 1),
        km_map=lambda b, s: (meta_blk, 0), vm_map=lambda b, s: (meta_blk, 1), chunk_offset=0)
    k_new = kvf[n_p:r1, :kv_width]
    v_new = kvf[n_p:r1, kv_width:]
    win = cache_k_win.shape[1]
    ks = jnp.concatenate([cache_k_win.reshape(dbsz, win, kv_width).astype(F32),
                          k_new.reshape(dbsz, t_new, kv_width)], axis=1).reshape(dbsz * (win + t_new), kv_width)
    vs = jnp.concatenate([cache_v_win.reshape(dbsz, win, kv_width).astype(F32),
                          v_new.reshape(dbsz, t_new, kv_width)], axis=1).reshape(dbsz * (win + t_new), kv_width)
    assert win + t_new == (WINDOW_CHUNKS + 1) * CHUNK
    o_att = window_attention(
        q, ks, vs, cache_k_meta.reshape(dbsz * N_META, kv_width).astype(F32),
        cache_v_meta.reshape(dbsz * N_META, kv_width).astype(F32), sink_tab, o_att,
        n_seq=dbsz, seq_len=t_new, key_len=win + t_new, row0=n_p,
        k_map=lambda b, s: (b, 0), v_map=lambda b, s: (b, 0),
        km_map=lambda b, s: (b, 0), vm_map=lambda b, s: (b, 0), chunk_offset=WINDOW_CHUNKS)
    h3 = matmul(o_att, attn_w_out, n_out=d, out_dtype=F32, mode="residual", residual=h2,
                rows=r1, tm=tm1, name="attn_out_proj")

    w_router_pad = jnp.pad(moe_w_router[0], ((0, 0), (0, LANES - N_EXPERTS)))
    xp, rt, cnt = route(h3, norm_ffn[1:2], w_router_pad)
    tme = ROW_TILE
    n_tiles_max = -(-(2 * r1 + N_EXPERTS * (tme - 1)) // tme)
    counts = cnt[0, :N_EXPERTS].astype(jnp.int32)
    tiles_per = (counts + tme - 1) // tme
    tile_end = jnp.cumsum(tiles_per)
    starts = (tile_end - tiles_per) * tme
    experts = rt[:, 0:2].astype(jnp.int32)
    ranks = rt[:, 2:4].astype(jnp.int32)
    pos_flat = (starts[experts] + ranks).reshape(-1)
    n_used = tile_end[-1:].astype(jnp.int32)
    tile_expert = jnp.minimum(
        jnp.sum(jnp.arange(n_tiles_max, dtype=jnp.int32)[:, None] >= tile_end[None, :], axis=1),
        N_EXPERTS - 1).astype(jnp.int32)
    xs = scatter_rows(pos_flat, xp, jnp.zeros((n_tiles_max * tme, d // 2), jnp.uint32))
    act_e = matmul(xs, moe_w_gate_up.reshape(N_EXPERTS, d, 2 * ffn_expert), n_out=ffn_expert, out_dtype=BF16,
                   mode="swiglu", up_col0=ffn_expert, tile_expert=tile_expert, n_used=n_used, a_packed=True,
                   name="moe_gate_up")
    half_tile = tme // 2
    ys = matmul(act_e, moe_w_down.reshape(N_EXPERTS, ffn_expert, d), n_out=d, out_dtype=F32,
                tile_expert=jnp.repeat(tile_expert, 2), n_used=2 * n_used, tm=half_tile, vmem_mib=56,
                name="moe_down")
    y_prompt, y_sample = combine_rows(pos_flat, rt, h3, norm_final[None, :], ys, n_prompt_rows=n_p)

    kv_meta = kvf[meta_row:meta_row + N_META]
    k_meta_p = jnp.broadcast_to(kv_meta[None, :, :kv_width], (bsz, N_META, kv_width))
    v_meta_p = jnp.broadcast_to(kv_meta[None, :, kv_width:], (bsz, N_META, kv_width))
    win_p = min(WINDOW_CHUNKS * CHUNK, seq)
    kv_win = kvf[:n_p].reshape(bsz, seq, 2 * kv_width)[:, seq - win_p:]
    shape4 = lambda a: a.reshape(a.shape[0], a.shape[1], SWA_KV_HEADS, HEAD_DIM)
    return (y_prompt.reshape(bsz, seq, d), y_sample.reshape(dbsz, t_new, d),
            s_prompt[None].astype(state_gla.dtype), s_sample[None].astype(state_gla.dtype),
            shape4(k_meta_p), shape4(v_meta_p),
            shape4(kv_win[:, :, :kv_width]), shape4(kv_win[:, :, kv_width:]),
            shape4(k_new.reshape(dbsz, t_new, kv_width)), shape4(v_new.reshape(dbsz, t_new, kv_width)))
```

```python
import functools

import numpy as np
import jax
import jax.numpy as jnp
from jax import lax
from jax.experimental import pallas as pl
from jax.experimental.pallas import tpu as pltpu

F32 = jnp.float32
BF16 = jnp.bfloat16
HIGHEST = lax.Precision.HIGHEST

D_MODEL = 2048
CHUNK = 64
N_META = 16
GLA_HEADS = 4
GLA_DK = 256
GLA_DV = 512
GLA_RANK = 16
GLA_GATE_NORM = 16.0
HEAD_DIM = 64
SWA_HEADS = 32
SWA_KV_HEADS = 4
SWA_GROUP = 8
WINDOW_CHUNKS = 2
ROPE_THETA = 10000.0
N_EXPERTS = 8
EPS = 1e-5
NEG_INF = -1e30

LANES = 128
ROW_TILE = 512
COL_TILE = 512
MOE_TILE = 1024
MOE_UNIT = 512
MOE_DOWN_TILE = 256
MIB = 2 ** 20


def _params(semantics, vmem_mib):
    return pltpu.CompilerParams(dimension_semantics=semantics, vmem_limit_bytes=vmem_mib * MIB)


def _norm_kernel(x_ref, g_ref, *o_refs):
    x = x_ref[...]
    y = x * lax.rsqrt(jnp.mean(x * x, axis=-1, keepdims=True) + EPS)
    for i, o_ref in enumerate(o_refs):
        o_ref[...] = (y * g_ref[i:i + 1, :]).astype(o_ref.dtype)


def rms_rows(x, gains):
    rows, d = x.shape
    n = gains.shape[0]
    return pl.pallas_call(
        _norm_kernel,
        grid=(rows // ROW_TILE,),
        in_specs=[pl.BlockSpec((ROW_TILE, d), lambda i: (i, 0)),
                  pl.BlockSpec((n, d), lambda i: (0, 0))],
        out_specs=[pl.BlockSpec((ROW_TILE, d), lambda i: (i, 0)) for _ in range(n)],
        out_shape=[jax.ShapeDtypeStruct((rows, d), BF16) for _ in range(n)],
        compiler_params=_params(("parallel",), 40),
        name="rms_rows",
    )(x, gains)


def _swap_halves(x):
    lane = lax.broadcasted_iota(jnp.int32, x.shape, 1)
    first_half = (lane % HEAD_DIM) < (HEAD_DIM // 2)
    return jnp.where(first_half, pltpu.roll(x, LANES - HEAD_DIM // 2, 1), pltpu.roll(x, HEAD_DIM // 2, 1))


def _mm_kernel(te_ref, nv_ref, nu_ref, *refs, mode, a_packed, rope_cols, scale, sub_tiles):
    if mode == "swiglu":
        a_ref, w_ref, w2_ref, o_ref, wb_ref, wb2_ref = refs
    elif mode == "residual":
        a_ref, w_ref, res_ref, o_ref, wb_ref = refs
    elif mode == "rope":
        a_ref, w_ref, cos_ref, sin_ref, o_ref, wb_ref = refs
    else:
        a_ref, w_ref, o_ref, wb_ref = refs
    t = pl.program_id(1)
    tm = o_ref.shape[0]
    sub = tm // sub_tiles

    def compute(n_rows):
        rows = slice(0, n_rows)
        a = a_ref[rows, :]
        if a_packed:
            lo = pltpu.bitcast(a << 16, F32)
            hi = pltpu.bitcast(a & jnp.uint32(0xFFFF0000), F32)
            a = jnp.concatenate([lo, hi], axis=1)
        a = a.astype(BF16)
        acc = jnp.dot(a, wb_ref[...], preferred_element_type=F32)
        if mode == "swiglu":
            up = jnp.dot(a, wb2_ref[...], preferred_element_type=F32)
            o_ref[rows, :] = (acc * jax.nn.sigmoid(acc) * up).astype(o_ref.dtype)
        elif mode == "residual":
            o_ref[rows, :] = (res_ref[rows, :] + acc).astype(o_ref.dtype)
        elif mode == "rope":
            cos = cos_ref[rows, :]
            sin = sin_ref[rows, :]
            for c in range(acc.shape[1] // LANES):
                x = acc[:, c * LANES:(c + 1) * LANES]
                if c * LANES < rope_cols:
                    x = x * cos + _swap_halves(x) * sin
                o_ref[rows, c * LANES:(c + 1) * LANES] = (x * scale).astype(o_ref.dtype)
        else:
            o_ref[rows, :] = acc.astype(o_ref.dtype)
        if n_rows < tm:
            o_ref[n_rows:, :] = jnp.zeros((tm - n_rows, o_ref.shape[1]), o_ref.dtype)

    @pl.when(t < nu_ref[0])
    def _():
        @pl.when((t == 0) | (te_ref[t] != te_ref[jnp.maximum(t - 1, 0)]))
        def _():
            wb_ref[...] = w_ref[...].astype(BF16)
            if mode == "swiglu":
                wb2_ref[...] = w2_ref[...].astype(BF16)

    n_valid = jnp.where(t < nu_ref[0], nv_ref[t], 0)

    @pl.when(n_valid == 0)
    def _():
        o_ref[...] = jnp.zeros_like(o_ref)

    for s in range(1, sub_tiles + 1):
        pl.when(n_valid == s)(functools.partial(compute, s * sub))


def matmul(a, w, *, n_out, out_dtype, mode="plain", schedule=None, sub_tiles=1, residual=None,
           rope_tabs=None, rope_cols=0, scale=1.0, col0=0, up_col0=0, tm=ROW_TILE, tn=COL_TILE,
           a_packed=False, rows=None, vmem_mib=48, name="matmul"):
    rows = a.shape[0] if rows is None else rows
    k = w.shape[1]
    n_tiles = rows // tm
    n_col = n_out // tn
    w_mode = pl.Buffered(1) if schedule is None else None
    if schedule is None:
        schedule = (jnp.zeros((n_tiles,), jnp.int32), jnp.full((n_tiles,), sub_tiles, jnp.int32),
                    jnp.full((1,), n_tiles, jnp.int32))
    cb0 = col0 // tn
    ub0 = up_col0 // tn

    def row_of(t, nu):
        return jnp.maximum(jnp.minimum(t, nu[0] - 1), 0)

    a_spec = pl.BlockSpec((tm, a.shape[1]), lambda j, t, te, nv, nu: (row_of(t, nu), 0))
    w_spec = pl.BlockSpec((None, k, tn), lambda j, t, te, nv, nu: (te[row_of(t, nu)], 0, cb0 + j),
                          pipeline_mode=w_mode)
    o_spec = pl.BlockSpec((tm, tn), lambda j, t, te, nv, nu: (t, j))
    in_specs = [a_spec, w_spec]
    args = [a, w]
    scratch = [pltpu.VMEM((k, tn), BF16)]
    if mode == "swiglu":
        in_specs.append(pl.BlockSpec((None, k, tn), lambda j, t, te, nv, nu: (te[row_of(t, nu)], 0, ub0 + j),
                                     pipeline_mode=w_mode))
        args.append(w)
        scratch.append(pltpu.VMEM((k, tn), BF16))
    elif mode == "residual":
        in_specs.append(o_spec)
        args.append(residual)
    elif mode == "rope":
        tab_spec = pl.BlockSpec((tm, LANES), lambda j, t, te, nv, nu: (row_of(t, nu), 0))
        in_specs += [tab_spec, tab_spec]
        args += list(rope_tabs)
    kern = functools.partial(_mm_kernel, mode=mode, a_packed=a_packed, rope_cols=rope_cols, scale=scale,
                             sub_tiles=sub_tiles)
    return pl.pallas_call(
        kern,
        grid_spec=pltpu.PrefetchScalarGridSpec(
            num_scalar_prefetch=3, grid=(n_col, n_tiles),
            in_specs=in_specs, out_specs=o_spec, scratch_shapes=scratch),
        out_shape=jax.ShapeDtypeStruct((rows, n_out), out_dtype),
        compiler_params=_params(("arbitrary", "arbitrary"), vmem_mib),
        name=name,
    )(*schedule, *args)


GLA_ROW_BLOCK = 512


def _gla_kernel(q_ref, k_ref, v_ref, go_ref, r_ref, wgk_ref, bgk_ref, gn_ref, s0_ref, og_in_ref,
                o_ref, sfin_ref, st_ref, *, n_chunks, n_valid):
    del og_in_ref
    rb = pl.program_id(1)

    @pl.when(rb == 0)
    def _():
        for h in range(GLA_HEADS):
            st_ref[h] = s0_ref[h].T

    ri = lax.broadcasted_iota(jnp.int32, (CHUNK, CHUNK), 0)
    ci = lax.broadcasted_iota(jnp.int32, (CHUNK, CHUNK), 1)
    causal = ci <= ri
    tril = causal.astype(F32)
    row_valid = lax.broadcasted_iota(jnp.int32, (CHUNK, 1), 0) < n_valid
    wgk = wgk_ref[...]
    bgk = bgk_ref[...]
    gn = gn_ref[...]

    def body(c, carry):
        rows = pl.ds(pl.multiple_of(c * CHUNK, CHUNK), CHUNK)
        z = jnp.dot(r_ref[rows, :], wgk, precision=HIGHEST, preferred_element_type=F32) + bgk
        logsig = jnp.minimum(z, 0.0) - jnp.log(1.0 + jnp.exp(-jnp.abs(z)))
        g = jnp.where(row_valid, logsig / GLA_GATE_NORM, 0.0)
        b = jnp.dot(tril, g, precision=HIGHEST, preferred_element_type=F32)
        b_last = b[CHUNK - 1:CHUNK, :]
        q = q_ref[rows, :].astype(F32) * (GLA_DK ** -0.5)
        k = jnp.where(row_valid, k_ref[rows, :].astype(F32), 0.0)
        q_dec_all = (q * jnp.exp(b)).astype(BF16)
        k_dec_all = (k * jnp.exp(-b)).astype(BF16)
        k_last_all = (k * jnp.exp(b_last - b)).astype(BF16)
        decay = jnp.exp(b_last)
        for h in range(GLA_HEADS):
            ks = slice(h * GLA_DK, (h + 1) * GLA_DK)
            vs = slice(h * GLA_DV, (h + 1) * GLA_DV)
            q_dec = q_dec_all[:, ks]
            v = v_ref[rows, vs]
            att = lax.dot_general(q_dec, k_dec_all[:, ks], (((1,), (1,)), ((), ())),
                                  preferred_element_type=F32)
            att = jnp.where(causal, att, 0.0).astype(BF16)
            st = st_ref[h]
            o = jnp.dot(att, v, preferred_element_type=F32)
            o = o + lax.dot_general(q_dec, st.astype(BF16), (((1,), (1,)), ((), ())),
                                    preferred_element_type=F32)
            st_ref[h] = st * decay[:, ks] + lax.dot_general(
                v, k_last_all[:, ks], (((0,), (0,)), ((), ())), preferred_element_type=F32)
            on = o * lax.rsqrt(jnp.mean(o * o, axis=-1, keepdims=True) + EPS) * gn
            go = go_ref[rows, vs].astype(F32)
            o_ref[rows, vs] = (on * (go * jax.nn.sigmoid(go))).astype(o_ref.dtype)
        return carry

    lax.fori_loop(0, n_chunks, body, 0, unroll=2 if n_chunks % 2 == 0 else 1)

    @pl.when(rb == pl.num_programs(1) - 1)
    def _():
        for h in range(GLA_HEADS):
            sfin_ref[h] = st_ref[h].T


def gla_scan(proj, r128, wgk_pad, bgk, gnorm, s0, og, *, n_seq, seq_len, row0, n_valid, s0_per_seq):
    blk = min(seq_len, GLA_ROW_BLOCK)
    n_rb = seq_len // blk
    hk = GLA_HEADS * GLA_DK
    hv = GLA_HEADS * GLA_DV
    rb0 = row0 // blk
    row = lambda b, r: rb0 + b * n_rb + r
    s0_map = (lambda b, r: (b, 0, 0, 0)) if s0_per_seq else (lambda b, r: (0, 0, 0, 0))
    kern = functools.partial(_gla_kernel, n_chunks=blk // CHUNK, n_valid=n_valid)
    return pl.pallas_call(
        kern,
        grid=(n_seq, n_rb),
        in_specs=[
            pl.BlockSpec((blk, hk), lambda b, r: (row(b, r), 0)),
            pl.BlockSpec((blk, hk), lambda b, r: (row(b, r), 1)),
            pl.BlockSpec((blk, hv), lambda b, r: (row(b, r), 2 * hk // hv)),
            pl.BlockSpec((blk, hv), lambda b, r: (row(b, r), 2 * hk // hv + 1)),
            pl.BlockSpec((blk, LANES), lambda b, r: (row(b, r), 0)),
            pl.BlockSpec((LANES, hk), lambda b, r: (0, 0)),
            pl.BlockSpec((1, hk), lambda b, r: (0, 0)),
            pl.BlockSpec((1, GLA_DV), lambda b, r: (0, 0)),
            pl.BlockSpec((None, GLA_HEADS, GLA_DK, GLA_DV), s0_map),
            pl.BlockSpec(memory_space=pl.ANY),
        ],
        out_specs=[
            pl.BlockSpec((blk, hv), lambda b, r: (row(b, r), 0)),
            pl.BlockSpec((None, GLA_HEADS, GLA_DK, GLA_DV), lambda b, r: (b, 0, 0, 0)),
        ],
        out_shape=[jax.ShapeDtypeStruct(og.shape, og.dtype),
                   jax.ShapeDtypeStruct((n_seq, GLA_HEADS, GLA_DK, GLA_DV), F32)],
        scratch_shapes=[pltpu.VMEM((GLA_HEADS, GLA_DV, GLA_DK), F32)],
        input_output_aliases={9: 0},
        compiler_params=_params(("parallel", "arbitrary"), 48),
        name="gla_scan",
    )(proj, proj, proj, proj, r128, wgk_pad, bgk, gnorm, s0, og)


WIN_KEYS = (WINDOW_CHUNKS + 1) * CHUNK
ATTN_KEYS = 2 * LANES
ATTN_ROW_BLOCK = 512
PAIRS = SWA_GROUP // 2


def _lane_halves(x2, head_in_pair):
    lane = lax.broadcasted_iota(jnp.int32, x2.shape, 1)
    low = lane < HEAD_DIM
    swapped = pltpu.roll(x2, HEAD_DIM, 1)
    if head_in_pair == 0:
        lo, hi = jnp.where(low, x2, 0.0), jnp.where(low, 0.0, swapped)
    else:
        lo, hi = jnp.where(low, swapped, 0.0), jnp.where(low, 0.0, x2)
    return lo.astype(BF16), hi.astype(BF16)


def _attn_kernel(q_ref, k_ref, v_ref, km_ref, vm_ref, sink_ref, o_in_ref, o_ref, kb_ref, vb_ref, kmb_ref, vmb_ref,
                 *, n_chunks, chunk_offset):
    del o_in_ref
    step = pl.program_id(1)

    @pl.when(step == 0)
    def _():
        for h in range(SWA_KV_HEADS):
            cols = slice((h // 2) * LANES, (h // 2 + 1) * LANES)
            for src, dst in ((k_ref, kb_ref), (v_ref, vb_ref), (km_ref, kmb_ref), (vm_ref, vmb_ref)):
                lo, hi = _lane_halves(src[:, cols], h % 2)
                dst[h, 0] = lo
                dst[h, 1] = hi

    j = lax.broadcasted_iota(jnp.int32, (1, ATTN_KEYS), 1)
    rel_chunk = (j >= N_META + CHUNK).astype(jnp.int32) + (j >= N_META + 2 * CHUNK).astype(jnp.int32)
    in_window = (j >= N_META) & (j < N_META + WIN_KEYS)
    low_lanes = lax.broadcasted_iota(jnp.int32, (1, LANES), 1) < HEAD_DIM
    zpad = jnp.zeros((ATTN_KEYS - N_META - WIN_KEYS, LANES), BF16)
    key_row = lax.broadcasted_iota(jnp.int32, (2 * ATTN_KEYS, LANES), 0)
    key_lane = lax.broadcasted_iota(jnp.int32, (2 * ATTN_KEYS, LANES), 1)
    ones_cols = ((key_row < ATTN_KEYS) == (key_lane < HEAD_DIM)).astype(BF16)
    nt = (((1,), (1,)), ((), ()))

    def body(ci, carry):
        c = step * n_chunks + ci + chunk_offset
        wc = jnp.maximum(c - WINDOW_CHUNKS, 0)
        win = pl.ds(pl.multiple_of(wc * CHUNK, CHUNK), WIN_KEYS)
        rows = pl.ds(pl.multiple_of(ci * CHUNK, CHUNK), CHUNK)
        valid = (j < N_META) | (in_window & (wc + rel_chunk <= c))
        for h in range(SWA_KV_HEADS):
            kb = jnp.concatenate([kmb_ref[h, 0], kb_ref[h, 0, win, :], zpad,
                                  kmb_ref[h, 1], kb_ref[h, 1, win, :], zpad], axis=0)
            vb = jnp.concatenate([vmb_ref[h, 0], vb_ref[h, 0, win, :], zpad,
                                  vmb_ref[h, 1], vb_ref[h, 1, win, :], zpad], axis=0)
            col = lambda p: slice((h * PAIRS + p) * LANES, (h * PAIRS + p + 1) * LANES)
            qs = jnp.concatenate([q_ref[rows, col(p)] for p in range(PAIRS)], axis=0)
            s = lax.dot_general(qs, kb, nt, preferred_element_type=F32)
            probs, sink_terms = [], []
            for half in range(2):
                sh = jnp.where(valid, s[:, half * ATTN_KEYS:(half + 1) * ATTN_KEYS], NEG_INF)
                sink = sink_ref[h, half][:, 0:1]
                m = jnp.maximum(jnp.max(sh, axis=-1, keepdims=True), sink)
                probs.append(jnp.exp(sh - m).astype(BF16))
                sink_terms.append(jnp.exp(sink - m))
            ov = jnp.dot(jnp.concatenate(probs, axis=1), jnp.concatenate([vb, ones_cols], axis=1),
                         preferred_element_type=F32)
            den = ov[:, LANES:] + jnp.where(low_lanes, sink_terms[0], sink_terms[1])
            o = ov[:, :LANES] / den
            for p in range(PAIRS):
                o_ref[rows, col(p)] = o[p * CHUNK:(p + 1) * CHUNK].astype(o_ref.dtype)
        return carry

    lax.fori_loop(0, n_chunks, body, 0, unroll=4 if n_chunks % 4 == 0 else 1)


def window_attention(q, k_arr, v_arr, km_arr, vm_arr, sink_tab, o, *, n_seq, seq_len, key_len, row0,
                     k_map, v_map, km_map, vm_map, chunk_offset):
    blk = min(seq_len, ATTN_ROW_BLOCK)
    steps = seq_len // blk
    q_spec = pl.BlockSpec((blk, q.shape[1]), lambda b, s: (row0 // blk + b * steps + s, 0))
    width = SWA_KV_HEADS * HEAD_DIM
    kern = functools.partial(_attn_kernel, n_chunks=blk // CHUNK, chunk_offset=chunk_offset)
    return pl.pallas_call(
        kern,
        grid=(n_seq, steps),
        in_specs=[q_spec,
                  pl.BlockSpec((key_len, width), k_map),
                  pl.BlockSpec((key_len, width), v_map),
                  pl.BlockSpec((N_META, width), km_map),
                  pl.BlockSpec((N_META, width), vm_map),
                  pl.BlockSpec((SWA_KV_HEADS, 2, PAIRS * CHUNK, LANES), lambda b, s: (0, 0, 0, 0)),
                  pl.BlockSpec(memory_space=pl.ANY)],
        out_specs=q_spec,
        out_shape=jax.ShapeDtypeStruct(o.shape, o.dtype),
        scratch_shapes=[pltpu.VMEM((SWA_KV_HEADS, 2, key_len, LANES), BF16),
                        pltpu.VMEM((SWA_KV_HEADS, 2, key_len, LANES), BF16),
                        pltpu.VMEM((SWA_KV_HEADS, 2, N_META, LANES), BF16),
                        pltpu.VMEM((SWA_KV_HEADS, 2, N_META, LANES), BF16)],
        input_output_aliases={6: 0},
        compiler_params=_params(("parallel", "arbitrary"), 48),
        name="window_attention",
    )(q, k_arr, v_arr, km_arr, vm_arr, sink_tab, o)


def _router_kernel(x_ref, g_ref, wr_ref, xp_ref, rt_ref, cnt_ref, carry_ref):
    i = pl.program_id(0)

    @pl.when(i == 0)
    def _():
        carry_ref[...] = jnp.zeros_like(carry_ref)

    x = x_ref[...]
    xn = x * lax.rsqrt(jnp.mean(x * x, axis=-1, keepdims=True) + EPS) * g_ref[...]
    bits = pltpu.bitcast(xn.astype(BF16).astype(F32), jnp.uint32)
    half = D_MODEL // 2
    xp_ref[...] = (bits[:, :half] >> 16) | (bits[:, half:] & jnp.uint32(0xFFFF0000))

    tm = x.shape[0]
    logits = jnp.dot(xn, wr_ref[...], precision=HIGHEST, preferred_element_type=F32)
    lane = lax.broadcasted_iota(jnp.int32, (tm, LANES), 1).astype(F32)
    neg = jnp.float32(-jnp.inf)
    lg = jnp.where(lane < N_EXPERTS, logits, neg)
    m1 = jnp.max(lg, axis=-1, keepdims=True)
    i1 = jnp.min(jnp.where(lg == m1, lane, float(LANES)), axis=-1, keepdims=True)
    lg2 = jnp.where(lane == i1, neg, lg)
    m2 = jnp.max(lg2, axis=-1, keepdims=True)
    i2 = jnp.min(jnp.where(lg2 == m2, lane, float(LANES)), axis=-1, keepdims=True)
    e21 = jnp.exp(m2 - m1)
    g1 = 1.0 / (1.0 + e21)
    g2 = e21 / (1.0 + e21)
    oh1 = (lane == i1).astype(F32)
    oh2 = (lane == i2).astype(F32)
    oh = oh1 + oh2
    ri = lax.broadcasted_iota(jnp.int32, (tm, tm), 0)
    ci = lax.broadcasted_iota(jnp.int32, (tm, tm), 1)
    before = (ci < ri).astype(BF16)
    prefix = jnp.dot(before, oh.astype(BF16), preferred_element_type=F32) + carry_ref[0:1, :]
    rank1 = jnp.sum(prefix * oh1, axis=-1, keepdims=True)
    rank2 = jnp.sum(prefix * oh2, axis=-1, keepdims=True)
    total = carry_ref[0:1, :] + jnp.sum(oh, axis=0, keepdims=True)
    carry_ref[...] = jnp.broadcast_to(total, carry_ref.shape)
    cnt_ref[...] = jnp.broadcast_to(total, cnt_ref.shape)
    rt = jnp.where(lane == 0, i1, 0.0)
    rt = jnp.where(lane == 1, i2, rt)
    rt = jnp.where(lane == 2, rank1, rt)
    rt = jnp.where(lane == 3, rank2, rt)
    rt = jnp.where(lane == 4, g1, rt)
    rt = jnp.where(lane == 5, g2, rt)
    rt_ref[...] = rt


def route(h, gain, w_router_pad):
    rows = h.shape[0]
    return pl.pallas_call(
        _router_kernel,
        grid=(rows // ROW_TILE,),
        in_specs=[pl.BlockSpec((ROW_TILE, D_MODEL), lambda i: (i, 0)),
                  pl.BlockSpec((1, D_MODEL), lambda i: (0, 0)),
                  pl.BlockSpec((D_MODEL, LANES), lambda i: (0, 0))],
        out_specs=[pl.BlockSpec((ROW_TILE, D_MODEL // 2), lambda i: (i, 0)),
                   pl.BlockSpec((ROW_TILE, LANES), lambda i: (i, 0)),
                   pl.BlockSpec((8, LANES), lambda i: (0, 0))],
        out_shape=[jax.ShapeDtypeStruct((rows, D_MODEL // 2), jnp.uint32),
                   jax.ShapeDtypeStruct((rows, LANES), F32),
                   jax.ShapeDtypeStruct((8, LANES), F32)],
        scratch_shapes=[pltpu.VMEM((8, LANES), F32)],
        compiler_params=_params(("arbitrary",), 40),
        name="route",
    )(h, gain, w_router_pad)


def _scatter_kernel(pos_ref, xp_ref, xs_in_ref, xs_ref, sem):
    del xs_in_ref
    tm = xp_ref.shape[0]

    def row_copy(r, slot):
        return pltpu.make_async_copy(xp_ref.at[pl.ds(r, 1), :], xs_ref.at[pl.ds(slot, 1), :], sem)

    def issue(r, carry):
        row_copy(r, pos_ref[2 * r]).start()
        row_copy(r, pos_ref[2 * r + 1]).start()
        return carry

    lax.fori_loop(0, tm, issue, 0, unroll=8)
    for _ in range(2):
        pltpu.make_async_copy(xp_ref, xs_ref.at[pl.ds(0, tm), :], sem).wait()


def scatter_rows(pos_flat, xp, xs_init):
    rows = xp.shape[0]
    return pl.pallas_call(
        _scatter_kernel,
        grid=(rows // ROW_TILE,),
        in_specs=[pl.BlockSpec((2 * ROW_TILE,), lambda i: (i,), memory_space=pltpu.SMEM),
                  pl.BlockSpec((ROW_TILE, xp.shape[1]), lambda i: (i, 0)),
                  pl.BlockSpec(memory_space=pl.ANY)],
        out_specs=pl.BlockSpec(memory_space=pl.ANY),
        out_shape=jax.ShapeDtypeStruct(xs_init.shape, xs_init.dtype),
        scratch_shapes=[pltpu.SemaphoreType.DMA(())],
        input_output_aliases={2: 0},
        compiler_params=_params(("arbitrary",), 32),
        name="scatter_rows",
    )(pos_flat, xp, xs_init)


def _combine_kernel(pos_ref, rt_ref, h_ref, g_ref, ys_ref, yp_ref, ysm_ref, buf_ref, sem, *, n_prompt_tiles):
    i = pl.program_id(0)
    tm = h_ref.shape[0]

    def row_copy(r, k, slot):
        return pltpu.make_async_copy(ys_ref.at[pl.ds(slot, 1), :], buf_ref.at[k, pl.ds(r, 1), :], sem)

    def issue(r, carry):
        row_copy(r, 0, pos_ref[2 * r]).start()
        row_copy(r, 1, pos_ref[2 * r + 1]).start()
        return carry

    lax.fori_loop(0, tm, issue, 0, unroll=8)
    for k in range(2):
        pltpu.make_async_copy(ys_ref.at[pl.ds(0, tm), :], buf_ref.at[k], sem).wait()
    rt = rt_ref[...]
    y = rt[:, 4:5] * buf_ref[0] + rt[:, 5:6] * buf_ref[1]
    x = h_ref[...] + y
    out = x * lax.rsqrt(jnp.mean(x * x, axis=-1, keepdims=True) + EPS) * g_ref[...]

    @pl.when(i < n_prompt_tiles)
    def _():
        yp_ref[...] = out

    @pl.when(i >= n_prompt_tiles)
    def _():
        ysm_ref[...] = out


def combine_rows(pos_flat, rt, h, gain, ys, *, n_prompt_rows):
    rows = h.shape[0]
    npt = n_prompt_rows // ROW_TILE
    kern = functools.partial(_combine_kernel, n_prompt_tiles=npt)
    return pl.pallas_call(
        kern,
        grid=(rows // ROW_TILE,),
        in_specs=[pl.BlockSpec((2 * ROW_TILE,), lambda i: (i,), memory_space=pltpu.SMEM),
                  pl.BlockSpec((ROW_TILE, LANES), lambda i: (i, 0)),
                  pl.BlockSpec((ROW_TILE, D_MODEL), lambda i: (i, 0)),
                  pl.BlockSpec((1, D_MODEL), lambda i: (0, 0)),
                  pl.BlockSpec(memory_space=pl.ANY)],
        out_specs=[pl.BlockSpec((ROW_TILE, D_MODEL), lambda i: (jnp.minimum(i, npt - 1), 0)),
                   pl.BlockSpec((ROW_TILE, D_MODEL), lambda i: (jnp.maximum(i - npt, 0), 0))],
        out_shape=[jax.ShapeDtypeStruct((n_prompt_rows, D_MODEL), F32),
                   jax.ShapeDtypeStruct((rows - n_prompt_rows, D_MODEL), F32)],
        scratch_shapes=[pltpu.VMEM((2, ROW_TILE, D_MODEL), F32), pltpu.SemaphoreType.DMA(())],
        compiler_params=_params(("arbitrary",), 48),
        name="combine_rows",
    )(pos_flat, rt, h, gain, ys)


def _rope_tables(pos):
    half = HEAD_DIM // 2
    inv_freq = ROPE_THETA ** (-jnp.arange(half, dtype=F32) / half)
    ang = jnp.asarray(pos, jnp.int32).astype(F32)[:, None] * inv_freq[None, :]
    cos = jnp.cos(ang)
    sin = jnp.sin(ang)
    reps = LANES // HEAD_DIM
    return (jnp.tile(jnp.concatenate([cos, cos], axis=1), (1, reps)),
            jnp.tile(jnp.concatenate([-sin, sin], axis=1), (1, reps)))


def kernel(x_prompt, x_sample, state_gla, cache_k_meta, cache_v_meta, cache_k_win, cache_v_win, meta_tokens,
           norm_mix, norm_ffn, norm_kv, norm_final, gla_w_in, gla_w_gk, gla_b_gk, gla_norm, gla_w_out, kv_w,
           attn_w_q, attn_sinks, attn_w_out, ffn_w_gate_up, ffn_w_down, moe_w_router, moe_w_gate_up,
           moe_w_down):
    bsz, seq, d = x_prompt.shape
    dbsz, t_new, _ = x_sample.shape
    past_len = 2048
    n_p = bsz * seq
    n_s = dbsz * t_new
    r1 = n_p + n_s
    assert r1 % ROW_TILE == 0 and seq % CHUNK == 0 and t_new == CHUNK
    r0 = -(-(r1 + CHUNK) // ROW_TILE) * ROW_TILE
    meta_row = r1
    hk = GLA_HEADS * GLA_DK
    hv = GLA_HEADS * GLA_DV
    ffn_dense = ffn_w_down.shape[1]
    ffn_expert = moe_w_down.shape[2]
    kv_width = SWA_KV_HEADS * HEAD_DIM

    x0 = jnp.concatenate([x_prompt.reshape(n_p, d), x_sample.reshape(n_s, d), meta_tokens.astype(F32),
                          jnp.zeros((r0 - r1 - N_META, d), F32)], axis=0)

    tm0 = r0 // 16
    tm1 = r1 // 16
    (xn0,) = rms_rows(x0, norm_mix[0:1])
    proj = matmul(xn0, gla_w_in, n_out=2 * hk + 2 * hv, out_dtype=BF16, tm=tm0, name="gla_in_proj")
    w_r = jnp.pad(gla_w_in[:, :, 2 * hk + 2 * hv:], ((0, 0), (0, 0), (0, LANES - GLA_RANK)))
    r128 = matmul(xn0, w_r, n_out=LANES, out_dtype=F32, tm=tm0, tn=LANES, name="gla_rank_proj")
    wgk_pad = jnp.pad(gla_w_gk[0], ((0, LANES - GLA_RANK), (0, 0)))
    bgk = gla_b_gk[0][None, :]
    gnorm = gla_norm[0][None, :]
    og = jnp.zeros((r0, hv), BF16)
    gla = functools.partial(gla_scan, proj, r128, wgk_pad, bgk, gnorm)
    og, s_meta = gla(jnp.zeros((1, GLA_HEADS, GLA_DK, GLA_DV), F32), og, n_seq=1, seq_len=CHUNK,
                     row0=meta_row, n_valid=N_META, s0_per_seq=False)
    og, s_prompt = gla(s_meta, og, n_seq=bsz, seq_len=seq, row0=0, n_valid=CHUNK, s0_per_seq=False)
    og, s_sample = gla(state_gla[0].astype(F32), og, n_seq=dbsz, seq_len=t_new, row0=n_p,
                       n_valid=CHUNK, s0_per_seq=True)
    h1 = matmul(og, gla_w_out, n_out=d, out_dtype=F32, mode="residual", residual=x0, tm=tm0,
                name="gla_out_proj")

    (hn1,) = rms_rows(h1, norm_ffn[0:1])
    act = matmul(hn1, ffn_w_gate_up, n_out=ffn_dense, out_dtype=BF16, mode="swiglu", up_col0=ffn_dense,
                 tm=tm0, name="ffn_gate_up")
    h2 = matmul(act, ffn_w_down, n_out=d, out_dtype=F32, mode="residual", residual=h1, tm=tm0 // 2,
                vmem_mib=56, name="ffn_down")

    pos = np.concatenate([np.tile(N_META + np.arange(seq), bsz),
                          np.tile(N_META + past_len + np.arange(t_new), dbsz),
                          np.arange(N_META), np.zeros(r0 - r1 - N_META, np.int64)])
    rope_tabs = _rope_tables(pos)
    xkv, xq = rms_rows(h2, jnp.stack([norm_kv, norm_mix[1]]))
    kvf = matmul(xkv, kv_w[None], n_out=2 * kv_width, out_dtype=F32, mode="rope", rope_tabs=rope_tabs,
                 rope_cols=kv_width, tm=tm0, name="shared_kv")
    q = matmul(xq, attn_w_q, n_out=d, out_dtype=BF16, mode="rope", rope_tabs=rope_tabs, rope_cols=d,
               scale=HEAD_DIM ** -0.5, rows=r1, tm=tm1, name="attn_q")

    sink_tab = jnp.broadcast_to(
        jnp.repeat(attn_sinks[0].astype(F32).reshape(SWA_KV_HEADS, PAIRS, 2).transpose(0, 2, 1), CHUNK, axis=2)
        [..., None], (SWA_KV_HEADS, 2, PAIRS * CHUNK, LANES))
    meta_blk = meta_row // N_META
    o_att = window_attention(
        q, kvf, kvf, kvf, kvf, sink_tab, jnp.zeros((r1, d), BF16), n_seq=bsz, seq_len=seq, key_len=seq, row0=0,
        k_map=lambda b, s: (b, 0), v_map=lambda b, s: (b, 1),
        km_map=lambda b, s: (meta_blk, 0), vm_map=lambda b, s: (meta_blk, 1), chunk_offset=0)
    k_new = kvf[n_p:r1, :kv_width]
    v_new = kvf[n_p:r1, kv_width:]
    win = cache_k_win.shape[1]
    ks = jnp.concatenate([cache_k_win.reshape(dbsz, win, kv_width).astype(F32),
                          k_new.reshape(dbsz, t_new, kv_width)], axis=1).reshape(dbsz * (win + t_new), kv_width)
    vs = jnp.concatenate([cache_v_win.reshape(dbsz, win, kv_width).astype(F32),
                          v_new.reshape(dbsz, t_new, kv_width)], axis=1).reshape(dbsz * (win + t_new), kv_width)
    assert win + t_new == (WINDOW_CHUNKS + 1) * CHUNK
    o_att = window_attention(
        q, ks, vs, cache_k_meta.reshape(dbsz * N_META, kv_width).astype(F32),
        cache_v_meta.reshape(dbsz * N_META, kv_width).astype(F32), sink_tab, o_att,
        n_seq=dbsz, seq_len=t_new, key_len=win + t_new, row0=n_p,
        k_map=lambda b, s: (b, 0), v_map=lambda b, s: (b, 0),
        km_map=lambda b, s: (b, 0), vm_map=lambda b, s: (b, 0), chunk_offset=WINDOW_CHUNKS)
    h3 = matmul(o_att, attn_w_out, n_out=d, out_dtype=F32, mode="residual", residual=h2,
                rows=r1, tm=tm1, name="attn_out_proj")

    w_router_pad = jnp.pad(moe_w_router[0], ((0, 0), (0, LANES - N_EXPERTS)))
    xp, rt, cnt = route(h3, norm_ffn[1:2], w_router_pad)
    unit, per_tile = MOE_UNIT, MOE_TILE // MOE_UNIT
    n_tiles_max = 2 * r1 // MOE_TILE + N_EXPERTS
    counts = cnt[0, :N_EXPERTS].astype(jnp.int32)
    units_per = (counts + unit - 1) // unit
    tiles_per = (units_per + per_tile - 1) // per_tile
    tile_end = jnp.cumsum(tiles_per)
    tile_start = tile_end - tiles_per
    experts = rt[:, 0:2].astype(jnp.int32)
    ranks = rt[:, 2:4].astype(jnp.int32)
    pos_flat = (tile_start[experts] * MOE_TILE + ranks).reshape(-1)
    n_used = tile_end[-1:].astype(jnp.int32)
    tile_id = jnp.arange(n_tiles_max, dtype=jnp.int32)
    tile_expert = jnp.minimum(jnp.sum(tile_id[:, None] >= tile_end[None, :], axis=1), N_EXPERTS - 1).astype(jnp.int32)
    tile_units = jnp.clip(units_per[tile_expert] - per_tile * (tile_id - tile_start[tile_expert]), 0, per_tile)
    tile_units = jnp.where(tile_id < n_used[0], tile_units, 0).astype(jnp.int32)
    xs = scatter_rows(pos_flat, xp, jnp.zeros((n_tiles_max * MOE_TILE, d // 2), jnp.uint32))
    act_e = matmul(xs, moe_w_gate_up.reshape(N_EXPERTS, d, 2 * ffn_expert), n_out=ffn_expert, out_dtype=BF16,
                   mode="swiglu", up_col0=ffn_expert, schedule=(tile_expert, tile_units, n_used),
                   sub_tiles=per_tile, tm=MOE_TILE, a_packed=True, vmem_mib=56, name="moe_gate_up")
    split = MOE_TILE // MOE_DOWN_TILE
    down_id = jnp.arange(n_tiles_max * split, dtype=jnp.int32)
    down_valid = ((down_id % split) * MOE_DOWN_TILE // unit < jnp.repeat(tile_units, split)).astype(jnp.int32)
    ys = matmul(act_e, moe_w_down.reshape(N_EXPERTS, ffn_expert, d), n_out=d, out_dtype=F32,
                schedule=(jnp.repeat(tile_expert, split), down_valid, split * n_used), tm=MOE_DOWN_TILE,
                vmem_mib=56, name="moe_down")
    y_prompt, y_sample = combine_rows(pos_flat, rt, h3, norm_final[None, :], ys, n_prompt_rows=n_p)

    kv_meta = kvf[meta_row:meta_row + N_META]
    k_meta_p = jnp.broadcast_to(kv_meta[None, :, :kv_width], (bsz, N_META, kv_width))
    v_meta_p = jnp.broadcast_to(kv_meta[None, :, kv_width:], (bsz, N_META, kv_width))
    win_p = min(WINDOW_CHUNKS * CHUNK, seq)
    kv_win = kvf[:n_p].reshape(bsz, seq, 2 * kv_width)[:, seq - win_p:]
    shape4 = lambda a: a.reshape(a.shape[0], a.shape[1], SWA_KV_HEADS, HEAD_DIM)
    return (y_prompt.reshape(bsz, seq, d), y_sample.reshape(dbsz, t_new, d),
            s_prompt[None].astype(state_gla.dtype), s_sample[None].astype(state_gla.dtype),
            shape4(k_meta_p), shape4(v_meta_p),
            shape4(kv_win[:, :, :kv_width]), shape4(kv_win[:, :, kv_width:]),
            shape4(k_new.reshape(dbsz, t_new, kv_width)), shape4(v_new.reshape(dbsz, t_new, kv_width)))
```

```python
import functools

import numpy as np
import jax
import jax.numpy as jnp
from jax import lax
from jax.experimental import pallas as pl
from jax.experimental.pallas import tpu as pltpu

F32 = jnp.float32
BF16 = jnp.bfloat16
HIGHEST = lax.Precision.HIGHEST

D_MODEL = 2048
CHUNK = 64
N_META = 16
GLA_HEADS = 4
GLA_DK = 256
GLA_DV = 512
GLA_RANK = 16
GLA_GATE_NORM = 16.0
HEAD_DIM = 64
SWA_HEADS = 32
SWA_KV_HEADS = 4
SWA_GROUP = 8
WINDOW_CHUNKS = 2
ROPE_THETA = 10000.0
N_EXPERTS = 8
EPS = 1e-5
NEG_INF = -1e30

LANES = 128
ROW_TILE = 512
COL_TILE = 512
MOE_TILE = 1024
MOE_UNIT = 512
MOE_DOWN_TILE = 512
MIB = 2 ** 20


def _params(semantics, vmem_mib):
    return pltpu.CompilerParams(dimension_semantics=semantics, vmem_limit_bytes=vmem_mib * MIB)


def _norm_kernel(x_ref, g_ref, *o_refs):
    x = x_ref[...]
    y = x * lax.rsqrt(jnp.mean(x * x, axis=-1, keepdims=True) + EPS)
    for i, o_ref in enumerate(o_refs):
        o_ref[...] = (y * g_ref[i:i + 1, :]).astype(o_ref.dtype)


def rms_rows(x, gains):
    rows, d = x.shape
    n = gains.shape[0]
    return pl.pallas_call(
        _norm_kernel,
        grid=(rows // ROW_TILE,),
        in_specs=[pl.BlockSpec((ROW_TILE, d), lambda i: (i, 0)),
                  pl.BlockSpec((n, d), lambda i: (0, 0))],
        out_specs=[pl.BlockSpec((ROW_TILE, d), lambda i: (i, 0)) for _ in range(n)],
        out_shape=[jax.ShapeDtypeStruct((rows, d), BF16) for _ in range(n)],
        compiler_params=_params(("parallel",), 40),
        name="rms_rows",
    )(x, gains)


def _swap_halves(x):
    lane = lax.broadcasted_iota(jnp.int32, x.shape, 1)
    first_half = (lane % HEAD_DIM) < (HEAD_DIM // 2)
    return jnp.where(first_half, pltpu.roll(x, LANES - HEAD_DIM // 2, 1), pltpu.roll(x, HEAD_DIM // 2, 1))


def _mm_kernel(te_ref, nv_ref, nu_ref, *refs, mode, a_packed, rope_cols, scale, sub_tiles):
    if mode == "swiglu":
        a_ref, w_ref, w2_ref, o_ref, wb_ref, wb2_ref = refs
    elif mode == "residual":
        a_ref, w_ref, res_ref, o_ref, wb_ref = refs
    elif mode == "rope":
        a_ref, w_ref, cos_ref, sin_ref, o_ref, wb_ref = refs
    else:
        a_ref, w_ref, o_ref, wb_ref = refs
    t = pl.program_id(1)
    tm = o_ref.shape[0]
    sub = tm // sub_tiles

    def compute(n_rows):
        rows = slice(0, n_rows)
        a = a_ref[rows, :]
        if a_packed:
            lo = pltpu.bitcast(a << 16, F32)
            hi = pltpu.bitcast(a & jnp.uint32(0xFFFF0000), F32)
            a = jnp.concatenate([lo, hi], axis=1)
        a = a.astype(BF16)
        acc = jnp.dot(a, wb_ref[...], preferred_element_type=F32)
        if mode == "swiglu":
            up = jnp.dot(a, wb2_ref[...], preferred_element_type=F32)
            o_ref[rows, :] = (acc * jax.nn.sigmoid(acc) * up).astype(o_ref.dtype)
        elif mode == "residual":
            o_ref[rows, :] = (res_ref[rows, :] + acc).astype(o_ref.dtype)
        elif mode == "rope":
            cos = cos_ref[rows, :]
            sin = sin_ref[rows, :]
            for c in range(acc.shape[1] // LANES):
                x = acc[:, c * LANES:(c + 1) * LANES]
                if c * LANES < rope_cols:
                    x = x * cos + _swap_halves(x) * sin
                o_ref[rows, c * LANES:(c + 1) * LANES] = (x * scale).astype(o_ref.dtype)
        else:
            o_ref[rows, :] = acc.astype(o_ref.dtype)
        if n_rows < tm:
            o_ref[n_rows:, :] = jnp.zeros((tm - n_rows, o_ref.shape[1]), o_ref.dtype)

    @pl.when(t < nu_ref[0])
    def _():
        @pl.when((t == 0) | (te_ref[t] != te_ref[jnp.maximum(t - 1, 0)]))
        def _():
            wb_ref[...] = w_ref[...].astype(BF16)
            if mode == "swiglu":
                wb2_ref[...] = w2_ref[...].astype(BF16)

    n_valid = jnp.where(t < nu_ref[0], nv_ref[t], 0)

    @pl.when(n_valid == 0)
    def _():
        o_ref[...] = jnp.zeros_like(o_ref)

    for s in range(1, sub_tiles + 1):
        pl.when(n_valid == s)(functools.partial(compute, s * sub))


def matmul(a, w, *, n_out, out_dtype, mode="plain", schedule=None, sub_tiles=1, residual=None,
           rope_tabs=None, rope_cols=0, scale=1.0, col0=0, up_col0=0, tm=ROW_TILE, tn=COL_TILE,
           a_packed=False, rows=None, vmem_mib=48, name="matmul"):
    rows = a.shape[0] if rows is None else rows
    k = w.shape[1]
    n_tiles = rows // tm
    n_col = n_out // tn
    w_mode = pl.Buffered(1) if schedule is None else None
    if schedule is None:
        schedule = (jnp.zeros((n_tiles,), jnp.int32), jnp.full((n_tiles,), sub_tiles, jnp.int32),
                    jnp.full((1,), n_tiles, jnp.int32))
    cb0 = col0 // tn
    ub0 = up_col0 // tn

    def row_of(t, nu):
        return jnp.maximum(jnp.minimum(t, nu[0] - 1), 0)

    a_spec = pl.BlockSpec((tm, a.shape[1]), lambda j, t, te, nv, nu: (row_of(t, nu), 0))
    w_spec = pl.BlockSpec((None, k, tn), lambda j, t, te, nv, nu: (te[row_of(t, nu)], 0, cb0 + j),
                          pipeline_mode=w_mode)
    o_spec = pl.BlockSpec((tm, tn), lambda j, t, te, nv, nu: (t, j))
    in_specs = [a_spec, w_spec]
    args = [a, w]
    scratch = [pltpu.VMEM((k, tn), BF16)]
    if mode == "swiglu":
        in_specs.append(pl.BlockSpec((None, k, tn), lambda j, t, te, nv, nu: (te[row_of(t, nu)], 0, ub0 + j),
                                     pipeline_mode=w_mode))
        args.append(w)
        scratch.append(pltpu.VMEM((k, tn), BF16))
    elif mode == "residual":
        in_specs.append(o_spec)
        args.append(residual)
    elif mode == "rope":
        tab_spec = pl.BlockSpec((tm, LANES), lambda j, t, te, nv, nu: (row_of(t, nu), 0))
        in_specs += [tab_spec, tab_spec]
        args += list(rope_tabs)
    kern = functools.partial(_mm_kernel, mode=mode, a_packed=a_packed, rope_cols=rope_cols, scale=scale,
                             sub_tiles=sub_tiles)
    return pl.pallas_call(
        kern,
        grid_spec=pltpu.PrefetchScalarGridSpec(
            num_scalar_prefetch=3, grid=(n_col, n_tiles),
            in_specs=in_specs, out_specs=o_spec, scratch_shapes=scratch),
        out_shape=jax.ShapeDtypeStruct((rows, n_out), out_dtype),
        compiler_params=_params(("arbitrary", "arbitrary"), vmem_mib),
        name=name,
    )(*schedule, *args)


GLA_ROW_BLOCK = 512


def _gate_kernel(r_ref, wgk_ref, bgk_ref, g_ref):
    z = jnp.dot(r_ref[...], wgk_ref[...], precision=HIGHEST, preferred_element_type=F32) + bgk_ref[...]
    g_ref[...] = (jnp.minimum(z, 0.0) - jnp.log(1.0 + jnp.exp(-jnp.abs(z)))) / GLA_GATE_NORM


def gate_rows(r128, wgk_pad, bgk):
    rows = r128.shape[0]
    width = wgk_pad.shape[1]
    return pl.pallas_call(
        _gate_kernel,
        grid=(rows // ROW_TILE,),
        in_specs=[pl.BlockSpec((ROW_TILE, LANES), lambda i: (i, 0)),
                  pl.BlockSpec((LANES, width), lambda i: (0, 0)),
                  pl.BlockSpec((1, width), lambda i: (0, 0))],
        out_specs=pl.BlockSpec((ROW_TILE, width), lambda i: (i, 0)),
        out_shape=jax.ShapeDtypeStruct((rows, width), F32),
        compiler_params=_params(("parallel",), 32),
        name="gate_rows",
    )(r128, wgk_pad, bgk)


def _gla_kernel(q_ref, k_ref, v_ref, go_ref, g_ref, gn_ref, s0_ref, og_in_ref,
                o_ref, sfin_ref, st_ref, *, n_chunks, n_valid):
    del og_in_ref
    rb = pl.program_id(1)

    @pl.when(rb == 0)
    def _():
        for h in range(GLA_HEADS):
            st_ref[h] = s0_ref[h].T

    ri = lax.broadcasted_iota(jnp.int32, (CHUNK, CHUNK), 0)
    ci = lax.broadcasted_iota(jnp.int32, (CHUNK, CHUNK), 1)
    causal = ci <= ri
    tril = jnp.where(causal, 1.0, 0.0).astype(BF16)
    row_valid = lax.broadcasted_iota(jnp.int32, (CHUNK, 1), 0) < n_valid
    gn = gn_ref[...]

    def body(c, carry):
        rows = pl.ds(pl.multiple_of(c * CHUNK, CHUNK), CHUNK)
        g = jnp.where(row_valid, g_ref[rows, :], 0.0)
        g_hi = g.astype(BF16)
        rest = g - g_hi.astype(F32)
        g_mid = rest.astype(BF16)
        g_lo = (rest - g_mid.astype(F32)).astype(BF16)
        b = (jnp.dot(tril, g_hi, preferred_element_type=F32) + jnp.dot(tril, g_mid, preferred_element_type=F32)
             + jnp.dot(tril, g_lo, preferred_element_type=F32))
        b_last = b[CHUNK - 1:CHUNK, :]
        q = q_ref[rows, :].astype(F32) * (GLA_DK ** -0.5)
        k = jnp.where(row_valid, k_ref[rows, :].astype(F32), 0.0)
        q_dec_all = (q * jnp.exp(b)).astype(BF16)
        k_dec_all = (k * jnp.exp(-b)).astype(BF16)
        k_last_all = (k * jnp.exp(b_last - b)).astype(BF16)
        decay = jnp.exp(b_last)
        for h in range(GLA_HEADS):
            ks = slice(h * GLA_DK, (h + 1) * GLA_DK)
            vs = slice(h * GLA_DV, (h + 1) * GLA_DV)
            q_dec = q_dec_all[:, ks]
            v = v_ref[rows, vs]
            att = lax.dot_general(q_dec, k_dec_all[:, ks], (((1,), (1,)), ((), ())),
                                  preferred_element_type=F32)
            att = jnp.where(causal, att, 0.0).astype(BF16)
            st = st_ref[h]
            o = jnp.dot(att, v, preferred_element_type=F32)
            o = o + lax.dot_general(q_dec, st.astype(BF16), (((1,), (1,)), ((), ())),
                                    preferred_element_type=F32)
            st_ref[h] = st * decay[:, ks] + lax.dot_general(
                v, k_last_all[:, ks], (((0,), (0,)), ((), ())), preferred_element_type=F32)
            on = o * lax.rsqrt(jnp.mean(o * o, axis=-1, keepdims=True) + EPS) * gn
            go = go_ref[rows, vs].astype(F32)
            o_ref[rows, vs] = (on * (go * jax.nn.sigmoid(go))).astype(o_ref.dtype)
        return carry

    lax.fori_loop(0, n_chunks, body, 0, unroll=4 if n_chunks % 4 == 0 else 1)

    @pl.when(rb == pl.num_programs(1) - 1)
    def _():
        for h in range(GLA_HEADS):
            sfin_ref[h] = st_ref[h].T


def gla_scan(proj, gates, gnorm, s0, og, *, n_seq, seq_len, row0, n_valid, s0_per_seq):
    blk = min(seq_len, GLA_ROW_BLOCK)
    n_rb = seq_len // blk
    hk = GLA_HEADS * GLA_DK
    hv = GLA_HEADS * GLA_DV
    rb0 = row0 // blk
    row = lambda b, r: rb0 + b * n_rb + r
    s0_map = (lambda b, r: (b, 0, 0, 0)) if s0_per_seq else (lambda b, r: (0, 0, 0, 0))
    kern = functools.partial(_gla_kernel, n_chunks=blk // CHUNK, n_valid=n_valid)
    return pl.pallas_call(
        kern,
        grid=(n_seq, n_rb),
        in_specs=[
            pl.BlockSpec((blk, hk), lambda b, r: (row(b, r), 0)),
            pl.BlockSpec((blk, hk), lambda b, r: (row(b, r), 1)),
            pl.BlockSpec((blk, hv), lambda b, r: (row(b, r), 2 * hk // hv)),
            pl.BlockSpec((blk, hv), lambda b, r: (row(b, r), 2 * hk // hv + 1)),
            pl.BlockSpec((blk, hk), lambda b, r: (row(b, r), 0)),
            pl.BlockSpec((1, GLA_DV), lambda b, r: (0, 0)),
            pl.BlockSpec((None, GLA_HEADS, GLA_DK, GLA_DV), s0_map),
            pl.BlockSpec(memory_space=pl.ANY),
        ],
        out_specs=[
            pl.BlockSpec((blk, hv), lambda b, r: (row(b, r), 0)),
            pl.BlockSpec((None, GLA_HEADS, GLA_DK, GLA_DV), lambda b, r: (b, 0, 0, 0)),
        ],
        out_shape=[jax.ShapeDtypeStruct(og.shape, og.dtype),
                   jax.ShapeDtypeStruct((n_seq, GLA_HEADS, GLA_DK, GLA_DV), F32)],
        scratch_shapes=[pltpu.VMEM((GLA_HEADS, GLA_DV, GLA_DK), F32)],
        input_output_aliases={7: 0},
        compiler_params=_params(("parallel", "arbitrary"), 48),
        name="gla_scan",
    )(proj, proj, proj, proj, gates, gnorm, s0, og)


WIN_KEYS = (WINDOW_CHUNKS + 1) * CHUNK
ATTN_KEYS = 2 * LANES
ATTN_ROW_BLOCK = 512
PAIRS = SWA_GROUP // 2


def _lane_halves(x2, head_in_pair):
    lane = lax.broadcasted_iota(jnp.int32, x2.shape, 1)
    low = lane < HEAD_DIM
    swapped = pltpu.roll(x2, HEAD_DIM, 1)
    if head_in_pair == 0:
        lo, hi = jnp.where(low, x2, 0.0), jnp.where(low, 0.0, swapped)
    else:
        lo, hi = jnp.where(low, swapped, 0.0), jnp.where(low, 0.0, x2)
    return lo.astype(BF16), hi.astype(BF16)


def _attn_kernel(q_ref, k_ref, v_ref, km_ref, vm_ref, sink_ref, o_in_ref, o_ref, kb_ref, vb_ref, kmb_ref, vmb_ref,
                 *, n_chunks, chunk_offset):
    del o_in_ref
    step = pl.program_id(1)

    @pl.when(step == 0)
    def _():
        for h in range(SWA_KV_HEADS):
            cols = slice((h // 2) * LANES, (h // 2 + 1) * LANES)
            for src, dst in ((k_ref, kb_ref), (v_ref, vb_ref), (km_ref, kmb_ref), (vm_ref, vmb_ref)):
                lo, hi = _lane_halves(src[:, cols], h % 2)
                dst[h, 0] = lo
                dst[h, 1] = hi

    j = lax.broadcasted_iota(jnp.int32, (1, ATTN_KEYS), 1)
    rel_chunk = (j >= N_META + CHUNK).astype(jnp.int32) + (j >= N_META + 2 * CHUNK).astype(jnp.int32)
    in_window = (j >= N_META) & (j < N_META + WIN_KEYS)
    low_lanes = lax.broadcasted_iota(jnp.int32, (1, LANES), 1) < HEAD_DIM
    zpad = jnp.zeros((ATTN_KEYS - N_META - WIN_KEYS, LANES), BF16)
    key_row = lax.broadcasted_iota(jnp.int32, (2 * ATTN_KEYS, LANES), 0)
    key_lane = lax.broadcasted_iota(jnp.int32, (2 * ATTN_KEYS, LANES), 1)
    ones_cols = ((key_row < ATTN_KEYS) == (key_lane < HEAD_DIM)).astype(BF16)
    nt = (((1,), (1,)), ((), ()))

    def body(ci, carry):
        c = step * n_chunks + ci + chunk_offset
        wc = jnp.maximum(c - WINDOW_CHUNKS, 0)
        win = pl.ds(pl.multiple_of(wc * CHUNK, CHUNK), WIN_KEYS)
        rows = pl.ds(pl.multiple_of(ci * CHUNK, CHUNK), CHUNK)
        valid = (j < N_META) | (in_window & (wc + rel_chunk <= c))
        for h in range(SWA_KV_HEADS):
            kb = jnp.concatenate([kmb_ref[h, 0], kb_ref[h, 0, win, :], zpad,
                                  kmb_ref[h, 1], kb_ref[h, 1, win, :], zpad], axis=0)
            vb = jnp.concatenate([vmb_ref[h, 0], vb_ref[h, 0, win, :], zpad,
                                  vmb_ref[h, 1], vb_ref[h, 1, win, :], zpad], axis=0)
            col = lambda p: slice((h * PAIRS + p) * LANES, (h * PAIRS + p + 1) * LANES)
            qs = jnp.concatenate([q_ref[rows, col(p)] for p in range(PAIRS)], axis=0)
            s = lax.dot_general(qs, kb, nt, preferred_element_type=F32)
            probs, sink_terms = [], []
            for half in range(2):
                sh = jnp.where(valid, s[:, half * ATTN_KEYS:(half + 1) * ATTN_KEYS], NEG_INF)
                sink = sink_ref[h, half][:, 0:1]
                m = jnp.maximum(jnp.max(sh, axis=-1, keepdims=True), sink)
                probs.append(jnp.exp(sh - m).astype(BF16))
                sink_terms.append(jnp.exp(sink - m))
            ov = jnp.dot(jnp.concatenate(probs, axis=1), jnp.concatenate([vb, ones_cols], axis=1),
                         preferred_element_type=F32)
            den = ov[:, LANES:] + jnp.where(low_lanes, sink_terms[0], sink_terms[1])
            o = ov[:, :LANES] / den
            for p in range(PAIRS):
                o_ref[rows, col(p)] = o[p * CHUNK:(p + 1) * CHUNK].astype(o_ref.dtype)
        return carry

    lax.fori_loop(0, n_chunks, body, 0, unroll=4 if n_chunks % 4 == 0 else 1)


def window_attention(q, k_arr, v_arr, km_arr, vm_arr, sink_tab, o, *, n_seq, seq_len, key_len, row0,
                     k_map, v_map, km_map, vm_map, chunk_offset):
    blk = min(seq_len, ATTN_ROW_BLOCK)
    steps = seq_len // blk
    q_spec = pl.BlockSpec((blk, q.shape[1]), lambda b, s: (row0 // blk + b * steps + s, 0))
    width = SWA_KV_HEADS * HEAD_DIM
    kern = functools.partial(_attn_kernel, n_chunks=blk // CHUNK, chunk_offset=chunk_offset)
    return pl.pallas_call(
        kern,
        grid=(n_seq, steps),
        in_specs=[q_spec,
                  pl.BlockSpec((key_len, width), k_map),
                  pl.BlockSpec((key_len, width), v_map),
                  pl.BlockSpec((N_META, width), km_map),
                  pl.BlockSpec((N_META, width), vm_map),
                  pl.BlockSpec((SWA_KV_HEADS, 2, PAIRS * CHUNK, LANES), lambda b, s: (0, 0, 0, 0)),
                  pl.BlockSpec(memory_space=pl.ANY)],
        out_specs=q_spec,
        out_shape=jax.ShapeDtypeStruct(o.shape, o.dtype),
        scratch_shapes=[pltpu.VMEM((SWA_KV_HEADS, 2, key_len, LANES), BF16),
                        pltpu.VMEM((SWA_KV_HEADS, 2, key_len, LANES), BF16),
                        pltpu.VMEM((SWA_KV_HEADS, 2, N_META, LANES), BF16),
                        pltpu.VMEM((SWA_KV_HEADS, 2, N_META, LANES), BF16)],
        input_output_aliases={6: 0},
        compiler_params=_params(("parallel", "arbitrary"), 48),
        name="window_attention",
    )(q, k_arr, v_arr, km_arr, vm_arr, sink_tab, o)


def _route_tile(xn, wr_ref, xp_ref, rt_ref, cnt_ref, carry_ref):
    bits = pltpu.bitcast(xn.astype(BF16).astype(F32), jnp.uint32)
    half = D_MODEL // 2
    xp_ref[...] = (bits[:, :half] >> 16) | (bits[:, half:] & jnp.uint32(0xFFFF0000))

    tm = xn.shape[0]
    hi = xn.astype(BF16)
    lo = (xn - hi.astype(F32)).astype(BF16)
    wide = jnp.dot(hi, wr_ref[...], preferred_element_type=F32)
    logits = (wide[:, :LANES] + wide[:, LANES:]
              + jnp.dot(lo, wr_ref[:, :LANES], preferred_element_type=F32))
    lane = lax.broadcasted_iota(jnp.int32, (tm, LANES), 1).astype(F32)
    neg = jnp.float32(-jnp.inf)
    lg = jnp.where(lane < N_EXPERTS, logits, neg)
    m1 = jnp.max(lg, axis=-1, keepdims=True)
    i1 = jnp.min(jnp.where(lg == m1, lane, float(LANES)), axis=-1, keepdims=True)
    lg2 = jnp.where(lane == i1, neg, lg)
    m2 = jnp.max(lg2, axis=-1, keepdims=True)
    i2 = jnp.min(jnp.where(lg2 == m2, lane, float(LANES)), axis=-1, keepdims=True)
    e21 = jnp.exp(m2 - m1)
    g1 = 1.0 / (1.0 + e21)
    g2 = e21 / (1.0 + e21)
    oh1 = (lane == i1).astype(F32)
    oh2 = (lane == i2).astype(F32)
    oh = oh1 + oh2
    ri = lax.broadcasted_iota(jnp.int32, (tm, tm), 0)
    ci = lax.broadcasted_iota(jnp.int32, (tm, tm), 1)
    before = (ci < ri).astype(BF16)
    prefix = jnp.dot(before, oh.astype(BF16), preferred_element_type=F32) + carry_ref[0:1, :]
    rank1 = jnp.sum(prefix * oh1, axis=-1, keepdims=True)
    rank2 = jnp.sum(prefix * oh2, axis=-1, keepdims=True)
    total = carry_ref[0:1, :] + jnp.sum(oh, axis=0, keepdims=True)
    carry_ref[...] = jnp.broadcast_to(total, carry_ref.shape)
    cnt_ref[...] = jnp.broadcast_to(total, cnt_ref.shape)
    rt = jnp.where(lane == 0, i1, 0.0)
    rt = jnp.where(lane == 1, i2, rt)
    rt = jnp.where(lane == 2, rank1, rt)
    rt = jnp.where(lane == 3, rank2, rt)
    rt = jnp.where(lane == 4, g1, rt)
    rt = jnp.where(lane == 5, g2, rt)
    rt_ref[...] = rt


def _rowmm_kernel(a_ref, w_ref, res_ref, g_ref, *rest, router):
    if router:
        wr_ref, h_ref, xp_ref, rt_ref, cnt_ref, wb_ref, carry_ref, wcat_ref = rest
    else:
        h_ref, xn_ref, wb_ref = rest

    @pl.when(pl.program_id(0) == 0)
    def _():
        wb_ref[...] = w_ref[...].astype(BF16)
        if router:
            carry_ref[...] = jnp.zeros_like(carry_ref)
            wr = wr_ref[...]
            wr_hi = wr.astype(BF16)
            wcat_ref[:, :LANES] = wr_hi
            wcat_ref[:, LANES:] = (wr - wr_hi.astype(F32)).astype(BF16)

    h = res_ref[...] + jnp.dot(a_ref[...], wb_ref[...], preferred_element_type=F32)
    h_ref[...] = h
    xn = h * lax.rsqrt(jnp.mean(h * h, axis=-1, keepdims=True) + EPS) * g_ref[...]
    if router:
        _route_tile(xn, wcat_ref, xp_ref, rt_ref, cnt_ref, carry_ref)
    else:
        xn_ref[...] = xn.astype(xn_ref.dtype)


def row_matmul(a, w, residual, gain, *, rows, tm, w_router_pad=None, name="row_matmul"):
    k, d = w.shape[1], w.shape[2]
    router = w_router_pad is not None
    row_spec = lambda width: pl.BlockSpec((tm, width), lambda t: (t, 0))
    in_specs = [row_spec(k),
                pl.BlockSpec((None, k, d), lambda t: (0, 0, 0), pipeline_mode=pl.Buffered(1)),
                row_spec(d),
                pl.BlockSpec((1, d), lambda t: (0, 0))]
    args = [a, w, residual, gain]
    out_specs = [row_spec(d)]
    out_shape = [jax.ShapeDtypeStruct((rows, d), F32)]
    scratch = [pltpu.VMEM((k, d), BF16)]
    if router:
        in_specs.append(pl.BlockSpec((d, LANES), lambda t: (0, 0)))
        args.append(w_router_pad)
        out_specs += [row_spec(d // 2), row_spec(LANES), pl.BlockSpec((8, LANES), lambda t: (0, 0))]
        out_shape += [jax.ShapeDtypeStruct((rows, d // 2), jnp.uint32),
                      jax.ShapeDtypeStruct((rows, LANES), F32),
                      jax.ShapeDtypeStruct((8, LANES), F32)]
        scratch += [pltpu.VMEM((8, LANES), F32), pltpu.VMEM((d, 2 * LANES), BF16)]
    else:
        out_specs.append(row_spec(d))
        out_shape.append(jax.ShapeDtypeStruct((rows, d), BF16))
    return pl.pallas_call(
        functools.partial(_rowmm_kernel, router=router),
        grid=(rows // tm,),
        in_specs=in_specs, out_specs=out_specs, out_shape=out_shape, scratch_shapes=scratch,
        compiler_params=_params(("arbitrary",), 56),
        name=name,
    )(*args)


def _scatter_kernel(pos_ref, xp_ref, xs_in_ref, xs_ref, sem):
    del xs_in_ref
    tm = xp_ref.shape[0]

    def row_copy(r, slot):
        return pltpu.make_async_copy(xp_ref.at[pl.ds(r, 1), :], xs_ref.at[pl.ds(slot, 1), :], sem)

    def issue(r, carry):
        row_copy(r, pos_ref[2 * r]).start()
        row_copy(r, pos_ref[2 * r + 1]).start()
        return carry

    lax.fori_loop(0, tm, issue, 0, unroll=8)
    for _ in range(2):
        pltpu.make_async_copy(xp_ref, xs_ref.at[pl.ds(0, tm), :], sem).wait()


def scatter_rows(pos_flat, xp, xs_init):
    rows = xp.shape[0]
    return pl.pallas_call(
        _scatter_kernel,
        grid=(rows // ROW_TILE,),
        in_specs=[pl.BlockSpec((2 * ROW_TILE,), lambda i: (i,), memory_space=pltpu.SMEM),
                  pl.BlockSpec((ROW_TILE, xp.shape[1]), lambda i: (i, 0)),
                  pl.BlockSpec(memory_space=pl.ANY)],
        out_specs=pl.BlockSpec(memory_space=pl.ANY),
        out_shape=jax.ShapeDtypeStruct(xs_init.shape, xs_init.dtype),
        scratch_shapes=[pltpu.SemaphoreType.DMA(())],
        input_output_aliases={2: 0},
        compiler_params=_params(("arbitrary",), 32),
        name="scatter_rows",
    )(pos_flat, xp, xs_init)


def _combine_kernel(pos_ref, rt_ref, h_ref, g_ref, ys_ref, yp_ref, ysm_ref, buf_ref, sem, *, n_prompt_tiles):
    i = pl.program_id(0)
    tm = h_ref.shape[0]

    def row_copy(r, k, slot):
        return pltpu.make_async_copy(ys_ref.at[pl.ds(slot, 1), :], buf_ref.at[k, pl.ds(r, 1), :], sem)

    def issue(r, carry):
        row_copy(r, 0, pos_ref[2 * r]).start()
        row_copy(r, 1, pos_ref[2 * r + 1]).start()
        return carry

    lax.fori_loop(0, tm, issue, 0, unroll=8)
    for k in range(2):
        pltpu.make_async_copy(ys_ref.at[pl.ds(0, tm), :], buf_ref.at[k], sem).wait()
    rt = rt_ref[...]
    y = rt[:, 4:5] * buf_ref[0] + rt[:, 5:6] * buf_ref[1]
    x = h_ref[...] + y
    out = x * lax.rsqrt(jnp.mean(x * x, axis=-1, keepdims=True) + EPS) * g_ref[...]

    @pl.when(i < n_prompt_tiles)
    def _():
        yp_ref[...] = out

    @pl.when(i >= n_prompt_tiles)
    def _():
        ysm_ref[...] = out


def combine_rows(pos_flat, rt, h, gain, ys, *, n_prompt_rows):
    rows = h.shape[0]
    npt = n_prompt_rows // ROW_TILE
    kern = functools.partial(_combine_kernel, n_prompt_tiles=npt)
    return pl.pallas_call(
        kern,
        grid=(rows // ROW_TILE,),
        in_specs=[pl.BlockSpec((2 * ROW_TILE,), lambda i: (i,), memory_space=pltpu.SMEM),
                  pl.BlockSpec((ROW_TILE, LANES), lambda i: (i, 0)),
                  pl.BlockSpec((ROW_TILE, D_MODEL), lambda i: (i, 0)),
                  pl.BlockSpec((1, D_MODEL), lambda i: (0, 0)),
                  pl.BlockSpec(memory_space=pl.ANY)],
        out_specs=[pl.BlockSpec((ROW_TILE, D_MODEL), lambda i: (jnp.minimum(i, npt - 1), 0)),
                   pl.BlockSpec((ROW_TILE, D_MODEL), lambda i: (jnp.maximum(i - npt, 0), 0))],
        out_shape=[jax.ShapeDtypeStruct((n_prompt_rows, D_MODEL), F32),
                   jax.ShapeDtypeStruct((rows - n_prompt_rows, D_MODEL), F32)],
        scratch_shapes=[pltpu.VMEM((2, ROW_TILE, D_MODEL), F32), pltpu.SemaphoreType.DMA(())],
        compiler_params=_params(("arbitrary",), 48),
        name="combine_rows",
    )(pos_flat, rt, h, gain, ys)


def _rope_tables(pos):
    half = HEAD_DIM // 2
    inv_freq = ROPE_THETA ** (-jnp.arange(half, dtype=F32) / half)
    ang = jnp.asarray(pos, jnp.int32).astype(F32)[:, None] * inv_freq[None, :]
    cos = jnp.cos(ang)
    sin = jnp.sin(ang)
    reps = LANES // HEAD_DIM
    return (jnp.tile(jnp.concatenate([cos, cos], axis=1), (1, reps)),
            jnp.tile(jnp.concatenate([-sin, sin], axis=1), (1, reps)))


def kernel(x_prompt, x_sample, state_gla, cache_k_meta, cache_v_meta, cache_k_win, cache_v_win, meta_tokens,
           norm_mix, norm_ffn, norm_kv, norm_final, gla_w_in, gla_w_gk, gla_b_gk, gla_norm, gla_w_out, kv_w,
           attn_w_q, attn_sinks, attn_w_out, ffn_w_gate_up, ffn_w_down, moe_w_router, moe_w_gate_up,
           moe_w_down):
    bsz, seq, d = x_prompt.shape
    dbsz, t_new, _ = x_sample.shape
    past_len = 2048
    n_p = bsz * seq
    n_s = dbsz * t_new
    r1 = n_p + n_s
    assert r1 % ROW_TILE == 0 and seq % CHUNK == 0 and t_new == CHUNK
    r0 = -(-(r1 + CHUNK) // ROW_TILE) * ROW_TILE
    meta_row = r1
    hk = GLA_HEADS * GLA_DK
    hv = GLA_HEADS * GLA_DV
    ffn_dense = ffn_w_down.shape[1]
    ffn_expert = moe_w_down.shape[2]
    kv_width = SWA_KV_HEADS * HEAD_DIM

    x0 = jnp.concatenate([x_prompt.reshape(n_p, d), x_sample.reshape(n_s, d), meta_tokens.astype(F32),
                          jnp.zeros((r0 - r1 - N_META, d), F32)], axis=0)

    tm0 = r0 // 16
    tm1 = r1 // 16
    (xn0,) = rms_rows(x0, norm_mix[0:1])
    proj = matmul(xn0, gla_w_in, n_out=2 * hk + 2 * hv, out_dtype=BF16, tm=tm0, name="gla_in_proj")
    w_r = jnp.pad(gla_w_in[:, :, 2 * hk + 2 * hv:], ((0, 0), (0, 0), (0, LANES - GLA_RANK)))
    r128 = matmul(xn0, w_r, n_out=LANES, out_dtype=F32, tm=tm0, tn=LANES, name="gla_rank_proj")
    wgk_pad = jnp.pad(gla_w_gk[0], ((0, LANES - GLA_RANK), (0, 0)))
    bgk = gla_b_gk[0][None, :]
    gnorm = gla_norm[0][None, :]
    og = jnp.zeros((r0, hv), BF16)
    gla = functools.partial(gla_scan, proj, gate_rows(r128, wgk_pad, bgk), gnorm)
    og, s_meta = gla(jnp.zeros((1, GLA_HEADS, GLA_DK, GLA_DV), F32), og, n_seq=1, seq_len=CHUNK,
                     row0=meta_row, n_valid=N_META, s0_per_seq=False)
    og, s_prompt = gla(s_meta, og, n_seq=bsz, seq_len=seq, row0=0, n_valid=CHUNK, s0_per_seq=False)
    og, s_sample = gla(state_gla[0].astype(F32), og, n_seq=dbsz, seq_len=t_new, row0=n_p,
                       n_valid=CHUNK, s0_per_seq=True)
    h1, hn1 = row_matmul(og, gla_w_out, x0, norm_ffn[0:1], rows=r0, tm=r0 // 64, name="gla_out_proj")

    act = matmul(hn1, ffn_w_gate_up, n_out=ffn_dense, out_dtype=BF16, mode="swiglu", up_col0=ffn_dense,
                 tm=tm0, name="ffn_gate_up")
    h2 = matmul(act, ffn_w_down, n_out=d, out_dtype=F32, mode="residual", residual=h1, tm=tm0 // 2,
                vmem_mib=56, name="ffn_down")

    pos = np.concatenate([np.tile(N_META + np.arange(seq), bsz),
                          np.tile(N_META + past_len + np.arange(t_new), dbsz),
                          np.arange(N_META), np.zeros(r0 - r1 - N_META, np.int64)])
    rope_tabs = _rope_tables(pos)
    xkv, xq = rms_rows(h2, jnp.stack([norm_kv, norm_mix[1]]))
    kvf = matmul(xkv, kv_w[None], n_out=2 * kv_width, out_dtype=F32, mode="rope", rope_tabs=rope_tabs,
                 rope_cols=kv_width, tm=tm0, name="shared_kv")
    q = matmul(xq, attn_w_q, n_out=d, out_dtype=BF16, mode="rope", rope_tabs=rope_tabs, rope_cols=d,
               scale=HEAD_DIM ** -0.5, rows=r1, tm=tm1, name="attn_q")

    sink_tab = jnp.broadcast_to(
        jnp.repeat(attn_sinks[0].astype(F32).reshape(SWA_KV_HEADS, PAIRS, 2).transpose(0, 2, 1), CHUNK, axis=2)
        [..., None], (SWA_KV_HEADS, 2, PAIRS * CHUNK, LANES))
    meta_blk = meta_row // N_META
    o_att = window_attention(
        q, kvf, kvf, kvf, kvf, sink_tab, jnp.zeros((r1, d), BF16), n_seq=bsz, seq_len=seq, key_len=seq, row0=0,
        k_map=lambda b, s: (b, 0), v_map=lambda b, s: (b, 1),
        km_map=lambda b, s: (meta_blk, 0), vm_map=lambda b, s: (meta_blk, 1), chunk_offset=0)
    k_new = kvf[n_p:r1, :kv_width]
    v_new = kvf[n_p:r1, kv_width:]
    win = cache_k_win.shape[1]
    ks = jnp.concatenate([cache_k_win.reshape(dbsz, win, kv_width).astype(F32),
                          k_new.reshape(dbsz, t_new, kv_width)], axis=1).reshape(dbsz * (win + t_new), kv_width)
    vs = jnp.concatenate([cache_v_win.reshape(dbsz, win, kv_width).astype(F32),
                          v_new.reshape(dbsz, t_new, kv_width)], axis=1).reshape(dbsz * (win + t_new), kv_width)
    assert win + t_new == (WINDOW_CHUNKS + 1) * CHUNK
    o_att = window_attention(
        q, ks, vs, cache_k_meta.reshape(dbsz * N_META, kv_width).astype(F32),
        cache_v_meta.reshape(dbsz * N_META, kv_width).astype(F32), sink_tab, o_att,
        n_seq=dbsz, seq_len=t_new, key_len=win + t_new, row0=n_p,
        k_map=lambda b, s: (b, 0), v_map=lambda b, s: (b, 0),
        km_map=lambda b, s: (b, 0), vm_map=lambda b, s: (b, 0), chunk_offset=WINDOW_CHUNKS)
    w_router_pad = jnp.pad(moe_w_router[0], ((0, 0), (0, LANES - N_EXPERTS)))
    h3, xp, rt, cnt = row_matmul(o_att, attn_w_out, h2, norm_ffn[1:2], rows=r1, tm=r1 // 48,
                                 w_router_pad=w_router_pad, name="attn_out_route")
    unit, per_tile = MOE_UNIT, MOE_TILE // MOE_UNIT
    n_tiles_max = 2 * r1 // MOE_TILE + N_EXPERTS
    counts = cnt[0, :N_EXPERTS].astype(jnp.int32)
    units_per = (counts + unit - 1) // unit
    tiles_per = (units_per + per_tile - 1) // per_tile
    tile_end = jnp.cumsum(tiles_per)
    tile_start = tile_end - tiles_per
    experts = rt[:, 0:2].astype(jnp.int32)
    ranks = rt[:, 2:4].astype(jnp.int32)
    pos_flat = (tile_start[experts] * MOE_TILE + ranks).reshape(-1)
    n_used = tile_end[-1:].astype(jnp.int32)
    tile_id = jnp.arange(n_tiles_max, dtype=jnp.int32)
    tile_expert = jnp.minimum(jnp.sum(tile_id[:, None] >= tile_end[None, :], axis=1), N_EXPERTS - 1).astype(jnp.int32)
    tile_units = jnp.clip(units_per[tile_expert] - per_tile * (tile_id - tile_start[tile_expert]), 0, per_tile)
    tile_units = jnp.where(tile_id < n_used[0], tile_units, 0).astype(jnp.int32)
    xs = scatter_rows(pos_flat, xp, jnp.zeros((n_tiles_max * MOE_TILE, d // 2), jnp.uint32))
    act_e = matmul(xs, moe_w_gate_up.reshape(N_EXPERTS, d, 2 * ffn_expert), n_out=ffn_expert, out_dtype=BF16,
                   mode="swiglu", up_col0=ffn_expert, schedule=(tile_expert, tile_units, n_used),
                   sub_tiles=per_tile, tm=MOE_TILE, a_packed=True, vmem_mib=56, name="moe_gate_up")
    split = MOE_TILE // MOE_DOWN_TILE
    down_id = jnp.arange(n_tiles_max * split, dtype=jnp.int32)
    down_valid = ((down_id % split) * MOE_DOWN_TILE // unit < jnp.repeat(tile_units, split)).astype(jnp.int32)
    ys = matmul(act_e, moe_w_down.reshape(N_EXPERTS, ffn_expert, d), n_out=d, out_dtype=F32,
                schedule=(jnp.repeat(tile_expert, split), down_valid, split * n_used), tm=MOE_DOWN_TILE,
                vmem_mib=56, name="moe_down")
    y_prompt, y_sample = combine_rows(pos_flat, rt, h3, norm_final[None, :], ys, n_prompt_rows=n_p)

    kv_meta = kvf[meta_row:meta_row + N_META]
    k_meta_p = jnp.broadcast_to(kv_meta[None, :, :kv_width], (bsz, N_META, kv_width))
    v_meta_p = jnp.broadcast_to(kv_meta[None, :, kv_width:], (bsz, N_META, kv_width))
    win_p = min(WINDOW_CHUNKS * CHUNK, seq)
    kv_win = kvf[:n_p].reshape(bsz, seq, 2 * kv_width)[:, seq - win_p:]
    shape4 = lambda a: a.reshape(a.shape[0], a.shape[1], SWA_KV_HEADS, HEAD_DIM)
    return (y_prompt.reshape(bsz, seq, d), y_sample.reshape(dbsz, t_new, d),
            s_prompt[None].astype(state_gla.dtype), s_sample[None].astype(state_gla.dtype),
            shape4(k_meta_p), shape4(v_meta_p),
            shape4(kv_win[:, :, :kv_width]), shape4(kv_win[:, :, kv_width:]),
            shape4(k_new.reshape(dbsz, t_new, kv_width)), shape4(v_new.reshape(dbsz, t_new, kv_width)))
```

```python
import functools

import numpy as np
import jax
import jax.numpy as jnp
from jax import lax
from jax.experimental import pallas as pl
from jax.experimental.pallas import tpu as pltpu

F32 = jnp.float32
BF16 = jnp.bfloat16
HIGHEST = lax.Precision.HIGHEST

D_MODEL = 2048
CHUNK = 64
N_META = 16
GLA_HEADS = 4
GLA_DK = 256
GLA_DV = 512
GLA_RANK = 16
GLA_GATE_NORM = 16.0
HEAD_DIM = 64
SWA_HEADS = 32
SWA_KV_HEADS = 4
SWA_GROUP = 8
WINDOW_CHUNKS = 2
ROPE_THETA = 10000.0
N_EXPERTS = 8
EPS = 1e-5
NEG_INF = -1e30

LANES = 128
ROW_TILE = 512
COL_TILE = 512
MOE_TILE = 1024
MOE_UNIT = 512
MOE_DOWN_TILE = 512
MIB = 2 ** 20


def _params(semantics, vmem_mib):
    return pltpu.CompilerParams(dimension_semantics=semantics, vmem_limit_bytes=vmem_mib * MIB)


def _part_specs(parts, tm):
    specs, bounds, start = [], [], 0
    for p in parts:
        n = p.shape[0] // tm
        assert n * tm == p.shape[0]
        specs.append(pl.BlockSpec((tm, p.shape[1]), lambda t, s=start, n=n: (jnp.clip(t - s, 0, n - 1), 0)))
        start += n
        bounds.append(start)
    return specs, tuple(bounds)


def _read_part(refs, bounds, t):
    x = refs[0][...]
    for ref, lo in zip(refs[1:], bounds[:-1]):
        x = jnp.where(t >= lo, ref[...], x)
    return x


def _norm_kernel(*refs, bounds):
    n_src = len(bounds)
    g_ref = refs[n_src]
    x = _read_part(refs[:n_src], bounds, pl.program_id(0))
    y = x * lax.rsqrt(jnp.mean(x * x, axis=-1, keepdims=True) + EPS)
    for i, o_ref in enumerate(refs[n_src + 1:]):
        o_ref[...] = (y * g_ref[i:i + 1, :]).astype(o_ref.dtype)


def rms_rows(parts, gains):
    d = parts[0].shape[1]
    n = gains.shape[0]
    specs, bounds = _part_specs(parts, ROW_TILE)
    rows = bounds[-1] * ROW_TILE
    return pl.pallas_call(
        functools.partial(_norm_kernel, bounds=bounds),
        grid=(bounds[-1],),
        in_specs=specs + [pl.BlockSpec((n, d), lambda i: (0, 0))],
        out_specs=[pl.BlockSpec((ROW_TILE, d), lambda i: (i, 0)) for _ in range(n)],
        out_shape=[jax.ShapeDtypeStruct((rows, d), BF16) for _ in range(n)],
        compiler_params=_params(("parallel",), 40),
        name="rms_rows",
    )(*parts, gains)


def _swap_halves(x):
    lane = lax.broadcasted_iota(jnp.int32, x.shape, 1)
    first_half = (lane % HEAD_DIM) < (HEAD_DIM // 2)
    return jnp.where(first_half, pltpu.roll(x, LANES - HEAD_DIM // 2, 1), pltpu.roll(x, HEAD_DIM // 2, 1))


def _mm_kernel(te_ref, nv_ref, nu_ref, *refs, mode, a_packed, rope_cols, scale, sub_tiles):
    if mode == "swiglu":
        a_ref, w_ref, w2_ref, o_ref, wb_ref, wb2_ref = refs
    elif mode == "residual":
        a_ref, w_ref, res_ref, o_ref, wb_ref = refs
    elif mode == "rope":
        a_ref, w_ref, cos_ref, sin_ref, o_ref, wb_ref = refs
    else:
        a_ref, w_ref, o_ref, wb_ref = refs
    t = pl.program_id(1)
    tm = o_ref.shape[0]
    sub = tm // sub_tiles

    def compute(n_rows):
        rows = slice(0, n_rows)
        a = a_ref[rows, :]
        if a_packed:
            lo = pltpu.bitcast(a << 16, F32)
            hi = pltpu.bitcast(a & jnp.uint32(0xFFFF0000), F32)
            a = jnp.concatenate([lo, hi], axis=1)
        a = a.astype(BF16)
        acc = jnp.dot(a, wb_ref[...], preferred_element_type=F32)
        if mode == "swiglu":
            up = jnp.dot(a, wb2_ref[...], preferred_element_type=F32)
            o_ref[rows, :] = (acc * jax.nn.sigmoid(acc) * up).astype(o_ref.dtype)
        elif mode == "residual":
            o_ref[rows, :] = (res_ref[rows, :] + acc).astype(o_ref.dtype)
        elif mode == "rope":
            cos = cos_ref[rows, :]
            sin = sin_ref[rows, :]
            for c in range(acc.shape[1] // LANES):
                x = acc[:, c * LANES:(c + 1) * LANES]
                if c * LANES < rope_cols:
                    x = x * cos + _swap_halves(x) * sin
                o_ref[rows, c * LANES:(c + 1) * LANES] = (x * scale).astype(o_ref.dtype)
        else:
            o_ref[rows, :] = acc.astype(o_ref.dtype)
        if n_rows < tm:
            o_ref[n_rows:, :] = jnp.zeros((tm - n_rows, o_ref.shape[1]), o_ref.dtype)

    @pl.when(t < nu_ref[0])
    def _():
        @pl.when((t == 0) | (te_ref[t] != te_ref[jnp.maximum(t - 1, 0)]))
        def _():
            wb_ref[...] = w_ref[...].astype(BF16)
            if mode == "swiglu":
                wb2_ref[...] = w2_ref[...].astype(BF16)

    n_valid = jnp.where(t < nu_ref[0], nv_ref[t], 0)

    @pl.when(n_valid == 0)
    def _():
        o_ref[...] = jnp.zeros_like(o_ref)

    for s in range(1, sub_tiles + 1):
        pl.when(n_valid == s)(functools.partial(compute, s * sub))


def matmul(a, w, *, n_out, out_dtype, mode="plain", schedule=None, sub_tiles=1, residual=None,
           rope_tabs=None, rope_cols=0, scale=1.0, col0=0, up_col0=0, tm=ROW_TILE, tn=COL_TILE,
           a_packed=False, rows=None, vmem_mib=48, name="matmul"):
    rows = a.shape[0] if rows is None else rows
    k = w.shape[1]
    n_tiles = rows // tm
    n_col = n_out // tn
    w_mode = pl.Buffered(1) if schedule is None else None
    if schedule is None:
        schedule = (jnp.zeros((n_tiles,), jnp.int32), jnp.full((n_tiles,), sub_tiles, jnp.int32),
                    jnp.full((1,), n_tiles, jnp.int32))
    cb0 = col0 // tn
    ub0 = up_col0 // tn

    def row_of(t, nu):
        return jnp.maximum(jnp.minimum(t, nu[0] - 1), 0)

    a_spec = pl.BlockSpec((tm, a.shape[1]), lambda j, t, te, nv, nu: (row_of(t, nu), 0))
    w_spec = pl.BlockSpec((None, k, tn), lambda j, t, te, nv, nu: (te[row_of(t, nu)], 0, cb0 + j),
                          pipeline_mode=w_mode)
    o_spec = pl.BlockSpec((tm, tn), lambda j, t, te, nv, nu: (t, j))
    in_specs = [a_spec, w_spec]
    args = [a, w]
    scratch = [pltpu.VMEM((k, tn), BF16)]
    if mode == "swiglu":
        in_specs.append(pl.BlockSpec((None, k, tn), lambda j, t, te, nv, nu: (te[row_of(t, nu)], 0, ub0 + j),
                                     pipeline_mode=w_mode))
        args.append(w)
        scratch.append(pltpu.VMEM((k, tn), BF16))
    elif mode == "residual":
        in_specs.append(o_spec)
        args.append(residual)
    elif mode == "rope":
        tab_spec = pl.BlockSpec((tm, LANES), lambda j, t, te, nv, nu: (row_of(t, nu), 0))
        in_specs += [tab_spec, tab_spec]
        args += list(rope_tabs)
    kern = functools.partial(_mm_kernel, mode=mode, a_packed=a_packed, rope_cols=rope_cols, scale=scale,
                             sub_tiles=sub_tiles)
    return pl.pallas_call(
        kern,
        grid_spec=pltpu.PrefetchScalarGridSpec(
            num_scalar_prefetch=3, grid=(n_col, n_tiles),
            in_specs=in_specs, out_specs=o_spec, scratch_shapes=scratch),
        out_shape=jax.ShapeDtypeStruct((rows, n_out), out_dtype),
        compiler_params=_params(("arbitrary", "arbitrary"), vmem_mib),
        name=name,
    )(*schedule, *args)


GLA_ROW_BLOCK = 512


def _gate_kernel(x_ref, wr_ref, wgk_ref, bgk_ref, g_ref):
    r = jnp.dot(x_ref[...], wr_ref[...].astype(BF16), preferred_element_type=F32)
    r_hi = r.astype(BF16)
    r_lo = (r - r_hi.astype(F32)).astype(BF16)
    w = wgk_ref[...]
    w_hi = w.astype(BF16)
    w_lo = (w - w_hi.astype(F32)).astype(BF16)
    z = (jnp.dot(r_hi, w_hi, preferred_element_type=F32) + jnp.dot(r_hi, w_lo, preferred_element_type=F32)
         + jnp.dot(r_lo, w_hi, preferred_element_type=F32) + bgk_ref[...])
    g_ref[...] = (jnp.minimum(z, 0.0) - jnp.log(1.0 + jnp.exp(-jnp.abs(z)))) / GLA_GATE_NORM


def gate_rows(xn, w_r_pad, wgk_pad, bgk):
    rows, d = xn.shape
    width = wgk_pad.shape[1]
    return pl.pallas_call(
        _gate_kernel,
        grid=(rows // ROW_TILE,),
        in_specs=[pl.BlockSpec((ROW_TILE, d), lambda i: (i, 0)),
                  pl.BlockSpec((d, LANES), lambda i: (0, 0)),
                  pl.BlockSpec((LANES, width), lambda i: (0, 0)),
                  pl.BlockSpec((1, width), lambda i: (0, 0))],
        out_specs=pl.BlockSpec((ROW_TILE, width), lambda i: (i, 0)),
        out_shape=jax.ShapeDtypeStruct((rows, width), F32),
        compiler_params=_params(("parallel",), 32),
        name="gate_rows",
    )(xn, w_r_pad, wgk_pad, bgk)


def _gla_kernel(q_ref, k_ref, v_ref, go_ref, g_ref, gn_ref, s0_ref,
                o_ref, sfin_ref, st_ref, *, n_chunks, n_valid):
    rb = pl.program_id(1)

    @pl.when(rb == 0)
    def _():
        for h in range(GLA_HEADS):
            st_ref[h] = s0_ref[h].T

    ri = lax.broadcasted_iota(jnp.int32, (CHUNK, CHUNK), 0)
    ci = lax.broadcasted_iota(jnp.int32, (CHUNK, CHUNK), 1)
    causal = ci <= ri
    tril = jnp.where(causal, 1.0, 0.0).astype(BF16)
    row_in_chunk = lax.broadcasted_iota(jnp.int32, (CHUNK, 1), 0)
    gn = gn_ref[...]

    def body(c, carry):
        rows = pl.ds(pl.multiple_of(c * CHUNK, CHUNK), CHUNK)
        row_valid = (rb * n_chunks + c) * CHUNK + row_in_chunk < n_valid
        g = jnp.where(row_valid, g_ref[rows, :], 0.0)
        g_hi = g.astype(BF16)
        rest = g - g_hi.astype(F32)
        g_mid = rest.astype(BF16)
        g_lo = (rest - g_mid.astype(F32)).astype(BF16)
        b = (jnp.dot(tril, g_hi, preferred_element_type=F32) + jnp.dot(tril, g_mid, preferred_element_type=F32)
             + jnp.dot(tril, g_lo, preferred_element_type=F32))
        b_last = b[CHUNK - 1:CHUNK, :]
        q = q_ref[rows, :].astype(F32) * (GLA_DK ** -0.5)
        k = jnp.where(row_valid, k_ref[rows, :].astype(F32), 0.0)
        q_dec_all = (q * jnp.exp(b)).astype(BF16)
        k_dec_all = (k * jnp.exp(-b)).astype(BF16)
        k_last_all = (k * jnp.exp(b_last - b)).astype(BF16)
        decay = jnp.exp(b_last)
        for h in range(GLA_HEADS):
            ks = slice(h * GLA_DK, (h + 1) * GLA_DK)
            vs = slice(h * GLA_DV, (h + 1) * GLA_DV)
            q_dec = q_dec_all[:, ks]
            v = v_ref[rows, vs]
            att = lax.dot_general(q_dec, k_dec_all[:, ks], (((1,), (1,)), ((), ())),
                                  preferred_element_type=F32)
            att = jnp.where(causal, att, 0.0).astype(BF16)
            st = st_ref[h]
            o = jnp.dot(att, v, preferred_element_type=F32)
            o = o + lax.dot_general(q_dec, st.astype(BF16), (((1,), (1,)), ((), ())),
                                    preferred_element_type=F32)
            st_ref[h] = st * decay[:, ks] + lax.dot_general(
                v, k_last_all[:, ks], (((0,), (0,)), ((), ())), preferred_element_type=F32)
            on = o * lax.rsqrt(jnp.mean(o * o, axis=-1, keepdims=True) + EPS) * gn
            go = go_ref[rows, vs].astype(F32)
            o_ref[rows, vs] = (on * (go * jax.nn.sigmoid(go))).astype(o_ref.dtype)
        return carry

    lax.fori_loop(0, n_chunks, body, 0, unroll=4 if n_chunks % 4 == 0 else 1)

    @pl.when(rb == pl.num_programs(1) - 1)
    def _():
        for h in range(GLA_HEADS):
            sfin_ref[h] = st_ref[h].T


def gla_scan(proj, gates, gnorm, s0, *, n_seq, seq_len, row0, n_valid, s0_per_seq):
    blk = min(seq_len, GLA_ROW_BLOCK)
    n_rb = seq_len // blk
    hk = GLA_HEADS * GLA_DK
    hv = GLA_HEADS * GLA_DV
    rb0 = row0 // blk
    row = lambda b, r: rb0 + b * n_rb + r
    s0_map = (lambda b, r: (b, 0, 0, 0)) if s0_per_seq else (lambda b, r: (0, 0, 0, 0))
    kern = functools.partial(_gla_kernel, n_chunks=blk // CHUNK, n_valid=n_valid)
    return pl.pallas_call(
        kern,
        grid=(n_seq, n_rb),
        in_specs=[
            pl.BlockSpec((blk, hk), lambda b, r: (row(b, r), 0)),
            pl.BlockSpec((blk, hk), lambda b, r: (row(b, r), 1)),
            pl.BlockSpec((blk, hv), lambda b, r: (row(b, r), 2 * hk // hv)),
            pl.BlockSpec((blk, hv), lambda b, r: (row(b, r), 2 * hk // hv + 1)),
            pl.BlockSpec((blk, hk), lambda b, r: (row(b, r), 0)),
            pl.BlockSpec((1, GLA_DV), lambda b, r: (0, 0)),
            pl.BlockSpec((None, GLA_HEADS, GLA_DK, GLA_DV), s0_map),
        ],
        out_specs=[
            pl.BlockSpec((blk, hv), lambda b, r: (b * n_rb + r, 0)),
            pl.BlockSpec((None, GLA_HEADS, GLA_DK, GLA_DV), lambda b, r: (b, 0, 0, 0)),
        ],
        out_shape=[jax.ShapeDtypeStruct((n_seq * seq_len, hv), BF16),
                   jax.ShapeDtypeStruct((n_seq, GLA_HEADS, GLA_DK, GLA_DV), F32)],
        scratch_shapes=[pltpu.VMEM((GLA_HEADS, GLA_DV, GLA_DK), F32)],
        compiler_params=_params(("parallel", "arbitrary"), 48),
        name="gla_scan",
    )(proj, proj, proj, proj, gates, gnorm, s0)


WIN_KEYS = (WINDOW_CHUNKS + 1) * CHUNK
ATTN_KEYS = 2 * LANES
ATTN_ROW_BLOCK = 512
PAIRS = SWA_GROUP // 2


def _lane_halves(x2, head_in_pair):
    lane = lax.broadcasted_iota(jnp.int32, x2.shape, 1)
    low = lane < HEAD_DIM
    swapped = pltpu.roll(x2, HEAD_DIM, 1)
    if head_in_pair == 0:
        lo, hi = jnp.where(low, x2, 0.0), jnp.where(low, 0.0, swapped)
    else:
        lo, hi = jnp.where(low, swapped, 0.0), jnp.where(low, 0.0, x2)
    return lo.astype(BF16), hi.astype(BF16)


def _attn_kernel(q_ref, k_ref, v_ref, km_ref, vm_ref, sink_ref, o_ref, kb_ref, vb_ref, kmb_ref, vmb_ref,
                 *, n_chunks, chunk_offset):
    step = pl.program_id(1)

    @pl.when(step == 0)
    def _():
        for h in range(SWA_KV_HEADS):
            cols = slice((h // 2) * LANES, (h // 2 + 1) * LANES)
            for src, dst in ((k_ref, kb_ref), (v_ref, vb_ref), (km_ref, kmb_ref), (vm_ref, vmb_ref)):
                lo, hi = _lane_halves(src[:, cols], h % 2)
                dst[h, 0] = lo
                dst[h, 1] = hi

    j = lax.broadcasted_iota(jnp.int32, (1, ATTN_KEYS), 1)
    rel_chunk = (j >= N_META + CHUNK).astype(jnp.int32) + (j >= N_META + 2 * CHUNK).astype(jnp.int32)
    in_window = (j >= N_META) & (j < N_META + WIN_KEYS)
    low_lanes = lax.broadcasted_iota(jnp.int32, (1, LANES), 1) < HEAD_DIM
    zpad = jnp.zeros((ATTN_KEYS - N_META - WIN_KEYS, LANES), BF16)
    key_row = lax.broadcasted_iota(jnp.int32, (2 * ATTN_KEYS, LANES), 0)
    key_lane = lax.broadcasted_iota(jnp.int32, (2 * ATTN_KEYS, LANES), 1)
    ones_cols = ((key_row < ATTN_KEYS) == (key_lane < HEAD_DIM)).astype(BF16)
    nt = (((1,), (1,)), ((), ()))

    def body(ci, carry):
        c = step * n_chunks + ci + chunk_offset
        wc = jnp.maximum(c - WINDOW_CHUNKS, 0)
        win = pl.ds(pl.multiple_of(wc * CHUNK, CHUNK), WIN_KEYS)
        rows = pl.ds(pl.multiple_of(ci * CHUNK, CHUNK), CHUNK)
        valid = (j < N_META) | (in_window & (wc + rel_chunk <= c))
        for h in range(SWA_KV_HEADS):
            kb = jnp.concatenate([kmb_ref[h, 0], kb_ref[h, 0, win, :], zpad,
                                  kmb_ref[h, 1], kb_ref[h, 1, win, :], zpad], axis=0)
            vb = jnp.concatenate([vmb_ref[h, 0], vb_ref[h, 0, win, :], zpad,
                                  vmb_ref[h, 1], vb_ref[h, 1, win, :], zpad], axis=0)
            col = lambda p: slice((h * PAIRS + p) * LANES, (h * PAIRS + p + 1) * LANES)
            qs = jnp.concatenate([q_ref[rows, col(p)] for p in range(PAIRS)], axis=0)
            s = lax.dot_general(qs, kb, nt, preferred_element_type=F32)
            probs, sink_terms = [], []
            for half in range(2):
                sh = jnp.where(valid, s[:, half * ATTN_KEYS:(half + 1) * ATTN_KEYS], NEG_INF)
                sink = sink_ref[h, half][:, 0:1]
                m = jnp.maximum(jnp.max(sh, axis=-1, keepdims=True), sink)
                probs.append(jnp.exp(sh - m).astype(BF16))
                sink_terms.append(jnp.exp(sink - m))
            ov = jnp.dot(jnp.concatenate(probs, axis=1), jnp.concatenate([vb, ones_cols], axis=1),
                         preferred_element_type=F32)
            den = ov[:, LANES:] + jnp.where(low_lanes, sink_terms[0], sink_terms[1])
            o = ov[:, :LANES] / den
            for p in range(PAIRS):
                o_ref[rows, col(p)] = o[p * CHUNK:(p + 1) * CHUNK].astype(o_ref.dtype)
        return carry

    lax.fori_loop(0, n_chunks, body, 0, unroll=4 if n_chunks % 4 == 0 else 1)


def window_attention(q, k_arr, v_arr, km_arr, vm_arr, sink_tab, *, n_seq, seq_len, key_len, row0,
                     k_map, v_map, km_map, vm_map, chunk_offset):
    blk = min(seq_len, ATTN_ROW_BLOCK)
    steps = seq_len // blk
    q_spec = pl.BlockSpec((blk, q.shape[1]), lambda b, s: (row0 // blk + b * steps + s, 0))
    o_spec = pl.BlockSpec((blk, q.shape[1]), lambda b, s: (b * steps + s, 0))
    width = SWA_KV_HEADS * HEAD_DIM
    kern = functools.partial(_attn_kernel, n_chunks=blk // CHUNK, chunk_offset=chunk_offset)
    return pl.pallas_call(
        kern,
        grid=(n_seq, steps),
        in_specs=[q_spec,
                  pl.BlockSpec((key_len, width), k_map),
                  pl.BlockSpec((key_len, width), v_map),
                  pl.BlockSpec((N_META, width), km_map),
                  pl.BlockSpec((N_META, width), vm_map),
                  pl.BlockSpec((SWA_KV_HEADS, 2, PAIRS * CHUNK, LANES), lambda b, s: (0, 0, 0, 0))],
        out_specs=o_spec,
        out_shape=jax.ShapeDtypeStruct((n_seq * seq_len, q.shape[1]), q.dtype),
        scratch_shapes=[pltpu.VMEM((SWA_KV_HEADS, 2, key_len, LANES), BF16),
                        pltpu.VMEM((SWA_KV_HEADS, 2, key_len, LANES), BF16),
                        pltpu.VMEM((SWA_KV_HEADS, 2, N_META, LANES), BF16),
                        pltpu.VMEM((SWA_KV_HEADS, 2, N_META, LANES), BF16)],
        compiler_params=_params(("parallel", "arbitrary"), 48),
        name="window_attention",
    )(q, k_arr, v_arr, km_arr, vm_arr, sink_tab)


def _route_tile(xn, wr_ref, xp_ref, rt_ref, cnt_ref, carry_ref):
    bits = pltpu.bitcast(xn.astype(BF16).astype(F32), jnp.uint32)
    half = D_MODEL // 2
    xp_ref[...] = (bits[:, :half] >> 16) | (bits[:, half:] & jnp.uint32(0xFFFF0000))

    tm = xn.shape[0]
    hi = xn.astype(BF16)
    lo = (xn - hi.astype(F32)).astype(BF16)
    wide = jnp.dot(hi, wr_ref[...], preferred_element_type=F32)
    logits = (wide[:, :LANES] + wide[:, LANES:]
              + jnp.dot(lo, wr_ref[:, :LANES], preferred_element_type=F32))
    lane = lax.broadcasted_iota(jnp.int32, (tm, LANES), 1).astype(F32)
    neg = jnp.float32(-jnp.inf)
    lg = jnp.where(lane < N_EXPERTS, logits, neg)
    m1 = jnp.max(lg, axis=-1, keepdims=True)
    i1 = jnp.min(jnp.where(lg == m1, lane, float(LANES)), axis=-1, keepdims=True)
    lg2 = jnp.where(lane == i1, neg, lg)
    m2 = jnp.max(lg2, axis=-1, keepdims=True)
    i2 = jnp.min(jnp.where(lg2 == m2, lane, float(LANES)), axis=-1, keepdims=True)
    e21 = jnp.exp(m2 - m1)
    g1 = 1.0 / (1.0 + e21)
    g2 = e21 / (1.0 + e21)
    oh1 = (lane == i1).astype(F32)
    oh2 = (lane == i2).astype(F32)
    oh = oh1 + oh2
    ri = lax.broadcasted_iota(jnp.int32, (tm, tm), 0)
    ci = lax.broadcasted_iota(jnp.int32, (tm, tm), 1)
    before = (ci < ri).astype(BF16)
    prefix = jnp.dot(before, oh.astype(BF16), preferred_element_type=F32) + carry_ref[0:1, :]
    rank1 = jnp.sum(prefix * oh1, axis=-1, keepdims=True)
    rank2 = jnp.sum(prefix * oh2, axis=-1, keepdims=True)
    total = carry_ref[0:1, :] + jnp.sum(oh, axis=0, keepdims=True)
    carry_ref[...] = jnp.broadcast_to(total, carry_ref.shape)
    cnt_ref[...] = jnp.broadcast_to(total, cnt_ref.shape)
    rt = jnp.where(lane == 0, i1, 0.0)
    rt = jnp.where(lane == 1, i2, rt)
    rt = jnp.where(lane == 2, rank1, rt)
    rt = jnp.where(lane == 3, rank2, rt)
    rt = jnp.where(lane == 4, g1, rt)
    rt = jnp.where(lane == 5, g2, rt)
    rt_ref[...] = rt


def _rowmm_kernel(*refs, router, a_bounds, res_bounds):
    na, nr = len(a_bounds), len(res_bounds)
    a_refs, w_ref, res_refs, g_ref = refs[:na], refs[na], refs[na + 1:na + 1 + nr], refs[na + 1 + nr]
    rest = refs[na + 2 + nr:]
    t = pl.program_id(0)
    if router:
        wr_ref, h_ref, xp_ref, rt_ref, cnt_ref, wb_ref, carry_ref, wcat_ref = rest
    else:
        h_ref, xn_ref, wb_ref = rest

    @pl.when(t == 0)
    def _():
        wb_ref[...] = w_ref[...].astype(BF16)
        if router:
            carry_ref[...] = jnp.zeros_like(carry_ref)
            wr = wr_ref[...]
            wr_hi = wr.astype(BF16)
            wcat_ref[:, :LANES] = wr_hi
            wcat_ref[:, LANES:] = (wr - wr_hi.astype(F32)).astype(BF16)

    h = _read_part(res_refs, res_bounds, t) + jnp.dot(_read_part(a_refs, a_bounds, t), wb_ref[...],
                                                     preferred_element_type=F32)
    h_ref[...] = h
    xn = h * lax.rsqrt(jnp.mean(h * h, axis=-1, keepdims=True) + EPS) * g_ref[...]
    if router:
        _route_tile(xn, wcat_ref, xp_ref, rt_ref, cnt_ref, carry_ref)
    else:
        xn_ref[...] = xn.astype(xn_ref.dtype)


def row_matmul(a_parts, w, res_parts, gain, *, rows, tm, w_router_pad=None, name="row_matmul"):
    k, d = w.shape[1], w.shape[2]
    router = w_router_pad is not None
    row_spec = lambda width: pl.BlockSpec((tm, width), lambda t: (t, 0))
    a_specs, a_bounds = _part_specs(a_parts, tm)
    res_specs, res_bounds = _part_specs(res_parts, tm)
    assert min(a_bounds[-1], res_bounds[-1]) * tm >= rows
    in_specs = (a_specs + [pl.BlockSpec((None, k, d), lambda t: (0, 0, 0), pipeline_mode=pl.Buffered(1))]
                + res_specs + [pl.BlockSpec((1, d), lambda t: (0, 0))])
    args = [*a_parts, w, *res_parts, gain]
    out_specs = [row_spec(d)]
    out_shape = [jax.ShapeDtypeStruct((rows, d), F32)]
    scratch = [pltpu.VMEM((k, d), BF16)]
    if router:
        in_specs.append(pl.BlockSpec((d, LANES), lambda t: (0, 0)))
        args.append(w_router_pad)
        out_specs += [row_spec(d // 2), row_spec(LANES), pl.BlockSpec((8, LANES), lambda t: (0, 0))]
        out_shape += [jax.ShapeDtypeStruct((rows, d // 2), jnp.uint32),
                      jax.ShapeDtypeStruct((rows, LANES), F32),
                      jax.ShapeDtypeStruct((8, LANES), F32)]
        scratch += [pltpu.VMEM((8, LANES), F32), pltpu.VMEM((d, 2 * LANES), BF16)]
    else:
        out_specs.append(row_spec(d))
        out_shape.append(jax.ShapeDtypeStruct((rows, d), BF16))
    return pl.pallas_call(
        functools.partial(_rowmm_kernel, router=router, a_bounds=a_bounds, res_bounds=res_bounds),
        grid=(rows // tm,),
        in_specs=in_specs, out_specs=out_specs, out_shape=out_shape, scratch_shapes=scratch,
        compiler_params=_params(("arbitrary",), 56),
        name=name,
    )(*args)


def _scatter_kernel(pos_ref, xp_ref, xs_in_ref, xs_ref, sem):
    del xs_in_ref
    tm = xp_ref.shape[0]

    def row_copy(r, slot):
        return pltpu.make_async_copy(xp_ref.at[pl.ds(r, 1), :], xs_ref.at[pl.ds(slot, 1), :], sem)

    def issue(r, carry):
        row_copy(r, pos_ref[2 * r]).start()
        row_copy(r, pos_ref[2 * r + 1]).start()
        return carry

    lax.fori_loop(0, tm, issue, 0, unroll=8)
    for _ in range(2):
        pltpu.make_async_copy(xp_ref, xs_ref.at[pl.ds(0, tm), :], sem).wait()


def scatter_rows(pos_flat, xp, xs_init):
    rows = xp.shape[0]
    return pl.pallas_call(
        _scatter_kernel,
        grid=(rows // ROW_TILE,),
        in_specs=[pl.BlockSpec((2 * ROW_TILE,), lambda i: (i,), memory_space=pltpu.SMEM),
                  pl.BlockSpec((ROW_TILE, xp.shape[1]), lambda i: (i, 0)),
                  pl.BlockSpec(memory_space=pl.ANY)],
        out_specs=pl.BlockSpec(memory_space=pl.ANY),
        out_shape=jax.ShapeDtypeStruct(xs_init.shape, xs_init.dtype),
        scratch_shapes=[pltpu.SemaphoreType.DMA(())],
        input_output_aliases={2: 0},
        compiler_params=_params(("arbitrary",), 32),
        name="scatter_rows",
    )(pos_flat, xp, xs_init)


def _combine_kernel(pos_ref, rt_ref, h_ref, g_ref, ys_ref, yp_ref, ysm_ref, buf_ref, sem, *, n_prompt_tiles):
    i = pl.program_id(0)
    tm = h_ref.shape[0]

    def row_copy(r, k, slot):
        return pltpu.make_async_copy(ys_ref.at[pl.ds(slot, 1), :], buf_ref.at[k, pl.ds(r, 1), :], sem)

    def issue(r, carry):
        row_copy(r, 0, pos_ref[2 * r]).start()
        row_copy(r, 1, pos_ref[2 * r + 1]).start()
        return carry

    lax.fori_loop(0, tm, issue, 0, unroll=8)
    for k in range(2):
        pltpu.make_async_copy(ys_ref.at[pl.ds(0, tm), :], buf_ref.at[k], sem).wait()
    rt = rt_ref[...]
    y = rt[:, 4:5] * buf_ref[0] + rt[:, 5:6] * buf_ref[1]
    x = h_ref[...] + y
    out = x * lax.rsqrt(jnp.mean(x * x, axis=-1, keepdims=True) + EPS) * g_ref[...]

    @pl.when(i < n_prompt_tiles)
    def _():
        yp_ref[...] = out

    @pl.when(i >= n_prompt_tiles)
    def _():
        ysm_ref[...] = out


def combine_rows(pos_flat, rt, h, gain, ys, *, n_prompt_rows):
    rows = h.shape[0]
    npt = n_prompt_rows // ROW_TILE
    kern = functools.partial(_combine_kernel, n_prompt_tiles=npt)
    return pl.pallas_call(
        kern,
        grid=(rows // ROW_TILE,),
        in_specs=[pl.BlockSpec((2 * ROW_TILE,), lambda i: (i,), memory_space=pltpu.SMEM),
                  pl.BlockSpec((ROW_TILE, LANES), lambda i: (i, 0)),
                  pl.BlockSpec((ROW_TILE, D_MODEL), lambda i: (i, 0)),
                  pl.BlockSpec((1, D_MODEL), lambda i: (0, 0)),
                  pl.BlockSpec(memory_space=pl.ANY)],
        out_specs=[pl.BlockSpec((ROW_TILE, D_MODEL), lambda i: (jnp.minimum(i, npt - 1), 0)),
                   pl.BlockSpec((ROW_TILE, D_MODEL), lambda i: (jnp.maximum(i - npt, 0), 0))],
        out_shape=[jax.ShapeDtypeStruct((n_prompt_rows, D_MODEL), F32),
                   jax.ShapeDtypeStruct((rows - n_prompt_rows, D_MODEL), F32)],
        scratch_shapes=[pltpu.VMEM((2, ROW_TILE, D_MODEL), F32), pltpu.SemaphoreType.DMA(())],
        compiler_params=_params(("arbitrary",), 48),
        name="combine_rows",
    )(pos_flat, rt, h, gain, ys)


def _rope_tables(pos):
    half = HEAD_DIM // 2
    inv_freq = ROPE_THETA ** (-jnp.arange(half, dtype=F32) / half)
    ang = jnp.asarray(pos, jnp.int32).astype(F32)[:, None] * inv_freq[None, :]
    cos = jnp.cos(ang)
    sin = jnp.sin(ang)
    reps = LANES // HEAD_DIM
    return (jnp.tile(jnp.concatenate([cos, cos], axis=1), (1, reps)),
            jnp.tile(jnp.concatenate([-sin, sin], axis=1), (1, reps)))


def kernel(x_prompt, x_sample, state_gla, cache_k_meta, cache_v_meta, cache_k_win, cache_v_win, meta_tokens,
           norm_mix, norm_ffn, norm_kv, norm_final, gla_w_in, gla_w_gk, gla_b_gk, gla_norm, gla_w_out, kv_w,
           attn_w_q, attn_sinks, attn_w_out, ffn_w_gate_up, ffn_w_down, moe_w_router, moe_w_gate_up,
           moe_w_down):
    bsz, seq, d = x_prompt.shape
    dbsz, t_new, _ = x_sample.shape
    past_len = 2048
    n_p = bsz * seq
    n_s = dbsz * t_new
    r1 = n_p + n_s
    assert r1 % ROW_TILE == 0 and seq % CHUNK == 0 and t_new == CHUNK
    r0 = -(-(r1 + CHUNK) // ROW_TILE) * ROW_TILE
    meta_row = r1
    hk = GLA_HEADS * GLA_DK
    hv = GLA_HEADS * GLA_DV
    ffn_dense = ffn_w_down.shape[1]
    ffn_expert = moe_w_down.shape[2]
    kv_width = SWA_KV_HEADS * HEAD_DIM

    x_parts = [x_prompt.reshape(n_p, d),
               jnp.concatenate([x_sample.reshape(n_s, d), meta_tokens.astype(F32),
                                jnp.zeros((r0 - r1 - N_META, d), F32)], axis=0)]
    tail_rows = r0 - r1

    tm0 = r0 // 16
    tm1 = r1 // 16
    (xn0,) = rms_rows(x_parts, norm_mix[0:1])
    proj = matmul(xn0, gla_w_in, n_out=2 * hk + 2 * hv, out_dtype=BF16, tm=tm0, name="gla_in_proj")
    w_r_pad = jnp.pad(gla_w_in[0, :, 2 * hk + 2 * hv:], ((0, 0), (0, LANES - GLA_RANK)))
    wgk_pad = jnp.pad(gla_w_gk[0], ((0, LANES - GLA_RANK), (0, 0)))
    gnorm = gla_norm[0][None, :]
    gla = functools.partial(gla_scan, proj, gate_rows(xn0, w_r_pad, wgk_pad, gla_b_gk[0][None, :]), gnorm)
    og_meta, s_meta = gla(jnp.zeros((1, GLA_HEADS, GLA_DK, GLA_DV), F32), n_seq=1, seq_len=tail_rows,
                          row0=meta_row, n_valid=N_META, s0_per_seq=False)
    og_prompt, s_prompt = gla(s_meta, n_seq=bsz, seq_len=seq, row0=0, n_valid=seq, s0_per_seq=False)
    og_sample, s_sample = gla(state_gla[0].astype(F32), n_seq=dbsz, seq_len=t_new, row0=n_p,
                              n_valid=t_new, s0_per_seq=True)
    h1, hn1 = row_matmul([og_prompt, jnp.concatenate([og_sample, og_meta], axis=0)], gla_w_out, x_parts,
                         norm_ffn[0:1], rows=r0, tm=r0 // 68, name="gla_out_proj")

    act = matmul(hn1, ffn_w_gate_up, n_out=ffn_dense, out_dtype=BF16, mode="swiglu", up_col0=ffn_dense,
                 tm=tm0, name="ffn_gate_up")
    h2 = matmul(act, ffn_w_down, n_out=d, out_dtype=F32, mode="residual", residual=h1, tm=tm0 // 2,
                vmem_mib=56, name="ffn_down")

    pos = np.concatenate([np.tile(N_META + np.arange(seq), bsz),
                          np.tile(N_META + past_len + np.arange(t_new), dbsz),
                          np.arange(N_META), np.zeros(r0 - r1 - N_META, np.int64)])
    rope_tabs = _rope_tables(pos)
    xkv, xq = rms_rows([h2], jnp.stack([norm_kv, norm_mix[1]]))
    kvf = matmul(xkv, kv_w[None], n_out=2 * kv_width, out_dtype=F32, mode="rope", rope_tabs=rope_tabs,
                 rope_cols=kv_width, tm=tm0, name="shared_kv")
    q = matmul(xq, attn_w_q, n_out=d, out_dtype=BF16, mode="rope", rope_tabs=rope_tabs, rope_cols=d,
               scale=HEAD_DIM ** -0.5, rows=r1, tm=tm1, name="attn_q")

    sink_tab = jnp.broadcast_to(
        jnp.repeat(attn_sinks[0].astype(F32).reshape(SWA_KV_HEADS, PAIRS, 2).transpose(0, 2, 1), CHUNK, axis=2)
        [..., None], (SWA_KV_HEADS, 2, PAIRS * CHUNK, LANES))
    meta_blk = meta_row // N_META
    o_prompt = window_attention(
        q, kvf, kvf, kvf, kvf, sink_tab, n_seq=bsz, seq_len=seq, key_len=seq, row0=0,
        k_map=lambda b, s: (b, 0), v_map=lambda b, s: (b, 1),
        km_map=lambda b, s: (meta_blk, 0), vm_map=lambda b, s: (meta_blk, 1), chunk_offset=0)
    k_new = kvf[n_p:r1, :kv_width]
    v_new = kvf[n_p:r1, kv_width:]
    win = cache_k_win.shape[1]
    ks = jnp.concatenate([cache_k_win.reshape(dbsz, win, kv_width).astype(F32),
                          k_new.reshape(dbsz, t_new, kv_width)], axis=1).reshape(dbsz * (win + t_new), kv_width)
    vs = jnp.concatenate([cache_v_win.reshape(dbsz, win, kv_width).astype(F32),
                          v_new.reshape(dbsz, t_new, kv_width)], axis=1).reshape(dbsz * (win + t_new), kv_width)
    assert win + t_new == (WINDOW_CHUNKS + 1) * CHUNK
    o_sample = window_attention(
        q, ks, vs, cache_k_meta.reshape(dbsz * N_META, kv_width).astype(F32),
        cache_v_meta.reshape(dbsz * N_META, kv_width).astype(F32), sink_tab,
        n_seq=dbsz, seq_len=t_new, key_len=win + t_new, row0=n_p,
        k_map=lambda b, s: (b, 0), v_map=lambda b, s: (b, 0),
        km_map=lambda b, s: (b, 0), vm_map=lambda b, s: (b, 0), chunk_offset=WINDOW_CHUNKS)
    w_router_pad = jnp.pad(moe_w_router[0], ((0, 0), (0, LANES - N_EXPERTS)))
    h3, xp, rt, cnt = row_matmul([o_prompt, o_sample], attn_w_out, [h2], norm_ffn[1:2], rows=r1, tm=r1 // 66,
                                 w_router_pad=w_router_pad, name="attn_out_route")
    unit, per_tile = MOE_UNIT, MOE_TILE // MOE_UNIT
    n_tiles_max = 2 * r1 // MOE_TILE + N_EXPERTS
    counts = cnt[0, :N_EXPERTS].astype(jnp.int32)
    units_per = (counts + unit - 1) // unit
    tiles_per = (units_per + per_tile - 1) // per_tile
    tile_end = jnp.cumsum(tiles_per)
    tile_start = tile_end - tiles_per
    experts = rt[:, 0:2].astype(jnp.int32)
    ranks = rt[:, 2:4].astype(jnp.int32)
    pos_flat = (tile_start[experts] * MOE_TILE + ranks).reshape(-1)
    n_used = tile_end[-1:].astype(jnp.int32)
    tile_id = jnp.arange(n_tiles_max, dtype=jnp.int32)
    tile_expert = jnp.minimum(jnp.sum(tile_id[:, None] >= tile_end[None, :], axis=1), N_EXPERTS - 1).astype(jnp.int32)
    tile_units = jnp.clip(units_per[tile_expert] - per_tile * (tile_id - tile_start[tile_expert]), 0, per_tile)
    tile_units = jnp.where(tile_id < n_used[0], tile_units, 0).astype(jnp.int32)
    xs = scatter_rows(pos_flat, xp, jnp.zeros((n_tiles_max * MOE_TILE, d // 2), jnp.uint32))
    act_e = matmul(xs, moe_w_gate_up.reshape(N_EXPERTS, d, 2 * ffn_expert), n_out=ffn_expert, out_dtype=BF16,
                   mode="swiglu", up_col0=ffn_expert, schedule=(tile_expert, tile_units, n_used),
                   sub_tiles=per_tile, tm=MOE_TILE, a_packed=True, vmem_mib=56, name="moe_gate_up")
    split = MOE_TILE // MOE_DOWN_TILE
    down_id = jnp.arange(n_tiles_max * split, dtype=jnp.int32)
    down_valid = ((down_id % split) * MOE_DOWN_TILE // unit < jnp.repeat(tile_units, split)).astype(jnp.int32)
    ys = matmul(act_e, moe_w_down.reshape(N_EXPERTS, ffn_expert, d), n_out=d, out_dtype=F32,
                schedule=(jnp.repeat(tile_expert, split), down_valid, split * n_used), tm=MOE_DOWN_TILE,
                vmem_mib=56, name="moe_down")
    y_prompt, y_sample = combine_rows(pos_flat, rt, h3, norm_final[None, :], ys, n_prompt_rows=n_p)

    kv_meta = kvf[meta_row:meta_row + N_META]
    k_meta_p = jnp.broadcast_to(kv_meta[None, :, :kv_width], (bsz, N_META, kv_width))
    v_meta_p = jnp.broadcast_to(kv_meta[None, :, kv_width:], (bsz, N_META, kv_width))
    win_p = min(WINDOW_CHUNKS * CHUNK, seq)
    kv_win = kvf[:n_p].reshape(bsz, seq, 2 * kv_width)[:, seq - win_p:]
    shape4 = lambda a: a.reshape(a.shape[0], a.shape[1], SWA_KV_HEADS, HEAD_DIM)
    return (y_prompt.reshape(bsz, seq, d), y_sample.reshape(dbsz, t_new, d),
            s_prompt[None].astype(state_gla.dtype), s_sample[None].astype(state_gla.dtype),
            shape4(k_meta_p), shape4(v_meta_p),
            shape4(kv_win[:, :, :kv_width]), shape4(kv_win[:, :, kv_width:]),
            shape4(k_new.reshape(dbsz, t_new, kv_width)), shape4(v_new.reshape(dbsz, t_new, kv_width)))
```

```python
import functools

import numpy as np
import jax
import jax.numpy as jnp
from jax import lax
from jax.experimental import pallas as pl
from jax.experimental.pallas import tpu as pltpu

F32 = jnp.float32
BF16 = jnp.bfloat16
HIGHEST = lax.Precision.HIGHEST

D_MODEL = 2048
CHUNK = 64
N_META = 16
GLA_HEADS = 4
GLA_DK = 256
GLA_DV = 512
GLA_RANK = 16
GLA_GATE_NORM = 16.0
HEAD_DIM = 64
SWA_HEADS = 32
SWA_KV_HEADS = 4
SWA_GROUP = 8
WINDOW_CHUNKS = 2
ROPE_THETA = 10000.0
N_EXPERTS = 8
EPS = 1e-5
NEG_INF = -1e30

LANES = 128
ROW_TILE = 512
COL_TILE = 512
MOE_TILE = 1024
MOE_UNIT = 512
MOE_DOWN_TILE = 512
MIB = 2 ** 20


def _params(semantics, vmem_mib):
    return pltpu.CompilerParams(dimension_semantics=semantics, vmem_limit_bytes=vmem_mib * MIB)


def _part_specs(parts, tm):
    specs, bounds, start = [], [], 0
    for p in parts:
        n = p.shape[0] // tm
        assert n * tm == p.shape[0]
        specs.append(pl.BlockSpec((tm, p.shape[1]), lambda t, s=start, n=n: (jnp.clip(t - s, 0, n - 1), 0)))
        start += n
        bounds.append(start)
    return specs, tuple(bounds)


def _read_part(refs, bounds, t):
    x = refs[0][...]
    for ref, lo in zip(refs[1:], bounds[:-1]):
        x = jnp.where(t >= lo, ref[...], x)
    return x


def _norm_kernel(*refs, bounds):
    n_src = len(bounds)
    g_ref = refs[n_src]
    x = _read_part(refs[:n_src], bounds, pl.program_id(0))
    y = x * lax.rsqrt(jnp.mean(x * x, axis=-1, keepdims=True) + EPS)
    for i, o_ref in enumerate(refs[n_src + 1:]):
        o_ref[...] = (y * g_ref[i:i + 1, :]).astype(o_ref.dtype)


def rms_rows(parts, gains):
    d = parts[0].shape[1]
    n = gains.shape[0]
    specs, bounds = _part_specs(parts, ROW_TILE)
    rows = bounds[-1] * ROW_TILE
    return pl.pallas_call(
        functools.partial(_norm_kernel, bounds=bounds),
        grid=(bounds[-1],),
        in_specs=specs + [pl.BlockSpec((n, d), lambda i: (0, 0))],
        out_specs=[pl.BlockSpec((ROW_TILE, d), lambda i: (i, 0)) for _ in range(n)],
        out_shape=[jax.ShapeDtypeStruct((rows, d), BF16) for _ in range(n)],
        compiler_params=_params(("parallel",), 40),
        name="rms_rows",
    )(*parts, gains)


def _swap_halves(x):
    lane = lax.broadcasted_iota(jnp.int32, x.shape, 1)
    first_half = (lane % HEAD_DIM) < (HEAD_DIM // 2)
    return jnp.where(first_half, pltpu.roll(x, LANES - HEAD_DIM // 2, 1), pltpu.roll(x, HEAD_DIM // 2, 1))


def _mm_kernel(te_ref, nv_ref, nu_ref, *refs, mode, a_packed, rope_cols, scale, sub_tiles):
    if mode == "swiglu":
        a_ref, w_ref, w2_ref, o_ref, wb_ref, wb2_ref = refs
    elif mode == "residual":
        a_ref, w_ref, res_ref, o_ref, wb_ref = refs
    elif mode == "rope":
        a_ref, w_ref, cos_ref, sin_ref, o_ref, wb_ref = refs
    else:
        a_ref, w_ref, o_ref, wb_ref = refs
    t = pl.program_id(1)
    tm = o_ref.shape[0]
    sub = tm // sub_tiles

    def compute(n_rows):
        rows = slice(0, n_rows)
        a = a_ref[rows, :]
        if a_packed:
            lo = pltpu.bitcast(a << 16, F32)
            hi = pltpu.bitcast(a & jnp.uint32(0xFFFF0000), F32)
            a = jnp.concatenate([lo, hi], axis=1)
        a = a.astype(BF16)
        acc = jnp.dot(a, wb_ref[...], preferred_element_type=F32)
        if mode == "swiglu":
            up = jnp.dot(a, wb2_ref[...], preferred_element_type=F32)
            o_ref[rows, :] = (acc * jax.nn.sigmoid(acc) * up).astype(o_ref.dtype)
        elif mode == "residual":
            o_ref[rows, :] = (res_ref[rows, :] + acc).astype(o_ref.dtype)
        elif mode == "rope":
            cos = cos_ref[rows, :]
            sin = sin_ref[rows, :]
            for c in range(acc.shape[1] // LANES):
                x = acc[:, c * LANES:(c + 1) * LANES]
                if c * LANES < rope_cols:
                    x = x * cos + _swap_halves(x) * sin
                o_ref[rows, c * LANES:(c + 1) * LANES] = (x * scale).astype(o_ref.dtype)
        else:
            o_ref[rows, :] = acc.astype(o_ref.dtype)
        if n_rows < tm:
            o_ref[n_rows:, :] = jnp.zeros((tm - n_rows, o_ref.shape[1]), o_ref.dtype)

    @pl.when(t < nu_ref[0])
    def _():
        @pl.when((t == 0) | (te_ref[t] != te_ref[jnp.maximum(t - 1, 0)]))
        def _():
            wb_ref[...] = w_ref[...].astype(BF16)
            if mode == "swiglu":
                wb2_ref[...] = w2_ref[...].astype(BF16)

    n_valid = jnp.where(t < nu_ref[0], nv_ref[t], 0)

    @pl.when(n_valid == 0)
    def _():
        o_ref[...] = jnp.zeros_like(o_ref)

    for s in range(1, sub_tiles + 1):
        pl.when(n_valid == s)(functools.partial(compute, s * sub))


def matmul(a, w, *, n_out, out_dtype, mode="plain", schedule=None, sub_tiles=1, residual=None,
           rope_tabs=None, rope_cols=0, scale=1.0, col0=0, up_col0=0, tm=ROW_TILE, tn=COL_TILE,
           a_packed=False, rows=None, vmem_mib=48, name="matmul"):
    rows = a.shape[0] if rows is None else rows
    k = w.shape[1]
    n_tiles = rows // tm
    n_col = n_out // tn
    w_mode = pl.Buffered(1) if schedule is None else None
    if schedule is None:
        schedule = (jnp.zeros((n_tiles,), jnp.int32), jnp.full((n_tiles,), sub_tiles, jnp.int32),
                    jnp.full((1,), n_tiles, jnp.int32))
    cb0 = col0 // tn
    ub0 = up_col0 // tn

    def row_of(t, nu):
        return jnp.maximum(jnp.minimum(t, nu[0] - 1), 0)

    a_spec = pl.BlockSpec((tm, a.shape[1]), lambda j, t, te, nv, nu: (row_of(t, nu), 0))
    w_spec = pl.BlockSpec((None, k, tn), lambda j, t, te, nv, nu: (te[row_of(t, nu)], 0, cb0 + j),
                          pipeline_mode=w_mode)
    o_spec = pl.BlockSpec((tm, tn), lambda j, t, te, nv, nu: (t, j))
    in_specs = [a_spec, w_spec]
    args = [a, w]
    scratch = [pltpu.VMEM((k, tn), BF16)]
    if mode == "swiglu":
        in_specs.append(pl.BlockSpec((None, k, tn), lambda j, t, te, nv, nu: (te[row_of(t, nu)], 0, ub0 + j),
                                     pipeline_mode=w_mode))
        args.append(w)
        scratch.append(pltpu.VMEM((k, tn), BF16))
    elif mode == "residual":
        in_specs.append(o_spec)
        args.append(residual)
    elif mode == "rope":
        tab_spec = pl.BlockSpec((tm, LANES), lambda j, t, te, nv, nu: (row_of(t, nu), 0))
        in_specs += [tab_spec, tab_spec]
        args += list(rope_tabs)
    kern = functools.partial(_mm_kernel, mode=mode, a_packed=a_packed, rope_cols=rope_cols, scale=scale,
                             sub_tiles=sub_tiles)
    return pl.pallas_call(
        kern,
        grid_spec=pltpu.PrefetchScalarGridSpec(
            num_scalar_prefetch=3, grid=(n_col, n_tiles),
            in_specs=in_specs, out_specs=o_spec, scratch_shapes=scratch),
        out_shape=jax.ShapeDtypeStruct((rows, n_out), out_dtype),
        compiler_params=_params(("arbitrary", "arbitrary"), vmem_mib),
        name=name,
    )(*schedule, *args)


GLA_ROW_BLOCK = 512


def _gate_kernel(x_ref, wr_ref, wgk_ref, bgk_ref, g_ref):
    r = jnp.dot(x_ref[...], wr_ref[...].astype(BF16), preferred_element_type=F32)
    r_hi = r.astype(BF16)
    r_lo = (r - r_hi.astype(F32)).astype(BF16)
    w = wgk_ref[...]
    w_hi = w.astype(BF16)
    w_lo = (w - w_hi.astype(F32)).astype(BF16)
    z = (jnp.dot(r_hi, w_hi, preferred_element_type=F32) + jnp.dot(r_hi, w_lo, preferred_element_type=F32)
         + jnp.dot(r_lo, w_hi, preferred_element_type=F32) + bgk_ref[...])
    g_ref[...] = (jnp.minimum(z, 0.0) - jnp.log(1.0 + jnp.exp(-jnp.abs(z)))) / GLA_GATE_NORM


def gate_rows(xn, w_r_pad, wgk_pad, bgk):
    rows, d = xn.shape
    width = wgk_pad.shape[1]
    return pl.pallas_call(
        _gate_kernel,
        grid=(rows // ROW_TILE,),
        in_specs=[pl.BlockSpec((ROW_TILE, d), lambda i: (i, 0)),
                  pl.BlockSpec((d, LANES), lambda i: (0, 0)),
                  pl.BlockSpec((LANES, width), lambda i: (0, 0)),
                  pl.BlockSpec((1, width), lambda i: (0, 0))],
        out_specs=pl.BlockSpec((ROW_TILE, width), lambda i: (i, 0)),
        out_shape=jax.ShapeDtypeStruct((rows, width), F32),
        compiler_params=_params(("parallel",), 32),
        name="gate_rows",
    )(xn, w_r_pad, wgk_pad, bgk)


def _gla_kernel(q_ref, k_ref, v_ref, go_ref, g_ref, gn_ref, s0_ref,
                o_ref, sfin_ref, st_ref, *, n_chunks, n_valid):
    rb = pl.program_id(1)

    @pl.when(rb == 0)
    def _():
        for h in range(GLA_HEADS):
            st_ref[h] = s0_ref[h].T

    ri = lax.broadcasted_iota(jnp.int32, (CHUNK, CHUNK), 0)
    ci = lax.broadcasted_iota(jnp.int32, (CHUNK, CHUNK), 1)
    causal = ci <= ri
    tril = jnp.where(causal, 1.0, 0.0).astype(BF16)
    row_in_chunk = lax.broadcasted_iota(jnp.int32, (CHUNK, 1), 0)
    gn = gn_ref[...]

    def body(c, carry):
        rows = pl.ds(pl.multiple_of(c * CHUNK, CHUNK), CHUNK)
        row_valid = (rb * n_chunks + c) * CHUNK + row_in_chunk < n_valid
        g = jnp.where(row_valid, g_ref[rows, :], 0.0)
        g_hi = g.astype(BF16)
        rest = g - g_hi.astype(F32)
        g_mid = rest.astype(BF16)
        g_lo = (rest - g_mid.astype(F32)).astype(BF16)
        b = (jnp.dot(tril, g_hi, preferred_element_type=F32) + jnp.dot(tril, g_mid, preferred_element_type=F32)
             + jnp.dot(tril, g_lo, preferred_element_type=F32))
        b_last = b[CHUNK - 1:CHUNK, :]
        q = q_ref[rows, :].astype(F32) * (GLA_DK ** -0.5)
        k = jnp.where(row_valid, k_ref[rows, :].astype(F32), 0.0)
        q_dec_all = (q * jnp.exp(b)).astype(BF16)
        k_dec_all = (k * jnp.exp(-b)).astype(BF16)
        k_last_all = (k * jnp.exp(b_last - b)).astype(BF16)
        decay = jnp.exp(b_last)
        for h in range(GLA_HEADS):
            ks = slice(h * GLA_DK, (h + 1) * GLA_DK)
            vs = slice(h * GLA_DV, (h + 1) * GLA_DV)
            q_dec = q_dec_all[:, ks]
            v = v_ref[rows, vs]
            att = lax.dot_general(q_dec, k_dec_all[:, ks], (((1,), (1,)), ((), ())),
                                  preferred_element_type=F32)
            att = jnp.where(causal, att, 0.0).astype(BF16)
            st = st_ref[h]
            o = jnp.dot(att, v, preferred_element_type=F32)
            o = o + lax.dot_general(q_dec, st.astype(BF16), (((1,), (1,)), ((), ())),
                                    preferred_element_type=F32)
            st_ref[h] = st * decay[:, ks] + lax.dot_general(
                v, k_last_all[:, ks], (((0,), (0,)), ((), ())), preferred_element_type=F32)
            on = o * lax.rsqrt(jnp.mean(o * o, axis=-1, keepdims=True) + EPS) * gn
            go = go_ref[rows, vs].astype(F32)
            o_ref[rows, vs] = (on * (go * jax.nn.sigmoid(go))).astype(o_ref.dtype)
        return carry

    lax.fori_loop(0, n_chunks, body, 0, unroll=4 if n_chunks % 4 == 0 else 1)

    @pl.when(rb == pl.num_programs(1) - 1)
    def _():
        for h in range(GLA_HEADS):
            sfin_ref[h] = st_ref[h].T


def gla_scan(proj, gates, gnorm, s0, *, n_seq, seq_len, row0, n_valid, s0_per_seq):
    blk = min(seq_len, GLA_ROW_BLOCK)
    n_rb = seq_len // blk
    hk = GLA_HEADS * GLA_DK
    hv = GLA_HEADS * GLA_DV
    rb0 = row0 // blk
    row = lambda b, r: rb0 + b * n_rb + r
    s0_map = (lambda b, r: (b, 0, 0, 0)) if s0_per_seq else (lambda b, r: (0, 0, 0, 0))
    kern = functools.partial(_gla_kernel, n_chunks=blk // CHUNK, n_valid=n_valid)
    return pl.pallas_call(
        kern,
        grid=(n_seq, n_rb),
        in_specs=[
            pl.BlockSpec((blk, hk), lambda b, r: (row(b, r), 0)),
            pl.BlockSpec((blk, hk), lambda b, r: (row(b, r), 1)),
            pl.BlockSpec((blk, hv), lambda b, r: (row(b, r), 2 * hk // hv)),
            pl.BlockSpec((blk, hv), lambda b, r: (row(b, r), 2 * hk // hv + 1)),
            pl.BlockSpec((blk, hk), lambda b, r: (row(b, r), 0)),
            pl.BlockSpec((1, GLA_DV), lambda b, r: (0, 0)),
            pl.BlockSpec((None, GLA_HEADS, GLA_DK, GLA_DV), s0_map),
        ],
        out_specs=[
            pl.BlockSpec((blk, hv), lambda b, r: (b * n_rb + r, 0)),
            pl.BlockSpec((None, GLA_HEADS, GLA_DK, GLA_DV), lambda b, r: (b, 0, 0, 0)),
        ],
        out_shape=[jax.ShapeDtypeStruct((n_seq * seq_len, hv), BF16),
                   jax.ShapeDtypeStruct((n_seq, GLA_HEADS, GLA_DK, GLA_DV), F32)],
        scratch_shapes=[pltpu.VMEM((GLA_HEADS, GLA_DV, GLA_DK), F32)],
        compiler_params=_params(("parallel", "arbitrary"), 48),
        name="gla_scan",
    )(proj, proj, proj, proj, gates, gnorm, s0)


WIN_KEYS = (WINDOW_CHUNKS + 1) * CHUNK
ATTN_KEYS = 2 * LANES
ATTN_ROW_BLOCK = 512
PAIRS = SWA_GROUP // 2


def _lane_halves(x2, head_in_pair):
    lane = lax.broadcasted_iota(jnp.int32, x2.shape, 1)
    low = lane < HEAD_DIM
    swapped = pltpu.roll(x2, HEAD_DIM, 1)
    if head_in_pair == 0:
        lo, hi = jnp.where(low, x2, 0.0), jnp.where(low, 0.0, swapped)
    else:
        lo, hi = jnp.where(low, swapped, 0.0), jnp.where(low, 0.0, x2)
    return lo.astype(BF16), hi.astype(BF16)


def _attn_kernel(q_ref, k_ref, v_ref, km_ref, vm_ref, sink_ref, o_ref, kb_ref, vb_ref, kmb_ref, vmb_ref,
                 *, n_chunks, chunk_offset):
    step = pl.program_id(1)

    @pl.when(step == 0)
    def _():
        for h in range(SWA_KV_HEADS):
            cols = slice((h // 2) * LANES, (h // 2 + 1) * LANES)
            for src, dst in ((k_ref, kb_ref), (v_ref, vb_ref), (km_ref, kmb_ref), (vm_ref, vmb_ref)):
                lo, hi = _lane_halves(src[:, cols], h % 2)
                dst[h, 0] = lo
                dst[h, 1] = hi

    j = lax.broadcasted_iota(jnp.int32, (1, ATTN_KEYS), 1)
    rel_chunk = (j >= N_META + CHUNK).astype(jnp.int32) + (j >= N_META + 2 * CHUNK).astype(jnp.int32)
    in_window = (j >= N_META) & (j < N_META + WIN_KEYS)
    is_sink_slot = j == N_META + WIN_KEYS
    zpad = jnp.zeros((ATTN_KEYS - N_META - WIN_KEYS, LANES), BF16)
    key_row = lax.broadcasted_iota(jnp.int32, (2 * ATTN_KEYS, LANES), 0)
    key_lane = lax.broadcasted_iota(jnp.int32, (2 * ATTN_KEYS, LANES), 1)
    ones_cols = ((key_row < ATTN_KEYS) == (key_lane < HEAD_DIM)).astype(BF16)
    nt = (((1,), (1,)), ((), ()))

    def body(ci, carry):
        c = step * n_chunks + ci + chunk_offset
        wc = jnp.maximum(c - WINDOW_CHUNKS, 0)
        win = pl.ds(pl.multiple_of(wc * CHUNK, CHUNK), WIN_KEYS)
        rows = pl.ds(pl.multiple_of(ci * CHUNK, CHUNK), CHUNK)
        valid = (j < N_META) | (in_window & (wc + rel_chunk <= c))
        for h in range(SWA_KV_HEADS):
            kb = jnp.concatenate([kmb_ref[h, 0], kb_ref[h, 0, win, :], zpad,
                                  kmb_ref[h, 1], kb_ref[h, 1, win, :], zpad], axis=0)
            vb = jnp.concatenate([vmb_ref[h, 0], vb_ref[h, 0, win, :], zpad,
                                  vmb_ref[h, 1], vb_ref[h, 1, win, :], zpad], axis=0)
            col = lambda p: slice((h * PAIRS + p) * LANES, (h * PAIRS + p + 1) * LANES)
            qs = jnp.concatenate([q_ref[rows, col(p)] for p in range(PAIRS)], axis=0)
            s = lax.dot_general(qs, kb, nt, preferred_element_type=F32)
            probs = []
            for half in range(2):
                sink = sink_ref[h, half]
                sh = jnp.where(valid, s[:, half * ATTN_KEYS:(half + 1) * ATTN_KEYS], NEG_INF)
                sh = jnp.where(is_sink_slot, jnp.concatenate([sink] * (ATTN_KEYS // LANES), axis=1), sh)
                m = jnp.max(sh, axis=-1, keepdims=True)
                probs.append(jnp.exp(sh - m).astype(BF16))
            ov = jnp.dot(jnp.concatenate(probs, axis=1), jnp.concatenate([vb, ones_cols], axis=1),
                         preferred_element_type=F32)
            o = ov[:, :LANES] / ov[:, LANES:]
            for p in range(PAIRS):
                o_ref[rows, col(p)] = o[p * CHUNK:(p + 1) * CHUNK].astype(o_ref.dtype)
        return carry

    lax.fori_loop(0, n_chunks, body, 0, unroll=4 if n_chunks % 4 == 0 else 1)


def window_attention(q, k_arr, v_arr, km_arr, vm_arr, sink_tab, *, n_seq, seq_len, key_len, row0,
                     k_map, v_map, km_map, vm_map, chunk_offset):
    blk = min(seq_len, ATTN_ROW_BLOCK)
    steps = seq_len // blk
    q_spec = pl.BlockSpec((blk, q.shape[1]), lambda b, s: (row0 // blk + b * steps + s, 0))
    o_spec = pl.BlockSpec((blk, q.shape[1]), lambda b, s: (b * steps + s, 0))
    width = SWA_KV_HEADS * HEAD_DIM
    kern = functools.partial(_attn_kernel, n_chunks=blk // CHUNK, chunk_offset=chunk_offset)
    return pl.pallas_call(
        kern,
        grid=(n_seq, steps),
        in_specs=[q_spec,
                  pl.BlockSpec((key_len, width), k_map),
                  pl.BlockSpec((key_len, width), v_map),
                  pl.BlockSpec((N_META, width), km_map),
                  pl.BlockSpec((N_META, width), vm_map),
                  pl.BlockSpec((SWA_KV_HEADS, 2, PAIRS * CHUNK, LANES), lambda b, s: (0, 0, 0, 0))],
        out_specs=o_spec,
        out_shape=jax.ShapeDtypeStruct((n_seq * seq_len, q.shape[1]), q.dtype),
        scratch_shapes=[pltpu.VMEM((SWA_KV_HEADS, 2, key_len, LANES), BF16),
                        pltpu.VMEM((SWA_KV_HEADS, 2, key_len, LANES), BF16),
                        pltpu.VMEM((SWA_KV_HEADS, 2, N_META, LANES), BF16),
                        pltpu.VMEM((SWA_KV_HEADS, 2, N_META, LANES), BF16)],
        compiler_params=_params(("parallel", "arbitrary"), 48),
        name="window_attention",
    )(q, k_arr, v_arr, km_arr, vm_arr, sink_tab)


def _route_tile(xn, wr_ref, xp_ref, rt_ref, cnt_ref, carry_ref):
    bits = pltpu.bitcast(xn.astype(BF16).astype(F32), jnp.uint32)
    half = D_MODEL // 2
    xp_ref[...] = (bits[:, :half] >> 16) | (bits[:, half:] & jnp.uint32(0xFFFF0000))

    tm = xn.shape[0]
    hi = xn.astype(BF16)
    lo = (xn - hi.astype(F32)).astype(BF16)
    wide = jnp.dot(hi, wr_ref[...], preferred_element_type=F32)
    logits = (wide[:, :LANES] + wide[:, LANES:]
              + jnp.dot(lo, wr_ref[:, :LANES], preferred_element_type=F32))
    lane = lax.broadcasted_iota(jnp.int32, (tm, LANES), 1).astype(F32)
    neg = jnp.float32(-jnp.inf)
    lg = jnp.where(lane < N_EXPERTS, logits, neg)
    m1 = jnp.max(lg, axis=-1, keepdims=True)
    i1 = jnp.min(jnp.where(lg == m1, lane, float(LANES)), axis=-1, keepdims=True)
    lg2 = jnp.where(lane == i1, neg, lg)
    m2 = jnp.max(lg2, axis=-1, keepdims=True)
    i2 = jnp.min(jnp.where(lg2 == m2, lane, float(LANES)), axis=-1, keepdims=True)
    e21 = jnp.exp(m2 - m1)
    g1 = 1.0 / (1.0 + e21)
    g2 = e21 / (1.0 + e21)
    oh1 = (lane == i1).astype(F32)
    oh2 = (lane == i2).astype(F32)
    oh = oh1 + oh2
    ri = lax.broadcasted_iota(jnp.int32, (tm, tm), 0)
    ci = lax.broadcasted_iota(jnp.int32, (tm, tm), 1)
    before = (ci < ri).astype(BF16)
    prefix = jnp.dot(before, oh.astype(BF16), preferred_element_type=F32) + carry_ref[0:1, :]
    rank1 = jnp.sum(prefix * oh1, axis=-1, keepdims=True)
    rank2 = jnp.sum(prefix * oh2, axis=-1, keepdims=True)
    total = carry_ref[0:1, :] + jnp.sum(oh, axis=0, keepdims=True)
    carry_ref[...] = jnp.broadcast_to(total, carry_ref.shape)
    cnt_ref[...] = jnp.broadcast_to(total, cnt_ref.shape)
    rt = jnp.where(lane == 0, i1, 0.0)
    rt = jnp.where(lane == 1, i2, rt)
    rt = jnp.where(lane == 2, rank1, rt)
    rt = jnp.where(lane == 3, rank2, rt)
    rt = jnp.where(lane == 4, g1, rt)
    rt = jnp.where(lane == 5, g2, rt)
    rt_ref[...] = rt


def _rowmm_kernel(*refs, router, a_bounds, res_bounds):
    na, nr = len(a_bounds), len(res_bounds)
    a_refs, w_ref, res_refs, g_ref = refs[:na], refs[na], refs[na + 1:na + 1 + nr], refs[na + 1 + nr]
    rest = refs[na + 2 + nr:]
    t = pl.program_id(0)
    if router:
        wr_ref, h_ref, xp_ref, rt_ref, cnt_ref, wb_ref, carry_ref, wcat_ref = rest
    else:
        h_ref, xn_ref, wb_ref = rest

    @pl.when(t == 0)
    def _():
        wb_ref[...] = w_ref[...].astype(BF16)
        if router:
            carry_ref[...] = jnp.zeros_like(carry_ref)
            wr = wr_ref[...]
            wr_hi = wr.astype(BF16)
            wcat_ref[:, :LANES] = wr_hi
            wcat_ref[:, LANES:] = (wr - wr_hi.astype(F32)).astype(BF16)

    h = _read_part(res_refs, res_bounds, t) + jnp.dot(_read_part(a_refs, a_bounds, t), wb_ref[...],
                                                     preferred_element_type=F32)
    h_ref[...] = h
    xn = h * lax.rsqrt(jnp.mean(h * h, axis=-1, keepdims=True) + EPS) * g_ref[...]
    if router:
        _route_tile(xn, wcat_ref, xp_ref, rt_ref, cnt_ref, carry_ref)
    else:
        xn_ref[...] = xn.astype(xn_ref.dtype)


def row_matmul(a_parts, w, res_parts, gain, *, rows, tm, w_router_pad=None, name="row_matmul"):
    k, d = w.shape[1], w.shape[2]
    router = w_router_pad is not None
    row_spec = lambda width: pl.BlockSpec((tm, width), lambda t: (t, 0))
    a_specs, a_bounds = _part_specs(a_parts, tm)
    res_specs, res_bounds = _part_specs(res_parts, tm)
    assert min(a_bounds[-1], res_bounds[-1]) * tm >= rows
    in_specs = (a_specs + [pl.BlockSpec((None, k, d), lambda t: (0, 0, 0), pipeline_mode=pl.Buffered(1))]
                + res_specs + [pl.BlockSpec((1, d), lambda t: (0, 0))])
    args = [*a_parts, w, *res_parts, gain]
    out_specs = [row_spec(d)]
    out_shape = [jax.ShapeDtypeStruct((rows, d), F32)]
    scratch = [pltpu.VMEM((k, d), BF16)]
    if router:
        in_specs.append(pl.BlockSpec((d, LANES), lambda t: (0, 0)))
        args.append(w_router_pad)
        out_specs += [row_spec(d // 2), row_spec(LANES), pl.BlockSpec((8, LANES), lambda t: (0, 0))]
        out_shape += [jax.ShapeDtypeStruct((rows, d // 2), jnp.uint32),
                      jax.ShapeDtypeStruct((rows, LANES), F32),
                      jax.ShapeDtypeStruct((8, LANES), F32)]
        scratch += [pltpu.VMEM((8, LANES), F32), pltpu.VMEM((d, 2 * LANES), BF16)]
    else:
        out_specs.append(row_spec(d))
        out_shape.append(jax.ShapeDtypeStruct((rows, d), BF16))
    return pl.pallas_call(
        functools.partial(_rowmm_kernel, router=router, a_bounds=a_bounds, res_bounds=res_bounds),
        grid=(rows // tm,),
        in_specs=in_specs, out_specs=out_specs, out_shape=out_shape, scratch_shapes=scratch,
        compiler_params=_params(("arbitrary",), 56),
        name=name,
    )(*args)


def _scatter_kernel(pos_ref, xp_ref, xs_in_ref, xs_ref, sem):
    del xs_in_ref
    tm = xp_ref.shape[0]

    def row_copy(r, slot):
        return pltpu.make_async_copy(xp_ref.at[pl.ds(r, 1), :], xs_ref.at[pl.ds(slot, 1), :], sem)

    def issue(r, carry):
        row_copy(r, pos_ref[2 * r]).start()
        row_copy(r, pos_ref[2 * r + 1]).start()
        return carry

    lax.fori_loop(0, tm, issue, 0, unroll=8)
    for _ in range(2):
        pltpu.make_async_copy(xp_ref, xs_ref.at[pl.ds(0, tm), :], sem).wait()


def scatter_rows(pos_flat, xp, xs_init):
    rows = xp.shape[0]
    return pl.pallas_call(
        _scatter_kernel,
        grid=(rows // ROW_TILE,),
        in_specs=[pl.BlockSpec((2 * ROW_TILE,), lambda i: (i,), memory_space=pltpu.SMEM),
                  pl.BlockSpec((ROW_TILE, xp.shape[1]), lambda i: (i, 0)),
                  pl.BlockSpec(memory_space=pl.ANY)],
        out_specs=pl.BlockSpec(memory_space=pl.ANY),
        out_shape=jax.ShapeDtypeStruct(xs_init.shape, xs_init.dtype),
        scratch_shapes=[pltpu.SemaphoreType.DMA(())],
        input_output_aliases={2: 0},
        compiler_params=_params(("arbitrary",), 32),
        name="scatter_rows",
    )(pos_flat, xp, xs_init)


def _combine_kernel(pos_ref, rt_ref, h_ref, g_ref, ys_ref, yp_ref, ysm_ref, buf_ref, sem, *, n_prompt_tiles):
    i = pl.program_id(0)
    tm = h_ref.shape[0]

    def row_copy(r, k, slot):
        return pltpu.make_async_copy(ys_ref.at[pl.ds(slot, 1), :], buf_ref.at[k, pl.ds(r, 1), :], sem)

    def issue(r, carry):
        row_copy(r, 0, pos_ref[2 * r]).start()
        row_copy(r, 1, pos_ref[2 * r + 1]).start()
        return carry

    lax.fori_loop(0, tm, issue, 0, unroll=8)
    for k in range(2):
        pltpu.make_async_copy(ys_ref.at[pl.ds(0, tm), :], buf_ref.at[k], sem).wait()
    rt = rt_ref[...]
    y = rt[:, 4:5] * buf_ref[0] + rt[:, 5:6] * buf_ref[1]
    x = h_ref[...] + y
    out = x * lax.rsqrt(jnp.mean(x * x, axis=-1, keepdims=True) + EPS) * g_ref[...]

    @pl.when(i < n_prompt_tiles)
    def _():
        yp_ref[...] = out

    @pl.when(i >= n_prompt_tiles)
    def _():
        ysm_ref[...] = out


def combine_rows(pos_flat, rt, h, gain, ys, *, n_prompt_rows):
    rows = h.shape[0]
    npt = n_prompt_rows // ROW_TILE
    kern = functools.partial(_combine_kernel, n_prompt_tiles=npt)
    return pl.pallas_call(
        kern,
        grid=(rows // ROW_TILE,),
        in_specs=[pl.BlockSpec((2 * ROW_TILE,), lambda i: (i,), memory_space=pltpu.SMEM),
                  pl.BlockSpec((ROW_TILE, LANES), lambda i: (i, 0)),
                  pl.BlockSpec((ROW_TILE, D_MODEL), lambda i: (i, 0)),
                  pl.BlockSpec((1, D_MODEL), lambda i: (0, 0)),
                  pl.BlockSpec(memory_space=pl.ANY)],
        out_specs=[pl.BlockSpec((ROW_TILE, D_MODEL), lambda i: (jnp.minimum(i, npt - 1), 0)),
                   pl.BlockSpec((ROW_TILE, D_MODEL), lambda i: (jnp.maximum(i - npt, 0), 0))],
        out_shape=[jax.ShapeDtypeStruct((n_prompt_rows, D_MODEL), F32),
                   jax.ShapeDtypeStruct((rows - n_prompt_rows, D_MODEL), F32)],
        scratch_shapes=[pltpu.VMEM((2, ROW_TILE, D_MODEL), F32), pltpu.SemaphoreType.DMA(())],
        compiler_params=_params(("arbitrary",), 48),
        name="combine_rows",
    )(pos_flat, rt, h, gain, ys)


def _rope_tables(pos):
    half = HEAD_DIM // 2
    inv_freq = ROPE_THETA ** (-jnp.arange(half, dtype=F32) / half)
    ang = jnp.asarray(pos, jnp.int32).astype(F32)[:, None] * inv_freq[None, :]
    cos = jnp.cos(ang)
    sin = jnp.sin(ang)
    reps = LANES // HEAD_DIM
    return (jnp.tile(jnp.concatenate([cos, cos], axis=1), (1, reps)),
            jnp.tile(jnp.concatenate([-sin, sin], axis=1), (1, reps)))


def kernel(x_prompt, x_sample, state_gla, cache_k_meta, cache_v_meta, cache_k_win, cache_v_win, meta_tokens,
           norm_mix, norm_ffn, norm_kv, norm_final, gla_w_in, gla_w_gk, gla_b_gk, gla_norm, gla_w_out, kv_w,
           attn_w_q, attn_sinks, attn_w_out, ffn_w_gate_up, ffn_w_down, moe_w_router, moe_w_gate_up,
           moe_w_down):
    bsz, seq, d = x_prompt.shape
    dbsz, t_new, _ = x_sample.shape
    past_len = 2048
    n_p = bsz * seq
    n_s = dbsz * t_new
    r1 = n_p + n_s
    assert r1 % ROW_TILE == 0 and seq % CHUNK == 0 and t_new == CHUNK
    r0 = -(-(r1 + CHUNK) // ROW_TILE) * ROW_TILE
    meta_row = r1
    hk = GLA_HEADS * GLA_DK
    hv = GLA_HEADS * GLA_DV
    ffn_dense = ffn_w_down.shape[1]
    ffn_expert = moe_w_down.shape[2]
    kv_width = SWA_KV_HEADS * HEAD_DIM

    x_parts = [x_prompt.reshape(n_p, d),
               jnp.concatenate([x_sample.reshape(n_s, d), meta_tokens.astype(F32),
                                jnp.zeros((r0 - r1 - N_META, d), F32)], axis=0)]
    tail_rows = r0 - r1

    tm0 = r0 // 16
    tm1 = r1 // 16
    (xn0,) = rms_rows(x_parts, norm_mix[0:1])
    proj = matmul(xn0, gla_w_in, n_out=2 * hk + 2 * hv, out_dtype=BF16, tm=tm0, tn=2 * COL_TILE,
                  name="gla_in_proj")
    w_r_pad = jnp.pad(gla_w_in[0, :, 2 * hk + 2 * hv:], ((0, 0), (0, LANES - GLA_RANK)))
    wgk_pad = jnp.pad(gla_w_gk[0], ((0, LANES - GLA_RANK), (0, 0)))
    gnorm = gla_norm[0][None, :]
    gla = functools.partial(gla_scan, proj, gate_rows(xn0, w_r_pad, wgk_pad, gla_b_gk[0][None, :]), gnorm)
    og_meta, s_meta = gla(jnp.zeros((1, GLA_HEADS, GLA_DK, GLA_DV), F32), n_seq=1, seq_len=tail_rows,
                          row0=meta_row, n_valid=N_META, s0_per_seq=False)
    og_prompt, s_prompt = gla(s_meta, n_seq=bsz, seq_len=seq, row0=0, n_valid=seq, s0_per_seq=False)
    og_sample, s_sample = gla(state_gla[0].astype(F32), n_seq=dbsz, seq_len=t_new, row0=n_p,
                              n_valid=t_new, s0_per_seq=True)
    h1, hn1 = row_matmul([og_prompt, jnp.concatenate([og_sample, og_meta], axis=0)], gla_w_out, x_parts,
                         norm_ffn[0:1], rows=r0, tm=r0 // 68, name="gla_out_proj")

    act = matmul(hn1, ffn_w_gate_up, n_out=ffn_dense, out_dtype=BF16, mode="swiglu", up_col0=ffn_dense,
                 tm=2 * tm0, vmem_mib=56, name="ffn_gate_up")
    h2 = matmul(act, ffn_w_down, n_out=d, out_dtype=F32, mode="residual", residual=h1, tm=tm0 // 2,
                vmem_mib=56, name="ffn_down")

    pos = np.concatenate([np.tile(N_META + np.arange(seq), bsz),
                          np.tile(N_META + past_len + np.arange(t_new), dbsz),
                          np.arange(N_META), np.zeros(r0 - r1 - N_META, np.int64)])
    rope_tabs = _rope_tables(pos)
    xkv, xq = rms_rows([h2], jnp.stack([norm_kv, norm_mix[1]]))
    kvf = matmul(xkv, kv_w[None], n_out=2 * kv_width, out_dtype=F32, mode="rope", rope_tabs=rope_tabs,
                 rope_cols=kv_width, tm=tm0, name="shared_kv")
    q = matmul(xq, attn_w_q, n_out=d, out_dtype=BF16, mode="rope", rope_tabs=rope_tabs, rope_cols=d,
               scale=HEAD_DIM ** -0.5, rows=r1, tm=tm1, tn=2 * COL_TILE, name="attn_q")

    sink_tab = jnp.broadcast_to(
        jnp.repeat(attn_sinks[0].astype(F32).reshape(SWA_KV_HEADS, PAIRS, 2).transpose(0, 2, 1), CHUNK, axis=2)
        [..., None], (SWA_KV_HEADS, 2, PAIRS * CHUNK, LANES))
    meta_blk = meta_row // N_META
    o_prompt = window_attention(
        q, kvf, kvf, kvf, kvf, sink_tab, n_seq=bsz, seq_len=seq, key_len=seq, row0=0,
        k_map=lambda b, s: (b, 0), v_map=lambda b, s: (b, 1),
        km_map=lambda b, s: (meta_blk, 0), vm_map=lambda b, s: (meta_blk, 1), chunk_offset=0)
    k_new = kvf[n_p:r1, :kv_width]
    v_new = kvf[n_p:r1, kv_width:]
    win = cache_k_win.shape[1]
    ks = jnp.concatenate([cache_k_win.reshape(dbsz, win, kv_width).astype(F32),
                          k_new.reshape(dbsz, t_new, kv_width)], axis=1).reshape(dbsz * (win + t_new), kv_width)
    vs = jnp.concatenate([cache_v_win.reshape(dbsz, win, kv_width).astype(F32),
                          v_new.reshape(dbsz, t_new, kv_width)], axis=1).reshape(dbsz * (win + t_new), kv_width)
    assert win + t_new == (WINDOW_CHUNKS + 1) * CHUNK
    o_sample = window_attention(
        q, ks, vs, cache_k_meta.reshape(dbsz * N_META, kv_width).astype(F32),
        cache_v_meta.reshape(dbsz * N_META, kv_width).astype(F32), sink_tab,
        n_seq=dbsz, seq_len=t_new, key_len=win + t_new, row0=n_p,
        k_map=lambda b, s: (b, 0), v_map=lambda b, s: (b, 0),
        km_map=lambda b, s: (b, 0), vm_map=lambda b, s: (b, 0), chunk_offset=WINDOW_CHUNKS)
    w_router_pad = jnp.pad(moe_w_router[0], ((0, 0), (0, LANES - N_EXPERTS)))
    h3, xp, rt, cnt = row_matmul([o_prompt, o_sample], attn_w_out, [h2], norm_ffn[1:2], rows=r1, tm=r1 // 66,
                                 w_router_pad=w_router_pad, name="attn_out_route")
    unit, per_tile = MOE_UNIT, MOE_TILE // MOE_UNIT
    n_tiles_max = 2 * r1 // MOE_TILE + N_EXPERTS
    counts = cnt[0, :N_EXPERTS].astype(jnp.int32)
    units_per = (counts + unit - 1) // unit
    tiles_per = (units_per + per_tile - 1) // per_tile
    tile_end = jnp.cumsum(tiles_per)
    tile_start = tile_end - tiles_per
    experts = rt[:, 0:2].astype(jnp.int32)
    ranks = rt[:, 2:4].astype(jnp.int32)
    pos_flat = (tile_start[experts] * MOE_TILE + ranks).reshape(-1)
    n_used = tile_end[-1:].astype(jnp.int32)
    tile_id = jnp.arange(n_tiles_max, dtype=jnp.int32)
    tile_expert = jnp.minimum(jnp.sum(tile_id[:, None] >= tile_end[None, :], axis=1), N_EXPERTS - 1).astype(jnp.int32)
    tile_units = jnp.clip(units_per[tile_expert] - per_tile * (tile_id - tile_start[tile_expert]), 0, per_tile)
    tile_units = jnp.where(tile_id < n_used[0], tile_units, 0).astype(jnp.int32)
    xs = scatter_rows(pos_flat, xp, jnp.zeros((n_tiles_max * MOE_TILE, d // 2), jnp.uint32))
    act_e = matmul(xs, moe_w_gate_up.reshape(N_EXPERTS, d, 2 * ffn_expert), n_out=ffn_expert, out_dtype=BF16,
                   mode="swiglu", up_col0=ffn_expert, schedule=(tile_expert, tile_units, n_used),
                   sub_tiles=per_tile, tm=MOE_TILE, a_packed=True, vmem_mib=56, name="moe_gate_up")
    split = MOE_TILE // MOE_DOWN_TILE
    down_id = jnp.arange(n_tiles_max * split, dtype=jnp.int32)
    down_valid = ((down_id % split) * MOE_DOWN_TILE // unit < jnp.repeat(tile_units, split)).astype(jnp.int32)
    ys = matmul(act_e, moe_w_down.reshape(N_EXPERTS, ffn_expert, d), n_out=d, out_dtype=F32,
                schedule=(jnp.repeat(tile_expert, split), down_valid, split * n_used), tm=MOE_DOWN_TILE,
                vmem_mib=56, name="moe_down")
    y_prompt, y_sample = combine_rows(pos_flat, rt, h3, norm_final[None, :], ys, n_prompt_rows=n_p)

    kv_meta = kvf[meta_row:meta_row + N_META]
    k_meta_p = jnp.broadcast_to(kv_meta[None, :, :kv_width], (bsz, N_META, kv_width))
    v_meta_p = jnp.broadcast_to(kv_meta[None, :, kv_width:], (bsz, N_META, kv_width))
    win_p = min(WINDOW_CHUNKS * CHUNK, seq)
    kv_win = kvf[:n_p].reshape(bsz, seq, 2 * kv_width)[:, seq - win_p:]
    shape4 = lambda a: a.reshape(a.shape[0], a.shape[1], SWA_KV_HEADS, HEAD_DIM)
    return (y_prompt.reshape(bsz, seq, d), y_sample.reshape(dbsz, t_new, d),
            s_prompt[None].astype(state_gla.dtype), s_sample[None].astype(state_gla.dtype),
            shape4(k_meta_p), shape4(v_meta_p),
            shape4(kv_win[:, :, :kv_width]), shape4(kv_win[:, :, kv_width:]),
            shape4(k_new.reshape(dbsz, t_new, kv_width)), shape4(v_new.reshape(dbsz, t_new, kv_width)))
```

```python
import functools

import numpy as np
import jax
import jax.numpy as jnp
from jax import lax
from jax.experimental import pallas as pl
from jax.experimental.pallas import tpu as pltpu

F32 = jnp.float32
BF16 = jnp.bfloat16
HIGHEST = lax.Precision.HIGHEST

D_MODEL = 2048
CHUNK = 64
N_META = 16
GLA_HEADS = 4
GLA_DK = 256
GLA_DV = 512
GLA_RANK = 16
GLA_GATE_NORM = 16.0
HEAD_DIM = 64
SWA_HEADS = 32
SWA_KV_HEADS = 4
SWA_GROUP = 8
WINDOW_CHUNKS = 2
ROPE_THETA = 10000.0
N_EXPERTS = 8
EPS = 1e-5
NEG_INF = -1e30

LANES = 128
ROW_TILE = 512
COL_TILE = 512
MOE_TILE = 1024
MOE_UNIT = 512
MOE_DOWN_TILE = 512
MIB = 2 ** 20


def _params(semantics, vmem_mib):
    return pltpu.CompilerParams(dimension_semantics=semantics, vmem_limit_bytes=vmem_mib * MIB)


def _part_specs(parts, tm):
    specs, bounds, start = [], [], 0
    for p in parts:
        n = p.shape[0] // tm
        assert n * tm == p.shape[0]
        specs.append(pl.BlockSpec((tm, p.shape[1]), lambda t, s=start, n=n: (jnp.clip(t - s, 0, n - 1), 0)))
        start += n
        bounds.append(start)
    return specs, tuple(bounds)


def _read_part(refs, bounds, t):
    x = refs[0][...]
    for ref, lo in zip(refs[1:], bounds[:-1]):
        x = jnp.where(t >= lo, ref[...], x)
    return x


def _norm_kernel(*refs, bounds):
    n_src = len(bounds)
    g_ref = refs[n_src]
    x = _read_part(refs[:n_src], bounds, pl.program_id(0))
    y = x * lax.rsqrt(jnp.mean(x * x, axis=-1, keepdims=True) + EPS)
    for i, o_ref in enumerate(refs[n_src + 1:]):
        o_ref[...] = (y * g_ref[i:i + 1, :]).astype(o_ref.dtype)


def rms_rows(parts, gains):
    d = parts[0].shape[1]
    n = gains.shape[0]
    specs, bounds = _part_specs(parts, ROW_TILE)
    rows = bounds[-1] * ROW_TILE
    return pl.pallas_call(
        functools.partial(_norm_kernel, bounds=bounds),
        grid=(bounds[-1],),
        in_specs=specs + [pl.BlockSpec((n, d), lambda i: (0, 0))],
        out_specs=[pl.BlockSpec((ROW_TILE, d), lambda i: (i, 0)) for _ in range(n)],
        out_shape=[jax.ShapeDtypeStruct((rows, d), BF16) for _ in range(n)],
        compiler_params=_params(("parallel",), 40),
        name="rms_rows",
    )(*parts, gains)


def _swap_halves(x):
    lane = lax.broadcasted_iota(jnp.int32, x.shape, 1)
    first_half = (lane % HEAD_DIM) < (HEAD_DIM // 2)
    return jnp.where(first_half, pltpu.roll(x, LANES - HEAD_DIM // 2, 1), pltpu.roll(x, HEAD_DIM // 2, 1))


def _mm_kernel(te_ref, nv_ref, nu_ref, ph_ref, *refs, mode, a_packed, rope_cols, scale, sub_tiles):
    if mode == "swiglu":
        a_ref, w_ref, w2_ref, o_ref, wb_ref, wb2_ref = refs
    elif mode == "residual":
        a_ref, w_ref, res_ref, o_ref, wb_ref = refs
    elif mode == "rope":
        a_ref, w_ref, cos_ref, sin_ref, o_ref, wb_ref = refs
    else:
        a_ref, w_ref, o_ref, wb_ref = refs
    t = pl.program_id(1)
    tm = o_ref.shape[0]
    sub = tm // sub_tiles

    def compute(n_rows):
        rows = slice(0, n_rows)
        a = a_ref[rows, :]
        if a_packed:
            lo = pltpu.bitcast(a << 16, F32)
            hi = pltpu.bitcast(a & jnp.uint32(0xFFFF0000), F32)
            a = jnp.concatenate([lo, hi], axis=1)
        a = a.astype(BF16)
        acc = jnp.dot(a, wb_ref[...], preferred_element_type=F32)
        if mode == "swiglu":
            up = jnp.dot(a, wb2_ref[...], preferred_element_type=F32)
            o_ref[rows, :] = (acc * jax.nn.sigmoid(acc) * up).astype(o_ref.dtype)
        elif mode == "residual":
            o_ref[rows, :] = (res_ref[rows, :] + acc).astype(o_ref.dtype)
        elif mode == "rope":
            cos = cos_ref[rows, :]
            sin = sin_ref[rows, :]
            for c in range(acc.shape[1] // LANES):
                x = acc[:, c * LANES:(c + 1) * LANES]
                if c * LANES < rope_cols:
                    x = x * cos + _swap_halves(x) * sin
                o_ref[rows, c * LANES:(c + 1) * LANES] = (x * scale).astype(o_ref.dtype)
        else:
            o_ref[rows, :] = acc.astype(o_ref.dtype)
        if n_rows < tm:
            o_ref[n_rows:, :] = jnp.zeros((tm - n_rows, o_ref.shape[1]), o_ref.dtype)

    @pl.when(t < nu_ref[0])
    def _():
        @pl.when((t == 0) | (te_ref[t] != te_ref[jnp.maximum(t - 1, 0)]))
        def _():
            wb_ref[...] = w_ref[...].astype(BF16)
            if mode == "swiglu":
                wb2_ref[...] = w2_ref[...].astype(BF16)

    n_valid = jnp.where(t < nu_ref[0], nv_ref[t], 0)

    @pl.when(n_valid == 0)
    def _():
        o_ref[...] = jnp.zeros_like(o_ref)

    for s in range(1, sub_tiles + 1):
        pl.when(n_valid == s)(functools.partial(compute, s * sub))


def matmul(a, w, *, n_out, out_dtype, mode="plain", schedule=None, sub_tiles=1, residual=None,
           rope_tabs=None, rope_cols=0, scale=1.0, col0=0, up_col0=0, tm=ROW_TILE, tn=COL_TILE,
           a_packed=False, rows=None, vmem_mib=48, name="matmul"):
    rows = a.shape[0] if rows is None else rows
    k = w.shape[1]
    n_tiles = rows // tm
    n_col = n_out // tn
    w_mode = pl.Buffered(1) if schedule is None else None
    if schedule is None:
        schedule = (jnp.zeros((n_tiles,), jnp.int32), jnp.full((n_tiles,), sub_tiles, jnp.int32),
                    jnp.full((1,), n_tiles, jnp.int32), jnp.arange(n_tiles, dtype=jnp.int32))
    cb0 = col0 // tn
    ub0 = up_col0 // tn

    def row_of(t, nu):
        return jnp.maximum(jnp.minimum(t, nu[0] - 1), 0)

    a_spec = pl.BlockSpec((tm, a.shape[1]), lambda j, t, te, nv, nu, ph: (ph[row_of(t, nu)], 0))
    w_spec = pl.BlockSpec((None, k, tn), lambda j, t, te, nv, nu, ph: (te[row_of(t, nu)], 0, cb0 + j),
                          pipeline_mode=w_mode)
    o_spec = pl.BlockSpec((tm, tn), lambda j, t, te, nv, nu, ph: (ph[t], j))
    in_specs = [a_spec, w_spec]
    args = [a, w]
    scratch = [pltpu.VMEM((k, tn), BF16)]
    if mode == "swiglu":
        in_specs.append(pl.BlockSpec((None, k, tn), lambda j, t, te, nv, nu, ph: (te[row_of(t, nu)], 0, ub0 + j),
                                     pipeline_mode=w_mode))
        args.append(w)
        scratch.append(pltpu.VMEM((k, tn), BF16))
    elif mode == "residual":
        in_specs.append(o_spec)
        args.append(residual)
    elif mode == "rope":
        tab_spec = pl.BlockSpec((tm, LANES), lambda j, t, te, nv, nu, ph: (ph[row_of(t, nu)], 0))
        in_specs += [tab_spec, tab_spec]
        args += list(rope_tabs)
    kern = functools.partial(_mm_kernel, mode=mode, a_packed=a_packed, rope_cols=rope_cols, scale=scale,
                             sub_tiles=sub_tiles)
    return pl.pallas_call(
        kern,
        grid_spec=pltpu.PrefetchScalarGridSpec(
            num_scalar_prefetch=4, grid=(n_col, n_tiles),
            in_specs=in_specs, out_specs=o_spec, scratch_shapes=scratch),
        out_shape=jax.ShapeDtypeStruct((rows, n_out), out_dtype),
        compiler_params=_params(("arbitrary", "arbitrary"), vmem_mib),
        name=name,
    )(*schedule, *args)


GLA_ROW_BLOCK = 512


def _gate_kernel(x_ref, wr_ref, wgk_ref, bgk_ref, g_ref):
    r = jnp.dot(x_ref[...], wr_ref[...].astype(BF16), preferred_element_type=F32)
    r_hi = r.astype(BF16)
    r_lo = (r - r_hi.astype(F32)).astype(BF16)
    w = wgk_ref[...]
    w_hi = w.astype(BF16)
    w_lo = (w - w_hi.astype(F32)).astype(BF16)
    z = (jnp.dot(r_hi, w_hi, preferred_element_type=F32) + jnp.dot(r_hi, w_lo, preferred_element_type=F32)
         + jnp.dot(r_lo, w_hi, preferred_element_type=F32) + bgk_ref[...])
    g_ref[...] = (jnp.minimum(z, 0.0) - jnp.log(1.0 + jnp.exp(-jnp.abs(z)))) / GLA_GATE_NORM


def gate_rows(xn, w_r_pad, wgk_pad, bgk):
    rows, d = xn.shape
    width = wgk_pad.shape[1]
    return pl.pallas_call(
        _gate_kernel,
        grid=(rows // ROW_TILE,),
        in_specs=[pl.BlockSpec((ROW_TILE, d), lambda i: (i, 0)),
                  pl.BlockSpec((d, LANES), lambda i: (0, 0)),
                  pl.BlockSpec((LANES, width), lambda i: (0, 0)),
                  pl.BlockSpec((1, width), lambda i: (0, 0))],
        out_specs=pl.BlockSpec((ROW_TILE, width), lambda i: (i, 0)),
        out_shape=jax.ShapeDtypeStruct((rows, width), F32),
        compiler_params=_params(("parallel",), 32),
        name="gate_rows",
    )(xn, w_r_pad, wgk_pad, bgk)


def _gla_kernel(q_ref, k_ref, v_ref, go_ref, g_ref, gn_ref, s0_ref,
                o_ref, sfin_ref, st_ref, *, n_chunks, n_valid):
    rb = pl.program_id(1)

    @pl.when(rb == 0)
    def _():
        for h in range(GLA_HEADS):
            st_ref[h] = s0_ref[h].T

    ri = lax.broadcasted_iota(jnp.int32, (CHUNK, CHUNK), 0)
    ci = lax.broadcasted_iota(jnp.int32, (CHUNK, CHUNK), 1)
    causal = ci <= ri
    tril = jnp.where(causal, 1.0, 0.0).astype(BF16)
    row_in_chunk = lax.broadcasted_iota(jnp.int32, (CHUNK, 1), 0)
    gn = gn_ref[...]

    def body(c, carry):
        rows = pl.ds(pl.multiple_of(c * CHUNK, CHUNK), CHUNK)
        row_valid = (rb * n_chunks + c) * CHUNK + row_in_chunk < n_valid
        g = jnp.where(row_valid, g_ref[rows, :], 0.0)
        g_hi = g.astype(BF16)
        rest = g - g_hi.astype(F32)
        g_mid = rest.astype(BF16)
        g_lo = (rest - g_mid.astype(F32)).astype(BF16)
        b = (jnp.dot(tril, g_hi, preferred_element_type=F32) + jnp.dot(tril, g_mid, preferred_element_type=F32)
             + jnp.dot(tril, g_lo, preferred_element_type=F32))
        b_last = b[CHUNK - 1:CHUNK, :]
        q = q_ref[rows, :].astype(F32) * (GLA_DK ** -0.5)
        k = jnp.where(row_valid, k_ref[rows, :].astype(F32), 0.0)
        q_dec_all = (q * jnp.exp(b)).astype(BF16)
        k_dec_all = (k * jnp.exp(-b)).astype(BF16)
        k_last_all = (k * jnp.exp(b_last - b)).astype(BF16)
        decay = jnp.exp(b_last)
        for h in range(GLA_HEADS):
            ks = slice(h * GLA_DK, (h + 1) * GLA_DK)
            vs = slice(h * GLA_DV, (h + 1) * GLA_DV)
            q_dec = q_dec_all[:, ks]
            v = v_ref[rows, vs]
            att = lax.dot_general(q_dec, k_dec_all[:, ks], (((1,), (1,)), ((), ())),
                                  preferred_element_type=F32)
            att = jnp.where(causal, att, 0.0).astype(BF16)
            st = st_ref[h]
            o = jnp.dot(att, v, preferred_element_type=F32)
            o = o + lax.dot_general(q_dec, st.astype(BF16), (((1,), (1,)), ((), ())),
                                    preferred_element_type=F32)
            st_ref[h] = st * decay[:, ks] + lax.dot_general(
                v, k_last_all[:, ks], (((0,), (0,)), ((), ())), preferred_element_type=F32)
            on = o * lax.rsqrt(jnp.mean(o * o, axis=-1, keepdims=True) + EPS) * gn
            go = go_ref[rows, vs].astype(F32)
            o_ref[rows, vs] = (on * (go * jax.nn.sigmoid(go))).astype(o_ref.dtype)
        return carry

    lax.fori_loop(0, n_chunks, body, 0, unroll=4 if n_chunks % 4 == 0 else 1)

    @pl.when(rb == pl.num_programs(1) - 1)
    def _():
        for h in range(GLA_HEADS):
            sfin_ref[h] = st_ref[h].T


def gla_scan(proj, gates, gnorm, s0, *, n_seq, seq_len, row0, n_valid, s0_per_seq):
    blk = min(seq_len, GLA_ROW_BLOCK)
    n_rb = seq_len // blk
    hk = GLA_HEADS * GLA_DK
    hv = GLA_HEADS * GLA_DV
    rb0 = row0 // blk
    row = lambda b, r: rb0 + b * n_rb + r
    s0_map = (lambda b, r: (b, 0, 0, 0)) if s0_per_seq else (lambda b, r: (0, 0, 0, 0))
    kern = functools.partial(_gla_kernel, n_chunks=blk // CHUNK, n_valid=n_valid)
    return pl.pallas_call(
        kern,
        grid=(n_seq, n_rb),
        in_specs=[
            pl.BlockSpec((blk, hk), lambda b, r: (row(b, r), 0)),
            pl.BlockSpec((blk, hk), lambda b, r: (row(b, r), 1)),
            pl.BlockSpec((blk, hv), lambda b, r: (row(b, r), 2 * hk // hv)),
            pl.BlockSpec((blk, hv), lambda b, r: (row(b, r), 2 * hk // hv + 1)),
            pl.BlockSpec((blk, hk), lambda b, r: (row(b, r), 0)),
            pl.BlockSpec((1, GLA_DV), lambda b, r: (0, 0)),
            pl.BlockSpec((None, GLA_HEADS, GLA_DK, GLA_DV), s0_map),
        ],
        out_specs=[
            pl.BlockSpec((blk, hv), lambda b, r: (b * n_rb + r, 0)),
            pl.BlockSpec((None, GLA_HEADS, GLA_DK, GLA_DV), lambda b, r: (b, 0, 0, 0)),
        ],
        out_shape=[jax.ShapeDtypeStruct((n_seq * seq_len, hv), BF16),
                   jax.ShapeDtypeStruct((n_seq, GLA_HEADS, GLA_DK, GLA_DV), F32)],
        scratch_shapes=[pltpu.VMEM((GLA_HEADS, GLA_DV, GLA_DK), F32)],
        compiler_params=_params(("parallel", "arbitrary"), 48),
        name="gla_scan",
    )(proj, proj, proj, proj, gates, gnorm, s0)


WIN_KEYS = (WINDOW_CHUNKS + 1) * CHUNK
ATTN_KEYS = 2 * LANES
ATTN_ROW_BLOCK = 512
PAIRS = SWA_GROUP // 2


def _lane_halves(x2, head_in_pair):
    lane = lax.broadcasted_iota(jnp.int32, x2.shape, 1)
    low = lane < HEAD_DIM
    swapped = pltpu.roll(x2, HEAD_DIM, 1)
    if head_in_pair == 0:
        lo, hi = jnp.where(low, x2, 0.0), jnp.where(low, 0.0, swapped)
    else:
        lo, hi = jnp.where(low, swapped, 0.0), jnp.where(low, 0.0, x2)
    return lo.astype(BF16), hi.astype(BF16)


def _attn_kernel(q_ref, k_ref, v_ref, km_ref, vm_ref, sink_ref, o_ref, kb_ref, vb_ref, kmb_ref, vmb_ref,
                 *, n_chunks, chunk_offset):
    step = pl.program_id(1)

    @pl.when(step == 0)
    def _():
        for h in range(SWA_KV_HEADS):
            cols = slice((h // 2) * LANES, (h // 2 + 1) * LANES)
            for src, dst in ((k_ref, kb_ref), (v_ref, vb_ref), (km_ref, kmb_ref), (vm_ref, vmb_ref)):
                lo, hi = _lane_halves(src[:, cols], h % 2)
                dst[h, 0] = lo
                dst[h, 1] = hi

    j = lax.broadcasted_iota(jnp.int32, (1, ATTN_KEYS), 1)
    rel_chunk = (j >= N_META + CHUNK).astype(jnp.int32) + (j >= N_META + 2 * CHUNK).astype(jnp.int32)
    in_window = (j >= N_META) & (j < N_META + WIN_KEYS)
    is_sink_slot = j == N_META + WIN_KEYS
    zpad = jnp.zeros((ATTN_KEYS - N_META - WIN_KEYS, LANES), BF16)
    key_row = lax.broadcasted_iota(jnp.int32, (2 * ATTN_KEYS, LANES), 0)
    key_lane = lax.broadcasted_iota(jnp.int32, (2 * ATTN_KEYS, LANES), 1)
    ones_cols = ((key_row < ATTN_KEYS) == (key_lane < HEAD_DIM)).astype(BF16)
    nt = (((1,), (1,)), ((), ()))

    def body(ci, carry):
        c = step * n_chunks + ci + chunk_offset
        wc = jnp.maximum(c - WINDOW_CHUNKS, 0)
        win = pl.ds(pl.multiple_of(wc * CHUNK, CHUNK), WIN_KEYS)
        rows = pl.ds(pl.multiple_of(ci * CHUNK, CHUNK), CHUNK)
        valid = (j < N_META) | (in_window & (wc + rel_chunk <= c))
        for h in range(SWA_KV_HEADS):
            kb = jnp.concatenate([kmb_ref[h, 0], kb_ref[h, 0, win, :], zpad,
                                  kmb_ref[h, 1], kb_ref[h, 1, win, :], zpad], axis=0)
            vb = jnp.concatenate([vmb_ref[h, 0], vb_ref[h, 0, win, :], zpad,
                                  vmb_ref[h, 1], vb_ref[h, 1, win, :], zpad], axis=0)
            col = lambda p: slice((h * PAIRS + p) * LANES, (h * PAIRS + p + 1) * LANES)
            qs = jnp.concatenate([q_ref[rows, col(p)] for p in range(PAIRS)], axis=0)
            s = lax.dot_general(qs, kb, nt, preferred_element_type=F32)
            probs = []
            for half in range(2):
                sink = sink_ref[h, half]
                sh = jnp.where(valid, s[:, half * ATTN_KEYS:(half + 1) * ATTN_KEYS], NEG_INF)
                sh = jnp.where(is_sink_slot, jnp.concatenate([sink] * (ATTN_KEYS // LANES), axis=1), sh)
                m = jnp.max(sh, axis=-1, keepdims=True)
                probs.append(jnp.exp(sh - m).astype(BF16))
            ov = jnp.dot(jnp.concatenate(probs, axis=1), jnp.concatenate([vb, ones_cols], axis=1),
                         preferred_element_type=F32)
            o = ov[:, :LANES] / ov[:, LANES:]
            for p in range(PAIRS):
                o_ref[rows, col(p)] = o[p * CHUNK:(p + 1) * CHUNK].astype(o_ref.dtype)
        return carry

    lax.fori_loop(0, n_chunks, body, 0, unroll=4 if n_chunks % 4 == 0 else 1)


def window_attention(q, k_arr, v_arr, km_arr, vm_arr, sink_tab, *, n_seq, seq_len, key_len, row0,
                     k_map, v_map, km_map, vm_map, chunk_offset):
    blk = min(seq_len, ATTN_ROW_BLOCK)
    steps = seq_len // blk
    q_spec = pl.BlockSpec((blk, q.shape[1]), lambda b, s: (row0 // blk + b * steps + s, 0))
    o_spec = pl.BlockSpec((blk, q.shape[1]), lambda b, s: (b * steps + s, 0))
    width = SWA_KV_HEADS * HEAD_DIM
    kern = functools.partial(_attn_kernel, n_chunks=blk // CHUNK, chunk_offset=chunk_offset)
    return pl.pallas_call(
        kern,
        grid=(n_seq, steps),
        in_specs=[q_spec,
                  pl.BlockSpec((key_len, width), k_map),
                  pl.BlockSpec((key_len, width), v_map),
                  pl.BlockSpec((N_META, width), km_map),
                  pl.BlockSpec((N_META, width), vm_map),
                  pl.BlockSpec((SWA_KV_HEADS, 2, PAIRS * CHUNK, LANES), lambda b, s: (0, 0, 0, 0))],
        out_specs=o_spec,
        out_shape=jax.ShapeDtypeStruct((n_seq * seq_len, q.shape[1]), q.dtype),
        scratch_shapes=[pltpu.VMEM((SWA_KV_HEADS, 2, key_len, LANES), BF16),
                        pltpu.VMEM((SWA_KV_HEADS, 2, key_len, LANES), BF16),
                        pltpu.VMEM((SWA_KV_HEADS, 2, N_META, LANES), BF16),
                        pltpu.VMEM((SWA_KV_HEADS, 2, N_META, LANES), BF16)],
        compiler_params=_params(("parallel", "arbitrary"), 48),
        name="window_attention",
    )(q, k_arr, v_arr, km_arr, vm_arr, sink_tab)


def _route_tile(xn, wr_ref, xp_ref, rt_ref, cnt_ref, carry_ref):
    bits = pltpu.bitcast(xn.astype(BF16).astype(F32), jnp.uint32)
    half = D_MODEL // 2
    xp_ref[...] = (bits[:, :half] >> 16) | (bits[:, half:] & jnp.uint32(0xFFFF0000))

    tm = xn.shape[0]
    hi = xn.astype(BF16)
    lo = (xn - hi.astype(F32)).astype(BF16)
    wide = jnp.dot(hi, wr_ref[...], preferred_element_type=F32)
    logits = (wide[:, :LANES] + wide[:, LANES:]
              + jnp.dot(lo, wr_ref[:, :LANES], preferred_element_type=F32))
    lane = lax.broadcasted_iota(jnp.int32, (tm, LANES), 1).astype(F32)
    neg = jnp.float32(-jnp.inf)
    lg = jnp.where(lane < N_EXPERTS, logits, neg)
    m1 = jnp.max(lg, axis=-1, keepdims=True)
    i1 = jnp.min(jnp.where(lg == m1, lane, float(LANES)), axis=-1, keepdims=True)
    lg2 = jnp.where(lane == i1, neg, lg)
    m2 = jnp.max(lg2, axis=-1, keepdims=True)
    i2 = jnp.min(jnp.where(lg2 == m2, lane, float(LANES)), axis=-1, keepdims=True)
    e21 = jnp.exp(m2 - m1)
    g1 = 1.0 / (1.0 + e21)
    g2 = e21 / (1.0 + e21)
    oh1 = (lane == i1).astype(F32)
    oh2 = (lane == i2).astype(F32)
    oh = oh1 + oh2
    ri = lax.broadcasted_iota(jnp.int32, (tm, tm), 0)
    ci = lax.broadcasted_iota(jnp.int32, (tm, tm), 1)
    before = (ci < ri).astype(BF16)
    prefix = jnp.dot(before, oh.astype(BF16), preferred_element_type=F32) + carry_ref[0:1, :]
    rank1 = jnp.sum(prefix * oh1, axis=-1, keepdims=True)
    rank2 = jnp.sum(prefix * oh2, axis=-1, keepdims=True)
    total = carry_ref[0:1, :] + jnp.sum(oh, axis=0, keepdims=True)
    carry_ref[...] = jnp.broadcast_to(total, carry_ref.shape)
    cnt_ref[...] = jnp.broadcast_to(total, cnt_ref.shape)
    rt = jnp.where(lane == 0, i1, 0.0)
    rt = jnp.where(lane == 1, i2, rt)
    rt = jnp.where(lane == 2, rank1, rt)
    rt = jnp.where(lane == 3, rank2, rt)
    rt = jnp.where(lane == 4, g1, rt)
    rt = jnp.where(lane == 5, g2, rt)
    rt_ref[...] = rt


def _rowmm_kernel(*refs, router, a_bounds, res_bounds):
    na, nr = len(a_bounds), len(res_bounds)
    a_refs, w_ref, res_refs, g_ref = refs[:na], refs[na], refs[na + 1:na + 1 + nr], refs[na + 1 + nr]
    rest = refs[na + 2 + nr:]
    t = pl.program_id(0)
    if router:
        wr_ref, h_ref, xp_ref, rt_ref, cnt_ref, wb_ref, carry_ref, wcat_ref = rest
    else:
        h_ref, xn_ref, wb_ref = rest

    @pl.when(t == 0)
    def _():
        wb_ref[...] = w_ref[...].astype(BF16)
        if router:
            carry_ref[...] = jnp.zeros_like(carry_ref)
            wr = wr_ref[...]
            wr_hi = wr.astype(BF16)
            wcat_ref[:, :LANES] = wr_hi
            wcat_ref[:, LANES:] = (wr - wr_hi.astype(F32)).astype(BF16)

    h = _read_part(res_refs, res_bounds, t) + jnp.dot(_read_part(a_refs, a_bounds, t), wb_ref[...],
                                                     preferred_element_type=F32)
    h_ref[...] = h
    xn = h * lax.rsqrt(jnp.mean(h * h, axis=-1, keepdims=True) + EPS) * g_ref[...]
    if router:
        _route_tile(xn, wcat_ref, xp_ref, rt_ref, cnt_ref, carry_ref)
    else:
        xn_ref[...] = xn.astype(xn_ref.dtype)


def row_matmul(a_parts, w, res_parts, gain, *, rows, tm, w_router_pad=None, name="row_matmul"):
    k, d = w.shape[1], w.shape[2]
    router = w_router_pad is not None
    row_spec = lambda width: pl.BlockSpec((tm, width), lambda t: (t, 0))
    a_specs, a_bounds = _part_specs(a_parts, tm)
    res_specs, res_bounds = _part_specs(res_parts, tm)
    assert min(a_bounds[-1], res_bounds[-1]) * tm >= rows
    in_specs = (a_specs + [pl.BlockSpec((None, k, d), lambda t: (0, 0, 0), pipeline_mode=pl.Buffered(1))]
                + res_specs + [pl.BlockSpec((1, d), lambda t: (0, 0))])
    args = [*a_parts, w, *res_parts, gain]
    out_specs = [row_spec(d)]
    out_shape = [jax.ShapeDtypeStruct((rows, d), F32)]
    scratch = [pltpu.VMEM((k, d), BF16)]
    if router:
        in_specs.append(pl.BlockSpec((d, LANES), lambda t: (0, 0)))
        args.append(w_router_pad)
        out_specs += [row_spec(d // 2), row_spec(LANES), pl.BlockSpec((8, LANES), lambda t: (0, 0))]
        out_shape += [jax.ShapeDtypeStruct((rows, d // 2), jnp.uint32),
                      jax.ShapeDtypeStruct((rows, LANES), F32),
                      jax.ShapeDtypeStruct((8, LANES), F32)]
        scratch += [pltpu.VMEM((8, LANES), F32), pltpu.VMEM((d, 2 * LANES), BF16)]
    else:
        out_specs.append(row_spec(d))
        out_shape.append(jax.ShapeDtypeStruct((rows, d), BF16))
    return pl.pallas_call(
        functools.partial(_rowmm_kernel, router=router, a_bounds=a_bounds, res_bounds=res_bounds),
        grid=(rows // tm,),
        in_specs=in_specs, out_specs=out_specs, out_shape=out_shape, scratch_shapes=scratch,
        compiler_params=_params(("arbitrary",), 56),
        name=name,
    )(*args)


def _scatter_kernel(pos_ref, xp_ref, xs_in_ref, xs_ref, sem):
    del xs_in_ref
    tm = xp_ref.shape[0]

    def row_copy(r, slot):
        return pltpu.make_async_copy(xp_ref.at[pl.ds(r, 1), :], xs_ref.at[pl.ds(slot, 1), :], sem)

    def issue(r, carry):
        row_copy(r, pos_ref[2 * r]).start()
        row_copy(r, pos_ref[2 * r + 1]).start()
        return carry

    lax.fori_loop(0, tm, issue, 0, unroll=8)
    for _ in range(2):
        pltpu.make_async_copy(xp_ref, xs_ref.at[pl.ds(0, tm), :], sem).wait()


def scatter_rows(pos_flat, xp, xs_init):
    rows = xp.shape[0]
    return pl.pallas_call(
        _scatter_kernel,
        grid=(rows // ROW_TILE,),
        in_specs=[pl.BlockSpec((2 * ROW_TILE,), lambda i: (i,), memory_space=pltpu.SMEM),
                  pl.BlockSpec((ROW_TILE, xp.shape[1]), lambda i: (i, 0)),
                  pl.BlockSpec(memory_space=pl.ANY)],
        out_specs=pl.BlockSpec(memory_space=pl.ANY),
        out_shape=jax.ShapeDtypeStruct(xs_init.shape, xs_init.dtype),
        scratch_shapes=[pltpu.SemaphoreType.DMA(())],
        input_output_aliases={2: 0},
        compiler_params=_params(("arbitrary",), 32),
        name="scatter_rows",
    )(pos_flat, xp, xs_init)


def _combine_kernel(pos_ref, rt_ref, h_ref, g_ref, ys_ref, yp_ref, ysm_ref, buf_ref, sem, *, n_prompt_tiles):
    i = pl.program_id(0)
    tm = h_ref.shape[0]

    def row_copy(r, k, slot):
        return pltpu.make_async_copy(ys_ref.at[pl.ds(slot, 1), :], buf_ref.at[k, pl.ds(r, 1), :], sem)

    def issue(r, carry):
        row_copy(r, 0, pos_ref[2 * r]).start()
        row_copy(r, 1, pos_ref[2 * r + 1]).start()
        return carry

    lax.fori_loop(0, tm, issue, 0, unroll=8)
    for k in range(2):
        pltpu.make_async_copy(ys_ref.at[pl.ds(0, tm), :], buf_ref.at[k], sem).wait()
    rt = rt_ref[...]
    y = rt[:, 4:5] * buf_ref[0] + rt[:, 5:6] * buf_ref[1]
    x = h_ref[...] + y
    out = x * lax.rsqrt(jnp.mean(x * x, axis=-1, keepdims=True) + EPS) * g_ref[...]

    @pl.when(i < n_prompt_tiles)
    def _():
        yp_ref[...] = out

    @pl.when(i >= n_prompt_tiles)
    def _():
        ysm_ref[...] = out


def combine_rows(pos_flat, rt, h, gain, ys, *, n_prompt_rows):
    rows = h.shape[0]
    npt = n_prompt_rows // ROW_TILE
    kern = functools.partial(_combine_kernel, n_prompt_tiles=npt)
    return pl.pallas_call(
        kern,
        grid=(rows // ROW_TILE,),
        in_specs=[pl.BlockSpec((2 * ROW_TILE,), lambda i: (i,), memory_space=pltpu.SMEM),
                  pl.BlockSpec((ROW_TILE, LANES), lambda i: (i, 0)),
                  pl.BlockSpec((ROW_TILE, D_MODEL), lambda i: (i, 0)),
                  pl.BlockSpec((1, D_MODEL), lambda i: (0, 0)),
                  pl.BlockSpec(memory_space=pl.ANY)],
        out_specs=[pl.BlockSpec((ROW_TILE, D_MODEL), lambda i: (jnp.minimum(i, npt - 1), 0)),
                   pl.BlockSpec((ROW_TILE, D_MODEL), lambda i: (jnp.maximum(i - npt, 0), 0))],
        out_shape=[jax.ShapeDtypeStruct((n_prompt_rows, D_MODEL), F32),
                   jax.ShapeDtypeStruct((rows - n_prompt_rows, D_MODEL), F32)],
        scratch_shapes=[pltpu.VMEM((2, ROW_TILE, D_MODEL), F32), pltpu.SemaphoreType.DMA(())],
        compiler_params=_params(("arbitrary",), 48),
        name="combine_rows",
    )(pos_flat, rt, h, gain, ys)


def _rope_tables(pos):
    half = HEAD_DIM // 2
    inv_freq = ROPE_THETA ** (-jnp.arange(half, dtype=F32) / half)
    ang = jnp.asarray(pos, jnp.int32).astype(F32)[:, None] * inv_freq[None, :]
    cos = jnp.cos(ang)
    sin = jnp.sin(ang)
    reps = LANES // HEAD_DIM
    return (jnp.tile(jnp.concatenate([cos, cos], axis=1), (1, reps)),
            jnp.tile(jnp.concatenate([-sin, sin], axis=1), (1, reps)))


def kernel(x_prompt, x_sample, state_gla, cache_k_meta, cache_v_meta, cache_k_win, cache_v_win, meta_tokens,
           norm_mix, norm_ffn, norm_kv, norm_final, gla_w_in, gla_w_gk, gla_b_gk, gla_norm, gla_w_out, kv_w,
           attn_w_q, attn_sinks, attn_w_out, ffn_w_gate_up, ffn_w_down, moe_w_router, moe_w_gate_up,
           moe_w_down):
    bsz, seq, d = x_prompt.shape
    dbsz, t_new, _ = x_sample.shape
    past_len = 2048
    n_p = bsz * seq
    n_s = dbsz * t_new
    r1 = n_p + n_s
    assert r1 % ROW_TILE == 0 and seq % CHUNK == 0 and t_new == CHUNK
    r0 = -(-(r1 + CHUNK) // ROW_TILE) * ROW_TILE
    meta_row = r1
    hk = GLA_HEADS * GLA_DK
    hv = GLA_HEADS * GLA_DV
    ffn_dense = ffn_w_down.shape[1]
    ffn_expert = moe_w_down.shape[2]
    kv_width = SWA_KV_HEADS * HEAD_DIM

    x_parts = [x_prompt.reshape(n_p, d),
               jnp.concatenate([x_sample.reshape(n_s, d), meta_tokens.astype(F32),
                                jnp.zeros((r0 - r1 - N_META, d), F32)], axis=0)]
    tail_rows = r0 - r1

    tm0 = r0 // 16
    tm1 = r1 // 16
    (xn0,) = rms_rows(x_parts, norm_mix[0:1])
    proj = matmul(xn0, gla_w_in, n_out=2 * hk + 2 * hv, out_dtype=BF16, tm=tm0, tn=2 * COL_TILE,
                  name="gla_in_proj")
    w_r_pad = jnp.pad(gla_w_in[0, :, 2 * hk + 2 * hv:], ((0, 0), (0, LANES - GLA_RANK)))
    wgk_pad = jnp.pad(gla_w_gk[0], ((0, LANES - GLA_RANK), (0, 0)))
    gnorm = gla_norm[0][None, :]
    gla = functools.partial(gla_scan, proj, gate_rows(xn0, w_r_pad, wgk_pad, gla_b_gk[0][None, :]), gnorm)
    og_meta, s_meta = gla(jnp.zeros((1, GLA_HEADS, GLA_DK, GLA_DV), F32), n_seq=1, seq_len=tail_rows,
                          row0=meta_row, n_valid=N_META, s0_per_seq=False)
    og_prompt, s_prompt = gla(s_meta, n_seq=bsz, seq_len=seq, row0=0, n_valid=seq, s0_per_seq=False)
    og_sample, s_sample = gla(state_gla[0].astype(F32), n_seq=dbsz, seq_len=t_new, row0=n_p,
                              n_valid=t_new, s0_per_seq=True)
    h1, hn1 = row_matmul([og_prompt, jnp.concatenate([og_sample, og_meta], axis=0)], gla_w_out, x_parts,
                         norm_ffn[0:1], rows=r0, tm=r0 // 68, name="gla_out_proj")

    act = matmul(hn1, ffn_w_gate_up, n_out=ffn_dense, out_dtype=BF16, mode="swiglu", up_col0=ffn_dense,
                 tm=2 * tm0, vmem_mib=56, name="ffn_gate_up")
    h2 = matmul(act, ffn_w_down, n_out=d, out_dtype=F32, mode="residual", residual=h1, tm=tm0 // 2,
                vmem_mib=56, name="ffn_down")

    pos = np.concatenate([np.tile(N_META + np.arange(seq), bsz),
                          np.tile(N_META + past_len + np.arange(t_new), dbsz),
                          np.arange(N_META), np.zeros(r0 - r1 - N_META, np.int64)])
    rope_tabs = _rope_tables(pos)
    xkv, xq = rms_rows([h2], jnp.stack([norm_kv, norm_mix[1]]))
    kvf = matmul(xkv, kv_w[None], n_out=2 * kv_width, out_dtype=F32, mode="rope", rope_tabs=rope_tabs,
                 rope_cols=kv_width, tm=tm0, name="shared_kv")
    q = matmul(xq, attn_w_q, n_out=d, out_dtype=BF16, mode="rope", rope_tabs=rope_tabs, rope_cols=d,
               scale=HEAD_DIM ** -0.5, rows=r1, tm=tm1, tn=2 * COL_TILE, name="attn_q")

    sink_tab = jnp.broadcast_to(
        jnp.repeat(attn_sinks[0].astype(F32).reshape(SWA_KV_HEADS, PAIRS, 2).transpose(0, 2, 1), CHUNK, axis=2)
        [..., None], (SWA_KV_HEADS, 2, PAIRS * CHUNK, LANES))
    meta_blk = meta_row // N_META
    o_prompt = window_attention(
        q, kvf, kvf, kvf, kvf, sink_tab, n_seq=bsz, seq_len=seq, key_len=seq, row0=0,
        k_map=lambda b, s: (b, 0), v_map=lambda b, s: (b, 1),
        km_map=lambda b, s: (meta_blk, 0), vm_map=lambda b, s: (meta_blk, 1), chunk_offset=0)
    k_new = kvf[n_p:r1, :kv_width]
    v_new = kvf[n_p:r1, kv_width:]
    win = cache_k_win.shape[1]
    ks = jnp.concatenate([cache_k_win.reshape(dbsz, win, kv_width).astype(F32),
                          k_new.reshape(dbsz, t_new, kv_width)], axis=1).reshape(dbsz * (win + t_new), kv_width)
    vs = jnp.concatenate([cache_v_win.reshape(dbsz, win, kv_width).astype(F32),
                          v_new.reshape(dbsz, t_new, kv_width)], axis=1).reshape(dbsz * (win + t_new), kv_width)
    assert win + t_new == (WINDOW_CHUNKS + 1) * CHUNK
    o_sample = window_attention(
        q, ks, vs, cache_k_meta.reshape(dbsz * N_META, kv_width).astype(F32),
        cache_v_meta.reshape(dbsz * N_META, kv_width).astype(F32), sink_tab,
        n_seq=dbsz, seq_len=t_new, key_len=win + t_new, row0=n_p,
        k_map=lambda b, s: (b, 0), v_map=lambda b, s: (b, 0),
        km_map=lambda b, s: (b, 0), vm_map=lambda b, s: (b, 0), chunk_offset=WINDOW_CHUNKS)
    w_router_pad = jnp.pad(moe_w_router[0], ((0, 0), (0, LANES - N_EXPERTS)))
    h3, xp, rt, cnt = row_matmul([o_prompt, o_sample], attn_w_out, [h2], norm_ffn[1:2], rows=r1, tm=r1 // 66,
                                 w_router_pad=w_router_pad, name="attn_out_route")
    unit, per_tile = MOE_UNIT, MOE_TILE // MOE_UNIT
    n_tiles_max = 2 * r1 // MOE_TILE + N_EXPERTS
    counts = cnt[0, :N_EXPERTS].astype(jnp.int32)
    units_per = (counts + unit - 1) // unit
    tiles_per = (units_per + per_tile - 1) // per_tile
    tile_end = jnp.cumsum(tiles_per)
    tile_start = tile_end - tiles_per
    experts = rt[:, 0:2].astype(jnp.int32)
    ranks = rt[:, 2:4].astype(jnp.int32)
    pos_flat = (tile_start[experts] * MOE_TILE + ranks).reshape(-1)
    n_used = tile_end[-1:].astype(jnp.int32)
    tile_id = jnp.arange(n_tiles_max, dtype=jnp.int32)
    tile_expert = jnp.minimum(jnp.sum(tile_id[:, None] >= tile_end[None, :], axis=1), N_EXPERTS - 1).astype(jnp.int32)
    tile_units = jnp.clip(units_per[tile_expert] - per_tile * (tile_id - tile_start[tile_expert]), 0, per_tile)
    tile_units = jnp.where(tile_id < n_used[0], tile_units, 0).astype(jnp.int32)
    xs = scatter_rows(pos_flat, xp, jnp.zeros((n_tiles_max * MOE_TILE, d // 2), jnp.uint32))

    def short_tile_first(step, start, count, has_short, n_steps_used):
        local = step - start
        rotated = start + jnp.where(local == 0, count - 1, local - 1)
        return jnp.where(has_short & (count >= 2) & (step < n_steps_used), rotated, step).astype(jnp.int32)

    short_last = (units_per % per_tile) != 0
    tile_order = short_tile_first(tile_id, tile_start[tile_expert], tiles_per[tile_expert],
                                  short_last[tile_expert], n_used[0])
    act_e = matmul(xs, moe_w_gate_up.reshape(N_EXPERTS, d, 2 * ffn_expert), n_out=ffn_expert, out_dtype=BF16,
                   mode="swiglu", up_col0=ffn_expert,
                   schedule=(tile_expert, tile_units[tile_order], n_used, tile_order),
                   sub_tiles=per_tile, tm=MOE_TILE, a_packed=True, vmem_mib=56, name="moe_gate_up")
    split = MOE_TILE // MOE_DOWN_TILE
    down_id = jnp.arange(n_tiles_max * split, dtype=jnp.int32)
    down_expert = jnp.repeat(tile_expert, split)
    down_valid = ((down_id % split) * MOE_DOWN_TILE // unit < jnp.repeat(tile_units, split)).astype(jnp.int32)
    down_order = short_tile_first(down_id, split * tile_start[down_expert], split * tiles_per[down_expert],
                                  short_last[down_expert], split * n_used[0])
    ys = matmul(act_e, moe_w_down.reshape(N_EXPERTS, ffn_expert, d), n_out=d, out_dtype=F32,
                schedule=(down_expert, down_valid[down_order], split * n_used, down_order), tm=MOE_DOWN_TILE,
                vmem_mib=56, name="moe_down")
    y_prompt, y_sample = combine_rows(pos_flat, rt, h3, norm_final[None, :], ys, n_prompt_rows=n_p)

    kv_meta = kvf[meta_row:meta_row + N_META]
    k_meta_p = jnp.broadcast_to(kv_meta[None, :, :kv_width], (bsz, N_META, kv_width))
    v_meta_p = jnp.broadcast_to(kv_meta[None, :, kv_width:], (bsz, N_META, kv_width))
    win_p = min(WINDOW_CHUNKS * CHUNK, seq)
    kv_win = kvf[:n_p].reshape(bsz, seq, 2 * kv_width)[:, seq - win_p:]
    shape4 = lambda a: a.reshape(a.shape[0], a.shape[1], SWA_KV_HEADS, HEAD_DIM)
    return (y_prompt.reshape(bsz, seq, d), y_sample.reshape(dbsz, t_new, d),
            s_prompt[None].astype(state_gla.dtype), s_sample[None].astype(state_gla.dtype),
            shape4(k_meta_p), shape4(v_meta_p),
            shape4(kv_win[:, :, :kv_width]), shape4(kv_win[:, :, kv_width:]),
            shape4(k_new.reshape(dbsz, t_new, kv_width)), shape4(v_new.reshape(dbsz, t_new, kv_width)))
```

```python
import functools

import numpy as np
import jax
import jax.numpy as jnp
from jax import lax
from jax.experimental import pallas as pl
from jax.experimental.pallas import tpu as pltpu

F32 = jnp.float32
BF16 = jnp.bfloat16
HIGHEST = lax.Precision.HIGHEST

D_MODEL = 2048
CHUNK = 64
N_META = 16
GLA_HEADS = 4
GLA_DK = 256
GLA_DV = 512
GLA_RANK = 16
GLA_GATE_NORM = 16.0
HEAD_DIM = 64
SWA_HEADS = 32
SWA_KV_HEADS = 4
SWA_GROUP = 8
WINDOW_CHUNKS = 2
ROPE_THETA = 10000.0
N_EXPERTS = 8
EPS = 1e-5
NEG_INF = -1e30

LANES = 128
ROW_TILE = 512
COL_TILE = 512
MOE_TILE = 1024
MOE_UNIT = 512
MOE_DOWN_TILE = 512
MIB = 2 ** 20


def _params(semantics, vmem_mib):
    return pltpu.CompilerParams(dimension_semantics=semantics, vmem_limit_bytes=vmem_mib * MIB)


def _part_specs(parts, tm):
    specs, bounds, start = [], [], 0
    for p in parts:
        n = p.shape[0] // tm
        assert n * tm == p.shape[0]
        specs.append(pl.BlockSpec((tm, p.shape[1]), lambda t, s=start, n=n: (jnp.clip(t - s, 0, n - 1), 0)))
        start += n
        bounds.append(start)
    return specs, tuple(bounds)


def _read_part(refs, bounds, t):
    x = refs[0][...]
    for ref, lo in zip(refs[1:], bounds[:-1]):
        x = jnp.where(t >= lo, ref[...], x)
    return x


def _norm_kernel(*refs, bounds):
    n_src = len(bounds)
    g_ref = refs[n_src]
    x = _read_part(refs[:n_src], bounds, pl.program_id(0))
    y = x * lax.rsqrt(jnp.mean(x * x, axis=-1, keepdims=True) + EPS)
    for i, o_ref in enumerate(refs[n_src + 1:]):
        o_ref[...] = (y * g_ref[i:i + 1, :]).astype(o_ref.dtype)


def rms_rows(parts, gains):
    d = parts[0].shape[1]
    n = gains.shape[0]
    specs, bounds = _part_specs(parts, ROW_TILE)
    rows = bounds[-1] * ROW_TILE
    return pl.pallas_call(
        functools.partial(_norm_kernel, bounds=bounds),
        grid=(bounds[-1],),
        in_specs=specs + [pl.BlockSpec((n, d), lambda i: (0, 0))],
        out_specs=[pl.BlockSpec((ROW_TILE, d), lambda i: (i, 0)) for _ in range(n)],
        out_shape=[jax.ShapeDtypeStruct((rows, d), BF16) for _ in range(n)],
        compiler_params=_params(("parallel",), 40),
        name="rms_rows",
    )(*parts, gains)


def _swap_halves(x):
    lane = lax.broadcasted_iota(jnp.int32, x.shape, 1)
    first_half = (lane % HEAD_DIM) < (HEAD_DIM // 2)
    return jnp.where(first_half, pltpu.roll(x, LANES - HEAD_DIM // 2, 1), pltpu.roll(x, HEAD_DIM // 2, 1))


def _mm_kernel(te_ref, nv_ref, nu_ref, ph_ref, *refs, mode, a_packed, rope_cols, scale, sub_tiles):
    if mode == "swiglu":
        a_ref, w_ref, w2_ref, o_ref, wb_ref, wb2_ref = refs
    elif mode == "residual":
        a_ref, w_ref, res_ref, o_ref, wb_ref = refs
    elif mode == "rope":
        a_ref, w_ref, cos_ref, sin_ref, o_ref, wb_ref = refs
    else:
        a_ref, w_ref, o_ref, wb_ref = refs
    t = pl.program_id(1)
    tm = o_ref.shape[0]
    sub = tm // sub_tiles

    def compute(n_rows):
        rows = slice(0, n_rows)
        a = a_ref[rows, :]
        if a_packed:
            lo = pltpu.bitcast(a << 16, F32)
            hi = pltpu.bitcast(a & jnp.uint32(0xFFFF0000), F32)
            a = jnp.concatenate([lo, hi], axis=1)
        a = a.astype(BF16)
        acc = jnp.dot(a, wb_ref[...], preferred_element_type=F32)
        if mode == "swiglu":
            up = jnp.dot(a, wb2_ref[...], preferred_element_type=F32)
            o_ref[rows, :] = (acc * jax.nn.sigmoid(acc) * up).astype(o_ref.dtype)
        elif mode == "residual":
            o_ref[rows, :] = (res_ref[rows, :] + acc).astype(o_ref.dtype)
        elif mode == "rope":
            cos = cos_ref[rows, :]
            sin = sin_ref[rows, :]
            for c in range(acc.shape[1] // LANES):
                x = acc[:, c * LANES:(c + 1) * LANES]
                if c * LANES < rope_cols:
                    x = x * cos + _swap_halves(x) * sin
                o_ref[rows, c * LANES:(c + 1) * LANES] = (x * scale).astype(o_ref.dtype)
        else:
            o_ref[rows, :] = acc.astype(o_ref.dtype)
        if n_rows < tm:
            o_ref[n_rows:, :] = jnp.zeros((tm - n_rows, o_ref.shape[1]), o_ref.dtype)

    @pl.when(t < nu_ref[0])
    def _():
        @pl.when((t == 0) | (te_ref[t] != te_ref[jnp.maximum(t - 1, 0)]))
        def _():
            wb_ref[...] = w_ref[...].astype(BF16)
            if mode == "swiglu":
                wb2_ref[...] = w2_ref[...].astype(BF16)

    n_valid = jnp.where(t < nu_ref[0], nv_ref[t], 0)

    @pl.when(n_valid == 0)
    def _():
        o_ref[...] = jnp.zeros_like(o_ref)

    for s in range(1, sub_tiles + 1):
        pl.when(n_valid == s)(functools.partial(compute, s * sub))


def matmul(a, w, *, n_out, out_dtype, mode="plain", schedule=None, sub_tiles=1, residual=None,
           rope_tabs=None, rope_cols=0, scale=1.0, col0=0, up_col0=0, tm=ROW_TILE, tn=COL_TILE,
           a_packed=False, rows=None, vmem_mib=48, name="matmul"):
    rows = a.shape[0] if rows is None else rows
    k = w.shape[1]
    n_tiles = rows // tm
    n_col = n_out // tn
    if schedule is None:
        schedule = (jnp.zeros((n_tiles,), jnp.int32), jnp.full((n_tiles,), sub_tiles, jnp.int32),
                    jnp.full((1,), n_tiles, jnp.int32), jnp.arange(n_tiles, dtype=jnp.int32))
    cb0 = col0 // tn
    ub0 = up_col0 // tn

    def step_tile(t, nu, ph):
        return ph[jnp.where(t < nu[0], t, 0)]

    def step_weight(j, t, te, nu, c0):
        idle = t >= nu[0]
        return te[jnp.where(idle, 0, t)], 0, c0 + jnp.where(idle, jnp.minimum(j + 1, n_col - 1), j)

    a_spec = pl.BlockSpec((tm, a.shape[1]), lambda j, t, te, nv, nu, ph: (step_tile(t, nu, ph), 0))
    w_spec = pl.BlockSpec((None, k, tn), lambda j, t, te, nv, nu, ph: step_weight(j, t, te, nu, cb0))
    o_spec = pl.BlockSpec((tm, tn), lambda j, t, te, nv, nu, ph: (ph[t], j))
    in_specs = [a_spec, w_spec]
    args = [a, w]
    scratch = [pltpu.VMEM((k, tn), BF16)]
    if mode == "swiglu":
        in_specs.append(pl.BlockSpec((None, k, tn), lambda j, t, te, nv, nu, ph: step_weight(j, t, te, nu, ub0)))
        args.append(w)
        scratch.append(pltpu.VMEM((k, tn), BF16))
    elif mode == "residual":
        in_specs.append(o_spec)
        args.append(residual)
    elif mode == "rope":
        tab_spec = pl.BlockSpec((tm, LANES), lambda j, t, te, nv, nu, ph: (step_tile(t, nu, ph), 0))
        in_specs += [tab_spec, tab_spec]
        args += list(rope_tabs)
    kern = functools.partial(_mm_kernel, mode=mode, a_packed=a_packed, rope_cols=rope_cols, scale=scale,
                             sub_tiles=sub_tiles)
    return pl.pallas_call(
        kern,
        grid_spec=pltpu.PrefetchScalarGridSpec(
            num_scalar_prefetch=4, grid=(n_col, n_tiles),
            in_specs=in_specs, out_specs=o_spec, scratch_shapes=scratch),
        out_shape=jax.ShapeDtypeStruct((rows, n_out), out_dtype),
        compiler_params=_params(("arbitrary", "arbitrary"), vmem_mib),
        name=name,
    )(*schedule, *args)


GLA_ROW_BLOCK = 512


def _gate_kernel(x_ref, wr_ref, wgk_ref, bgk_ref, g_ref):
    r = jnp.dot(x_ref[...], wr_ref[...].astype(BF16), preferred_element_type=F32)
    r_hi = r.astype(BF16)
    r_lo = (r - r_hi.astype(F32)).astype(BF16)
    w = wgk_ref[...]
    w_hi = w.astype(BF16)
    w_lo = (w - w_hi.astype(F32)).astype(BF16)
    z = (jnp.dot(r_hi, w_hi, preferred_element_type=F32) + jnp.dot(r_hi, w_lo, preferred_element_type=F32)
         + jnp.dot(r_lo, w_hi, preferred_element_type=F32) + bgk_ref[...])
    g_ref[...] = (jnp.minimum(z, 0.0) - jnp.log(1.0 + jnp.exp(-jnp.abs(z)))) / GLA_GATE_NORM


def gate_rows(xn, w_r_pad, wgk_pad, bgk):
    rows, d = xn.shape
    width = wgk_pad.shape[1]
    return pl.pallas_call(
        _gate_kernel,
        grid=(rows // ROW_TILE,),
        in_specs=[pl.BlockSpec((ROW_TILE, d), lambda i: (i, 0)),
                  pl.BlockSpec((d, LANES), lambda i: (0, 0)),
                  pl.BlockSpec((LANES, width), lambda i: (0, 0)),
                  pl.BlockSpec((1, width), lambda i: (0, 0))],
        out_specs=pl.BlockSpec((ROW_TILE, width), lambda i: (i, 0)),
        out_shape=jax.ShapeDtypeStruct((rows, width), F32),
        compiler_params=_params(("parallel",), 32),
        name="gate_rows",
    )(xn, w_r_pad, wgk_pad, bgk)


def _gla_kernel(q_ref, k_ref, v_ref, go_ref, g_ref, gn_ref, s0_ref,
                o_ref, sfin_ref, st_ref, *, n_chunks, n_valid):
    rb = pl.program_id(1)

    @pl.when(rb == 0)
    def _():
        for h in range(GLA_HEADS):
            st_ref[h] = s0_ref[h].T

    ri = lax.broadcasted_iota(jnp.int32, (CHUNK, CHUNK), 0)
    ci = lax.broadcasted_iota(jnp.int32, (CHUNK, CHUNK), 1)
    causal = ci <= ri
    tril = jnp.where(causal, 1.0, 0.0).astype(BF16)
    row_in_chunk = lax.broadcasted_iota(jnp.int32, (CHUNK, 1), 0)
    gn = gn_ref[...]

    def body(c, carry):
        rows = pl.ds(pl.multiple_of(c * CHUNK, CHUNK), CHUNK)
        row_valid = (rb * n_chunks + c) * CHUNK + row_in_chunk < n_valid
        g = jnp.where(row_valid, g_ref[rows, :], 0.0)
        g_hi = g.astype(BF16)
        rest = g - g_hi.astype(F32)
        g_mid = rest.astype(BF16)
        g_lo = (rest - g_mid.astype(F32)).astype(BF16)
        b = (jnp.dot(tril, g_hi, preferred_element_type=F32) + jnp.dot(tril, g_mid, preferred_element_type=F32)
             + jnp.dot(tril, g_lo, preferred_element_type=F32))
        b_last = b[CHUNK - 1:CHUNK, :]
        q = q_ref[rows, :].astype(F32) * (GLA_DK ** -0.5)
        k = jnp.where(row_valid, k_ref[rows, :].astype(F32), 0.0)
        q_dec_all = (q * jnp.exp(b)).astype(BF16)
        k_dec_all = (k * jnp.exp(-b)).astype(BF16)
        k_last_all = (k * jnp.exp(b_last - b)).astype(BF16)
        decay = jnp.exp(b_last)
        for h in range(GLA_HEADS):
            ks = slice(h * GLA_DK, (h + 1) * GLA_DK)
            vs = slice(h * GLA_DV, (h + 1) * GLA_DV)
            q_dec = q_dec_all[:, ks]
            v = v_ref[rows, vs]
            att = lax.dot_general(q_dec, k_dec_all[:, ks], (((1,), (1,)), ((), ())),
                                  preferred_element_type=F32)
            att = jnp.where(causal, att, 0.0).astype(BF16)
            st = st_ref[h]
            o = jnp.dot(att, v, preferred_element_type=F32)
            o = o + lax.dot_general(q_dec, st.astype(BF16), (((1,), (1,)), ((), ())),
                                    preferred_element_type=F32)
            st_ref[h] = st * decay[:, ks] + lax.dot_general(
                v, k_last_all[:, ks], (((0,), (0,)), ((), ())), preferred_element_type=F32)
            on = o * lax.rsqrt(jnp.mean(o * o, axis=-1, keepdims=True) + EPS) * gn
            go = go_ref[rows, vs].astype(F32)
            o_ref[rows, vs] = (on * (go * jax.nn.sigmoid(go))).astype(o_ref.dtype)
        return carry

    lax.fori_loop(0, n_chunks, body, 0, unroll=4 if n_chunks % 4 == 0 else 1)

    @pl.when(rb == pl.num_programs(1) - 1)
    def _():
        for h in range(GLA_HEADS):
            sfin_ref[h] = st_ref[h].T


def gla_scan(proj, gates, gnorm, s0, *, n_seq, seq_len, row0, n_valid, s0_per_seq):
    blk = min(seq_len, GLA_ROW_BLOCK)
    n_rb = seq_len // blk
    hk = GLA_HEADS * GLA_DK
    hv = GLA_HEADS * GLA_DV
    rb0 = row0 // blk
    row = lambda b, r: rb0 + b * n_rb + r
    s0_map = (lambda b, r: (b, 0, 0, 0)) if s0_per_seq else (lambda b, r: (0, 0, 0, 0))
    kern = functools.partial(_gla_kernel, n_chunks=blk // CHUNK, n_valid=n_valid)
    return pl.pallas_call(
        kern,
        grid=(n_seq, n_rb),
        in_specs=[
            pl.BlockSpec((blk, hk), lambda b, r: (row(b, r), 0)),
            pl.BlockSpec((blk, hk), lambda b, r: (row(b, r), 1)),
            pl.BlockSpec((blk, hv), lambda b, r: (row(b, r), 2 * hk // hv)),
            pl.BlockSpec((blk, hv), lambda b, r: (row(b, r), 2 * hk // hv + 1)),
            pl.BlockSpec((blk, hk), lambda b, r: (row(b, r), 0)),
            pl.BlockSpec((1, GLA_DV), lambda b, r: (0, 0)),
            pl.BlockSpec((None, GLA_HEADS, GLA_DK, GLA_DV), s0_map),
        ],
        out_specs=[
            pl.BlockSpec((blk, hv), lambda b, r: (b * n_rb + r, 0)),
            pl.BlockSpec((None, GLA_HEADS, GLA_DK, GLA_DV), lambda b, r: (b, 0, 0, 0)),
        ],
        out_shape=[jax.ShapeDtypeStruct((n_seq * seq_len, hv), BF16),
                   jax.ShapeDtypeStruct((n_seq, GLA_HEADS, GLA_DK, GLA_DV), F32)],
        scratch_shapes=[pltpu.VMEM((GLA_HEADS, GLA_DV, GLA_DK), F32)],
        compiler_params=_params(("parallel", "arbitrary"), 48),
        name="gla_scan",
    )(proj, proj, proj, proj, gates, gnorm, s0)


WIN_KEYS = (WINDOW_CHUNKS + 1) * CHUNK
ATTN_KEYS = 2 * LANES
ATTN_ROW_BLOCK = 512
PAIRS = SWA_GROUP // 2


def _lane_halves(x2, head_in_pair):
    lane = lax.broadcasted_iota(jnp.int32, x2.shape, 1)
    low = lane < HEAD_DIM
    swapped = pltpu.roll(x2, HEAD_DIM, 1)
    if head_in_pair == 0:
        lo, hi = jnp.where(low, x2, 0.0), jnp.where(low, 0.0, swapped)
    else:
        lo, hi = jnp.where(low, swapped, 0.0), jnp.where(low, 0.0, x2)
    return lo.astype(BF16), hi.astype(BF16)


def _attn_kernel(q_ref, k_ref, v_ref, km_ref, vm_ref, sink_ref, o_ref, kb_ref, vb_ref, kmb_ref, vmb_ref,
                 *, n_chunks, chunk_offset):
    step = pl.program_id(1)

    @pl.when(step == 0)
    def _():
        for h in range(SWA_KV_HEADS):
            cols = slice((h // 2) * LANES, (h // 2 + 1) * LANES)
            for src, dst in ((k_ref, kb_ref), (v_ref, vb_ref), (km_ref, kmb_ref), (vm_ref, vmb_ref)):
                lo, hi = _lane_halves(src[:, cols], h % 2)
                dst[h, 0] = lo
                dst[h, 1] = hi

    j = lax.broadcasted_iota(jnp.int32, (1, ATTN_KEYS), 1)
    rel_chunk = (j >= N_META + CHUNK).astype(jnp.int32) + (j >= N_META + 2 * CHUNK).astype(jnp.int32)
    in_window = (j >= N_META) & (j < N_META + WIN_KEYS)
    is_sink_slot = j == N_META + WIN_KEYS
    zpad = jnp.zeros((ATTN_KEYS - N_META - WIN_KEYS, LANES), BF16)
    key_row = lax.broadcasted_iota(jnp.int32, (2 * ATTN_KEYS, LANES), 0)
    key_lane = lax.broadcasted_iota(jnp.int32, (2 * ATTN_KEYS, LANES), 1)
    ones_cols = ((key_row < ATTN_KEYS) == (key_lane < HEAD_DIM)).astype(BF16)
    nt = (((1,), (1,)), ((), ()))

    def body(ci, carry):
        c = step * n_chunks + ci + chunk_offset
        wc = jnp.maximum(c - WINDOW_CHUNKS, 0)
        win = pl.ds(pl.multiple_of(wc * CHUNK, CHUNK), WIN_KEYS)
        rows = pl.ds(pl.multiple_of(ci * CHUNK, CHUNK), CHUNK)
        valid = (j < N_META) | (in_window & (wc + rel_chunk <= c))
        for h in range(SWA_KV_HEADS):
            kb = jnp.concatenate([kmb_ref[h, 0], kb_ref[h, 0, win, :], zpad,
                                  kmb_ref[h, 1], kb_ref[h, 1, win, :], zpad], axis=0)
            vb = jnp.concatenate([vmb_ref[h, 0], vb_ref[h, 0, win, :], zpad,
                                  vmb_ref[h, 1], vb_ref[h, 1, win, :], zpad], axis=0)
            col = lambda p: slice((h * PAIRS + p) * LANES, (h * PAIRS + p + 1) * LANES)
            qs = jnp.concatenate([q_ref[rows, col(p)] for p in range(PAIRS)], axis=0)
            s = lax.dot_general(qs, kb, nt, preferred_element_type=F32)
            probs = []
            for half in range(2):
                sink = sink_ref[h, half]
                sh = jnp.where(valid, s[:, half * ATTN_KEYS:(half + 1) * ATTN_KEYS], NEG_INF)
                sh = jnp.where(is_sink_slot, jnp.concatenate([sink] * (ATTN_KEYS // LANES), axis=1), sh)
                m = jnp.max(sh, axis=-1, keepdims=True)
                probs.append(jnp.exp(sh - m).astype(BF16))
            ov = jnp.dot(jnp.concatenate(probs, axis=1), jnp.concatenate([vb, ones_cols], axis=1),
                         preferred_element_type=F32)
            o = ov[:, :LANES] / ov[:, LANES:]
            for p in range(PAIRS):
                o_ref[rows, col(p)] = o[p * CHUNK:(p + 1) * CHUNK].astype(o_ref.dtype)
        return carry

    lax.fori_loop(0, n_chunks, body, 0, unroll=4 if n_chunks % 4 == 0 else 1)


def window_attention(q, k_arr, v_arr, km_arr, vm_arr, sink_tab, *, n_seq, seq_len, key_len, row0,
                     k_map, v_map, km_map, vm_map, chunk_offset):
    blk = min(seq_len, ATTN_ROW_BLOCK)
    steps = seq_len // blk
    q_spec = pl.BlockSpec((blk, q.shape[1]), lambda b, s: (row0 // blk + b * steps + s, 0))
    o_spec = pl.BlockSpec((blk, q.shape[1]), lambda b, s: (b * steps + s, 0))
    width = SWA_KV_HEADS * HEAD_DIM
    kern = functools.partial(_attn_kernel, n_chunks=blk // CHUNK, chunk_offset=chunk_offset)
    return pl.pallas_call(
        kern,
        grid=(n_seq, steps),
        in_specs=[q_spec,
                  pl.BlockSpec((key_len, width), k_map),
                  pl.BlockSpec((key_len, width), v_map),
                  pl.BlockSpec((N_META, width), km_map),
                  pl.BlockSpec((N_META, width), vm_map),
                  pl.BlockSpec((SWA_KV_HEADS, 2, PAIRS * CHUNK, LANES), lambda b, s: (0, 0, 0, 0))],
        out_specs=o_spec,
        out_shape=jax.ShapeDtypeStruct((n_seq * seq_len, q.shape[1]), q.dtype),
        scratch_shapes=[pltpu.VMEM((SWA_KV_HEADS, 2, key_len, LANES), BF16),
                        pltpu.VMEM((SWA_KV_HEADS, 2, key_len, LANES), BF16),
                        pltpu.VMEM((SWA_KV_HEADS, 2, N_META, LANES), BF16),
                        pltpu.VMEM((SWA_KV_HEADS, 2, N_META, LANES), BF16)],
        compiler_params=_params(("parallel", "arbitrary"), 48),
        name="window_attention",
    )(q, k_arr, v_arr, km_arr, vm_arr, sink_tab)


def _route_tile(xn, wr_ref, xp_ref, rt_ref, cnt_ref, carry_ref):
    bits = pltpu.bitcast(xn.astype(BF16).astype(F32), jnp.uint32)
    half = D_MODEL // 2
    xp_ref[...] = (bits[:, :half] >> 16) | (bits[:, half:] & jnp.uint32(0xFFFF0000))

    tm = xn.shape[0]
    hi = xn.astype(BF16)
    lo = (xn - hi.astype(F32)).astype(BF16)
    wide = jnp.dot(hi, wr_ref[...], preferred_element_type=F32)
    logits = (wide[:, :LANES] + wide[:, LANES:]
              + jnp.dot(lo, wr_ref[:, :LANES], preferred_element_type=F32))
    lane = lax.broadcasted_iota(jnp.int32, (tm, LANES), 1).astype(F32)
    neg = jnp.float32(-jnp.inf)
    lg = jnp.where(lane < N_EXPERTS, logits, neg)
    m1 = jnp.max(lg, axis=-1, keepdims=True)
    i1 = jnp.min(jnp.where(lg == m1, lane, float(LANES)), axis=-1, keepdims=True)
    lg2 = jnp.where(lane == i1, neg, lg)
    m2 = jnp.max(lg2, axis=-1, keepdims=True)
    i2 = jnp.min(jnp.where(lg2 == m2, lane, float(LANES)), axis=-1, keepdims=True)
    e21 = jnp.exp(m2 - m1)
    g1 = 1.0 / (1.0 + e21)
    g2 = e21 / (1.0 + e21)
    oh1 = (lane == i1).astype(F32)
    oh2 = (lane == i2).astype(F32)
    oh = oh1 + oh2
    ri = lax.broadcasted_iota(jnp.int32, (tm, tm), 0)
    ci = lax.broadcasted_iota(jnp.int32, (tm, tm), 1)
    before = (ci < ri).astype(BF16)
    prefix = jnp.dot(before, oh.astype(BF16), preferred_element_type=F32) + carry_ref[0:1, :]
    rank1 = jnp.sum(prefix * oh1, axis=-1, keepdims=True)
    rank2 = jnp.sum(prefix * oh2, axis=-1, keepdims=True)
    total = carry_ref[0:1, :] + jnp.sum(oh, axis=0, keepdims=True)
    carry_ref[...] = jnp.broadcast_to(total, carry_ref.shape)
    cnt_ref[...] = jnp.broadcast_to(total, cnt_ref.shape)
    rt = jnp.where(lane == 0, i1, 0.0)
    rt = jnp.where(lane == 1, i2, rt)
    rt = jnp.where(lane == 2, rank1, rt)
    rt = jnp.where(lane == 3, rank2, rt)
    rt = jnp.where(lane == 4, g1, rt)
    rt = jnp.where(lane == 5, g2, rt)
    rt_ref[...] = rt


def _rowmm_kernel(*refs, router, a_bounds, res_bounds):
    na, nr = len(a_bounds), len(res_bounds)
    a_refs, w_ref, res_refs, g_ref = refs[:na], refs[na], refs[na + 1:na + 1 + nr], refs[na + 1 + nr]
    rest = refs[na + 2 + nr:]
    t = pl.program_id(0)
    if router:
        wr_ref, h_ref, xp_ref, rt_ref, cnt_ref, wb_ref, carry_ref, wcat_ref = rest
    else:
        h_ref, xn_ref, wb_ref = rest

    @pl.when(t == 0)
    def _():
        wb_ref[...] = w_ref[...].astype(BF16)
        if router:
            carry_ref[...] = jnp.zeros_like(carry_ref)
            wr = wr_ref[...]
            wr_hi = wr.astype(BF16)
            wcat_ref[:, :LANES] = wr_hi
            wcat_ref[:, LANES:] = (wr - wr_hi.astype(F32)).astype(BF16)

    h = _read_part(res_refs, res_bounds, t) + jnp.dot(_read_part(a_refs, a_bounds, t), wb_ref[...],
                                                     preferred_element_type=F32)
    h_ref[...] = h
    xn = h * lax.rsqrt(jnp.mean(h * h, axis=-1, keepdims=True) + EPS) * g_ref[...]
    if router:
        _route_tile(xn, wcat_ref, xp_ref, rt_ref, cnt_ref, carry_ref)
    else:
        xn_ref[...] = xn.astype(xn_ref.dtype)


def row_matmul(a_parts, w, res_parts, gain, *, rows, tm, w_router_pad=None, name="row_matmul"):
    k, d = w.shape[1], w.shape[2]
    router = w_router_pad is not None
    row_spec = lambda width: pl.BlockSpec((tm, width), lambda t: (t, 0))
    a_specs, a_bounds = _part_specs(a_parts, tm)
    res_specs, res_bounds = _part_specs(res_parts, tm)
    assert min(a_bounds[-1], res_bounds[-1]) * tm >= rows
    in_specs = (a_specs + [pl.BlockSpec((None, k, d), lambda t: (0, 0, 0), pipeline_mode=pl.Buffered(1))]
                + res_specs + [pl.BlockSpec((1, d), lambda t: (0, 0))])
    args = [*a_parts, w, *res_parts, gain]
    out_specs = [row_spec(d)]
    out_shape = [jax.ShapeDtypeStruct((rows, d), F32)]
    scratch = [pltpu.VMEM((k, d), BF16)]
    if router:
        in_specs.append(pl.BlockSpec((d, LANES), lambda t: (0, 0)))
        args.append(w_router_pad)
        out_specs += [row_spec(d // 2), row_spec(LANES), pl.BlockSpec((8, LANES), lambda t: (0, 0))]
        out_shape += [jax.ShapeDtypeStruct((rows, d // 2), jnp.uint32),
                      jax.ShapeDtypeStruct((rows, LANES), F32),
                      jax.ShapeDtypeStruct((8, LANES), F32)]
        scratch += [pltpu.VMEM((8, LANES), F32), pltpu.VMEM((d, 2 * LANES), BF16)]
    else:
        out_specs.append(row_spec(d))
        out_shape.append(jax.ShapeDtypeStruct((rows, d), BF16))
    return pl.pallas_call(
        functools.partial(_rowmm_kernel, router=router, a_bounds=a_bounds, res_bounds=res_bounds),
        grid=(rows // tm,),
        in_specs=in_specs, out_specs=out_specs, out_shape=out_shape, scratch_shapes=scratch,
        compiler_params=_params(("arbitrary",), 56),
        name=name,
    )(*args)


def _scatter_kernel(pos_ref, xp_ref, xs_in_ref, xs_ref, sem):
    del xs_in_ref
    tm = xp_ref.shape[0]

    def row_copy(r, slot):
        return pltpu.make_async_copy(xp_ref.at[pl.ds(r, 1), :], xs_ref.at[pl.ds(slot, 1), :], sem)

    def issue(r, carry):
        row_copy(r, pos_ref[2 * r]).start()
        row_copy(r, pos_ref[2 * r + 1]).start()
        return carry

    lax.fori_loop(0, tm, issue, 0, unroll=8)
    for _ in range(2):
        pltpu.make_async_copy(xp_ref, xs_ref.at[pl.ds(0, tm), :], sem).wait()


def scatter_rows(pos_flat, xp, xs_init):
    rows = xp.shape[0]
    return pl.pallas_call(
        _scatter_kernel,
        grid=(rows // ROW_TILE,),
        in_specs=[pl.BlockSpec((2 * ROW_TILE,), lambda i: (i,), memory_space=pltpu.SMEM),
                  pl.BlockSpec((ROW_TILE, xp.shape[1]), lambda i: (i, 0)),
                  pl.BlockSpec(memory_space=pl.ANY)],
        out_specs=pl.BlockSpec(memory_space=pl.ANY),
        out_shape=jax.ShapeDtypeStruct(xs_init.shape, xs_init.dtype),
        scratch_shapes=[pltpu.SemaphoreType.DMA(())],
        input_output_aliases={2: 0},
        compiler_params=_params(("arbitrary",), 32),
        name="scatter_rows",
    )(pos_flat, xp, xs_init)


def _combine_kernel(pos_ref, rt_ref, h_ref, g_ref, ys_ref, yp_ref, ysm_ref, buf_ref, sem, *, n_prompt_tiles):
    i = pl.program_id(0)
    tm = h_ref.shape[0]

    def row_copy(r, k, slot):
        return pltpu.make_async_copy(ys_ref.at[pl.ds(slot, 1), :], buf_ref.at[k, pl.ds(r, 1), :], sem)

    def issue(r, carry):
        row_copy(r, 0, pos_ref[2 * r]).start()
        row_copy(r, 1, pos_ref[2 * r + 1]).start()
        return carry

    lax.fori_loop(0, tm, issue, 0, unroll=8)
    for k in range(2):
        pltpu.make_async_copy(ys_ref.at[pl.ds(0, tm), :], buf_ref.at[k], sem).wait()
    rt = rt_ref[...]
    y = rt[:, 4:5] * buf_ref[0] + rt[:, 5:6] * buf_ref[1]
    x = h_ref[...] + y
    out = x * lax.rsqrt(jnp.mean(x * x, axis=-1, keepdims=True) + EPS) * g_ref[...]

    @pl.when(i < n_prompt_tiles)
    def _():
        yp_ref[...] = out

    @pl.when(i >= n_prompt_tiles)
    def _():
        ysm_ref[...] = out


def combine_rows(pos_flat, rt, h, gain, ys, *, n_prompt_rows):
    rows = h.shape[0]
    npt = n_prompt_rows // ROW_TILE
    kern = functools.partial(_combine_kernel, n_prompt_tiles=npt)
    return pl.pallas_call(
        kern,
        grid=(rows // ROW_TILE,),
        in_specs=[pl.BlockSpec((2 * ROW_TILE,), lambda i: (i,), memory_space=pltpu.SMEM),
                  pl.BlockSpec((ROW_TILE, LANES), lambda i: (i, 0)),
                  pl.BlockSpec((ROW_TILE, D_MODEL), lambda i: (i, 0)),
                  pl.BlockSpec((1, D_MODEL), lambda i: (0, 0)),
                  pl.BlockSpec(memory_space=pl.ANY)],
        out_specs=[pl.BlockSpec((ROW_TILE, D_MODEL), lambda i: (jnp.minimum(i, npt - 1), 0)),
                   pl.BlockSpec((ROW_TILE, D_MODEL), lambda i: (jnp.maximum(i - npt, 0), 0))],
        out_shape=[jax.ShapeDtypeStruct((n_prompt_rows, D_MODEL), F32),
                   jax.ShapeDtypeStruct((rows - n_prompt_rows, D_MODEL), F32)],
        scratch_shapes=[pltpu.VMEM((2, ROW_TILE, D_MODEL), F32), pltpu.SemaphoreType.DMA(())],
        compiler_params=_params(("arbitrary",), 48),
        name="combine_rows",
    )(pos_flat, rt, h, gain, ys)


def _rope_tables(pos):
    half = HEAD_DIM // 2
    inv_freq = ROPE_THETA ** (-jnp.arange(half, dtype=F32) / half)
    ang = jnp.asarray(pos, jnp.int32).astype(F32)[:, None] * inv_freq[None, :]
    cos = jnp.cos(ang)
    sin = jnp.sin(ang)
    reps = LANES // HEAD_DIM
    return (jnp.tile(jnp.concatenate([cos, cos], axis=1), (1, reps)),
            jnp.tile(jnp.concatenate([-sin, sin], axis=1), (1, reps)))


def kernel(x_prompt, x_sample, state_gla, cache_k_meta, cache_v_meta, cache_k_win, cache_v_win, meta_tokens,
           norm_mix, norm_ffn, norm_kv, norm_final, gla_w_in, gla_w_gk, gla_b_gk, gla_norm, gla_w_out, kv_w,
           attn_w_q, attn_sinks, attn_w_out, ffn_w_gate_up, ffn_w_down, moe_w_router, moe_w_gate_up,
           moe_w_down):
    bsz, seq, d = x_prompt.shape
    dbsz, t_new, _ = x_sample.shape
    past_len = 2048
    n_p = bsz * seq
    n_s = dbsz * t_new
    r1 = n_p + n_s
    assert r1 % ROW_TILE == 0 and seq % CHUNK == 0 and t_new == CHUNK
    r0 = -(-(r1 + CHUNK) // ROW_TILE) * ROW_TILE
    meta_row = r1
    hk = GLA_HEADS * GLA_DK
    hv = GLA_HEADS * GLA_DV
    ffn_dense = ffn_w_down.shape[1]
    ffn_expert = moe_w_down.shape[2]
    kv_width = SWA_KV_HEADS * HEAD_DIM

    x_parts = [x_prompt.reshape(n_p, d),
               jnp.concatenate([x_sample.reshape(n_s, d), meta_tokens.astype(F32),
                                jnp.zeros((r0 - r1 - N_META, d), F32)], axis=0)]
    tail_rows = r0 - r1

    tm0 = r0 // 16
    tm1 = r1 // 16
    (xn0,) = rms_rows(x_parts, norm_mix[0:1])
    proj = matmul(xn0, gla_w_in, n_out=2 * hk + 2 * hv, out_dtype=BF16, tm=tm0, tn=2 * COL_TILE,
                  name="gla_in_proj")
    w_r_pad = jnp.pad(gla_w_in[0, :, 2 * hk + 2 * hv:], ((0, 0), (0, LANES - GLA_RANK)))
    wgk_pad = jnp.pad(gla_w_gk[0], ((0, LANES - GLA_RANK), (0, 0)))
    gnorm = gla_norm[0][None, :]
    gla = functools.partial(gla_scan, proj, gate_rows(xn0, w_r_pad, wgk_pad, gla_b_gk[0][None, :]), gnorm)
    og_meta, s_meta = gla(jnp.zeros((1, GLA_HEADS, GLA_DK, GLA_DV), F32), n_seq=1, seq_len=tail_rows,
                          row0=meta_row, n_valid=N_META, s0_per_seq=False)
    og_prompt, s_prompt = gla(s_meta, n_seq=bsz, seq_len=seq, row0=0, n_valid=seq, s0_per_seq=False)
    og_sample, s_sample = gla(state_gla[0].astype(F32), n_seq=dbsz, seq_len=t_new, row0=n_p,
                              n_valid=t_new, s0_per_seq=True)
    h1, hn1 = row_matmul([og_prompt, jnp.concatenate([og_sample, og_meta], axis=0)], gla_w_out, x_parts,
                         norm_ffn[0:1], rows=r0, tm=r0 // 68, name="gla_out_proj")

    act = matmul(hn1, ffn_w_gate_up, n_out=ffn_dense, out_dtype=BF16, mode="swiglu", up_col0=ffn_dense,
                 tm=tm0, name="ffn_gate_up")
    h2 = matmul(act, ffn_w_down, n_out=d, out_dtype=F32, mode="residual", residual=h1, tm=tm0 // 2,
                vmem_mib=56, name="ffn_down")

    pos = np.concatenate([np.tile(N_META + np.arange(seq), bsz),
                          np.tile(N_META + past_len + np.arange(t_new), dbsz),
                          np.arange(N_META), np.zeros(r0 - r1 - N_META, np.int64)])
    rope_tabs = _rope_tables(pos)
    xkv, xq = rms_rows([h2], jnp.stack([norm_kv, norm_mix[1]]))
    kvf = matmul(xkv, kv_w[None], n_out=2 * kv_width, out_dtype=F32, mode="rope", rope_tabs=rope_tabs,
                 rope_cols=kv_width, tm=tm0, name="shared_kv")
    q = matmul(xq, attn_w_q, n_out=d, out_dtype=BF16, mode="rope", rope_tabs=rope_tabs, rope_cols=d,
               scale=HEAD_DIM ** -0.5, rows=r1, tm=tm1, tn=2 * COL_TILE, name="attn_q")

    sink_tab = jnp.broadcast_to(
        jnp.repeat(attn_sinks[0].astype(F32).reshape(SWA_KV_HEADS, PAIRS, 2).transpose(0, 2, 1), CHUNK, axis=2)
        [..., None], (SWA_KV_HEADS, 2, PAIRS * CHUNK, LANES))
    meta_blk = meta_row // N_META
    o_prompt = window_attention(
        q, kvf, kvf, kvf, kvf, sink_tab, n_seq=bsz, seq_len=seq, key_len=seq, row0=0,
        k_map=lambda b, s: (b, 0), v_map=lambda b, s: (b, 1),
        km_map=lambda b, s: (meta_blk, 0), vm_map=lambda b, s: (meta_blk, 1), chunk_offset=0)
    k_new = kvf[n_p:r1, :kv_width]
    v_new = kvf[n_p:r1, kv_width:]
    win = cache_k_win.shape[1]
    ks = jnp.concatenate([cache_k_win.reshape(dbsz, win, kv_width).astype(F32),
                          k_new.reshape(dbsz, t_new, kv_width)], axis=1).reshape(dbsz * (win + t_new), kv_width)
    vs = jnp.concatenate([cache_v_win.reshape(dbsz, win, kv_width).astype(F32),
                          v_new.reshape(dbsz, t_new, kv_width)], axis=1).reshape(dbsz * (win + t_new), kv_width)
    assert win + t_new == (WINDOW_CHUNKS + 1) * CHUNK
    o_sample = window_attention(
        q, ks, vs, cache_k_meta.reshape(dbsz * N_META, kv_width).astype(F32),
        cache_v_meta.reshape(dbsz * N_META, kv_width).astype(F32), sink_tab,
        n_seq=dbsz, seq_len=t_new, key_len=win + t_new, row0=n_p,
        k_map=lambda b, s: (b, 0), v_map=lambda b, s: (b, 0),
        km_map=lambda b, s: (b, 0), vm_map=lambda b, s: (b, 0), chunk_offset=WINDOW_CHUNKS)
    w_router_pad = jnp.pad(moe_w_router[0], ((0, 0), (0, LANES - N_EXPERTS)))
    h3, xp, rt, cnt = row_matmul([o_prompt, o_sample], attn_w_out, [h2], norm_ffn[1:2], rows=r1, tm=r1 // 66,
                                 w_router_pad=w_router_pad, name="attn_out_route")
    unit, per_tile = MOE_UNIT, MOE_TILE // MOE_UNIT
    n_tiles_max = 2 * r1 // MOE_TILE + N_EXPERTS
    counts = cnt[0, :N_EXPERTS].astype(jnp.int32)
    units_per = (counts + unit - 1) // unit
    tiles_per = (units_per + per_tile - 1) // per_tile
    tile_end = jnp.cumsum(tiles_per)
    tile_start = tile_end - tiles_per
    experts = rt[:, 0:2].astype(jnp.int32)
    ranks = rt[:, 2:4].astype(jnp.int32)
    pos_flat = (tile_start[experts] * MOE_TILE + ranks).reshape(-1)
    n_used = tile_end[-1:].astype(jnp.int32)
    tile_id = jnp.arange(n_tiles_max, dtype=jnp.int32)
    tile_expert = jnp.minimum(jnp.sum(tile_id[:, None] >= tile_end[None, :], axis=1), N_EXPERTS - 1).astype(jnp.int32)
    tile_units = jnp.clip(units_per[tile_expert] - per_tile * (tile_id - tile_start[tile_expert]), 0, per_tile)
    tile_units = jnp.where(tile_id < n_used[0], tile_units, 0).astype(jnp.int32)
    xs = scatter_rows(pos_flat, xp, jnp.zeros((n_tiles_max * MOE_TILE, d // 2), jnp.uint32))

    def short_tile_first(step, start, count, has_short, n_steps_used):
        local = step - start
        rotated = start + jnp.where(local == 0, count - 1, local - 1)
        return jnp.where(has_short & (count >= 2) & (step < n_steps_used), rotated, step).astype(jnp.int32)

    short_last = (units_per % per_tile) != 0
    tile_order = short_tile_first(tile_id, tile_start[tile_expert], tiles_per[tile_expert],
                                  short_last[tile_expert], n_used[0])
    act_e = matmul(xs, moe_w_gate_up.reshape(N_EXPERTS, d, 2 * ffn_expert), n_out=ffn_expert, out_dtype=BF16,
                   mode="swiglu", up_col0=ffn_expert,
                   schedule=(tile_expert, tile_units[tile_order], n_used, tile_order),
                   sub_tiles=per_tile, tm=MOE_TILE, a_packed=True, vmem_mib=56, name="moe_gate_up")
    split = MOE_TILE // MOE_DOWN_TILE
    down_id = jnp.arange(n_tiles_max * split, dtype=jnp.int32)
    down_expert = jnp.repeat(tile_expert, split)
    down_valid = ((down_id % split) * MOE_DOWN_TILE // unit < jnp.repeat(tile_units, split)).astype(jnp.int32)
    down_order = short_tile_first(down_id, split * tile_start[down_expert], split * tiles_per[down_expert],
                                  short_last[down_expert], split * n_used[0])
    ys = matmul(act_e, moe_w_down.reshape(N_EXPERTS, ffn_expert, d), n_out=d, out_dtype=F32,
                schedule=(down_expert, down_valid[down_order], split * n_used, down_order), tm=MOE_DOWN_TILE,
                vmem_mib=56, name="moe_down")
    y_prompt, y_sample = combine_rows(pos_flat, rt, h3, norm_final[None, :], ys, n_prompt_rows=n_p)

    kv_meta = kvf[meta_row:meta_row + N_META]
    k_meta_p = jnp.broadcast_to(kv_meta[None, :, :kv_width], (bsz, N_META, kv_width))
    v_meta_p = jnp.broadcast_to(kv_meta[None, :, kv_width:], (bsz, N_META, kv_width))
    win_p = min(WINDOW_CHUNKS * CHUNK, seq)
    kv_win = kvf[:n_p].reshape(bsz, seq, 2 * kv_width)[:, seq - win_p:]
    shape4 = lambda a: a.reshape(a.shape[0], a.shape[1], SWA_KV_HEADS, HEAD_DIM)
    return (y_prompt.reshape(bsz, seq, d), y_sample.reshape(dbsz, t_new, d),
            s_prompt[None].astype(state_gla.dtype), s_sample[None].astype(state_gla.dtype),
            shape4(k_meta_p), shape4(v_meta_p),
            shape4(kv_win[:, :, :kv_width]), shape4(kv_win[:, :, kv_width:]),
            shape4(k_new.reshape(dbsz, t_new, kv_width)), shape4(v_new.reshape(dbsz, t_new, kv_width)))
```

```python
import functools

import numpy as np
import jax
import jax.numpy as jnp
from jax import lax
from jax.experimental import pallas as pl
from jax.experimental.pallas import tpu as pltpu

F32 = jnp.float32
BF16 = jnp.bfloat16

D_MODEL = 2048
PAST_LEN = 2048
CHUNK = 64
N_META = 16
GLA_HEADS = 4
GLA_DK = 256
GLA_DV = 512
GLA_RANK = 16
GLA_GATE_NORM = 16.0
HEAD_DIM = 64
SWA_KV_HEADS = 4
SWA_GROUP = 8
WINDOW_CHUNKS = 2
ROPE_THETA = 10000.0
N_EXPERTS = 8
EPS = 1e-5
NEG_INF = -1e30

LANES = 128
ROW_TILE = 512
COL_TILE = 512
MOE_TILE = 1024
MOE_UNIT = 512
MOE_DOWN_TILE = 512
MIB = 2 ** 20


def _params(semantics, vmem_mib):
    return pltpu.CompilerParams(dimension_semantics=semantics, vmem_limit_bytes=vmem_mib * MIB)


def _part_specs(parts, tm):
    specs, bounds, start = [], [], 0
    for p in parts:
        n = p.shape[0] // tm
        assert n * tm == p.shape[0]
        specs.append(pl.BlockSpec((tm, p.shape[1]), lambda t, s=start, n=n: (jnp.clip(t - s, 0, n - 1), 0)))
        start += n
        bounds.append(start)
    return specs, tuple(bounds)


def _read_part(refs, bounds, t):
    x = refs[0][...]
    for ref, lo in zip(refs[1:], bounds[:-1]):
        x = jnp.where(t >= lo, ref[...], x)
    return x


def _norm_kernel(*refs, bounds):
    n_src = len(bounds)
    g_ref = refs[n_src]
    x = _read_part(refs[:n_src], bounds, pl.program_id(0))
    y = x * lax.rsqrt(jnp.mean(x * x, axis=-1, keepdims=True) + EPS)
    for i, o_ref in enumerate(refs[n_src + 1:]):
        o_ref[...] = (y * g_ref[i:i + 1, :]).astype(o_ref.dtype)


def rms_rows(parts, gains):
    d = parts[0].shape[1]
    n = gains.shape[0]
    specs, bounds = _part_specs(parts, ROW_TILE)
    rows = bounds[-1] * ROW_TILE
    return pl.pallas_call(
        functools.partial(_norm_kernel, bounds=bounds),
        grid=(bounds[-1],),
        in_specs=specs + [pl.BlockSpec((n, d), lambda i: (0, 0))],
        out_specs=[pl.BlockSpec((ROW_TILE, d), lambda i: (i, 0)) for _ in range(n)],
        out_shape=[jax.ShapeDtypeStruct((rows, d), BF16) for _ in range(n)],
        compiler_params=_params(("parallel",), 40),
        name="rms_rows",
    )(*parts, gains)


def _swap_halves(x):
    lane = lax.broadcasted_iota(jnp.int32, x.shape, 1)
    first_half = (lane % HEAD_DIM) < (HEAD_DIM // 2)
    return jnp.where(first_half, pltpu.roll(x, LANES - HEAD_DIM // 2, 1), pltpu.roll(x, HEAD_DIM // 2, 1))


def _mm_kernel(te_ref, nv_ref, nu_ref, ph_ref, *refs, mode, a_packed, rope_cols, scale, sub_tiles):
    if mode == "swiglu":
        a_ref, w_ref, w2_ref, o_ref, wb_ref, wb2_ref = refs
    elif mode == "residual":
        a_ref, w_ref, res_ref, o_ref, wb_ref = refs
    elif mode == "rope":
        a_ref, w_ref, cos_ref, sin_ref, o_ref, wb_ref = refs
    else:
        a_ref, w_ref, o_ref, wb_ref = refs
    t = pl.program_id(1)
    tm = o_ref.shape[0]
    sub = tm // sub_tiles

    def compute(n_rows):
        rows = slice(0, n_rows)
        a = a_ref[rows, :]
        if a_packed:
            lo = pltpu.bitcast(a << 16, F32)
            hi = pltpu.bitcast(a & jnp.uint32(0xFFFF0000), F32)
            a = jnp.concatenate([lo, hi], axis=1)
        a = a.astype(BF16)
        acc = jnp.dot(a, wb_ref[...], preferred_element_type=F32)
        if mode == "swiglu":
            up = jnp.dot(a, wb2_ref[...], preferred_element_type=F32)
            o_ref[rows, :] = (acc * jax.nn.sigmoid(acc) * up).astype(o_ref.dtype)
        elif mode == "residual":
            o_ref[rows, :] = (res_ref[rows, :] + acc).astype(o_ref.dtype)
        elif mode == "rope":
            cos = cos_ref[rows, :]
            sin = sin_ref[rows, :]
            for c in range(acc.shape[1] // LANES):
                x = acc[:, c * LANES:(c + 1) * LANES]
                if c * LANES < rope_cols:
                    x = x * cos + _swap_halves(x) * sin
                o_ref[rows, c * LANES:(c + 1) * LANES] = (x * scale).astype(o_ref.dtype)
        else:
            o_ref[rows, :] = acc.astype(o_ref.dtype)
        if n_rows < tm:
            o_ref[n_rows:, :] = jnp.zeros((tm - n_rows, o_ref.shape[1]), o_ref.dtype)

    @pl.when(t < nu_ref[0])
    def _():
        @pl.when((t == 0) | (te_ref[t] != te_ref[jnp.maximum(t - 1, 0)]))
        def _():
            wb_ref[...] = w_ref[...].astype(BF16)
            if mode == "swiglu":
                wb2_ref[...] = w2_ref[...].astype(BF16)

    n_valid = jnp.where(t < nu_ref[0], nv_ref[t], 0)

    @pl.when(n_valid == 0)
    def _():
        o_ref[...] = jnp.zeros_like(o_ref)

    for s in range(1, sub_tiles + 1):
        pl.when(n_valid == s)(functools.partial(compute, s * sub))


def matmul(a, w, *, n_out, out_dtype, mode="plain", schedule=None, sub_tiles=1, residual=None,
           rope_tabs=None, rope_cols=0, scale=1.0, col0=0, up_col0=0, tm=ROW_TILE, tn=COL_TILE,
           a_packed=False, rows=None, vmem_mib=48, name="matmul"):
    rows = a.shape[0] if rows is None else rows
    k = w.shape[1]
    n_tiles = rows // tm
    n_col = n_out // tn
    if schedule is None:
        schedule = (jnp.zeros((n_tiles,), jnp.int32), jnp.full((n_tiles,), sub_tiles, jnp.int32),
                    jnp.full((1,), n_tiles, jnp.int32), jnp.arange(n_tiles, dtype=jnp.int32))
    cb0 = col0 // tn
    ub0 = up_col0 // tn

    def step_tile(t, nu, ph):
        return ph[jnp.where(t < nu[0], t, 0)]

    def step_weight(j, t, te, nu, c0):
        idle = t >= nu[0]
        return te[jnp.where(idle, 0, t)], 0, c0 + jnp.where(idle, jnp.minimum(j + 1, n_col - 1), j)

    a_spec = pl.BlockSpec((tm, a.shape[1]), lambda j, t, te, nv, nu, ph: (step_tile(t, nu, ph), 0))
    w_spec = pl.BlockSpec((None, k, tn), lambda j, t, te, nv, nu, ph: step_weight(j, t, te, nu, cb0))
    o_spec = pl.BlockSpec((tm, tn), lambda j, t, te, nv, nu, ph: (ph[t], j))
    in_specs = [a_spec, w_spec]
    args = [a, w]
    scratch = [pltpu.VMEM((k, tn), BF16)]
    if mode == "swiglu":
        in_specs.append(pl.BlockSpec((None, k, tn), lambda j, t, te, nv, nu, ph: step_weight(j, t, te, nu, ub0)))
        args.append(w)
        scratch.append(pltpu.VMEM((k, tn), BF16))
    elif mode == "residual":
        in_specs.append(o_spec)
        args.append(residual)
    elif mode == "rope":
        tab_spec = pl.BlockSpec((tm, LANES), lambda j, t, te, nv, nu, ph: (step_tile(t, nu, ph), 0))
        in_specs += [tab_spec, tab_spec]
        args += list(rope_tabs)
    kern = functools.partial(_mm_kernel, mode=mode, a_packed=a_packed, rope_cols=rope_cols, scale=scale,
                             sub_tiles=sub_tiles)
    return pl.pallas_call(
        kern,
        grid_spec=pltpu.PrefetchScalarGridSpec(
            num_scalar_prefetch=4, grid=(n_col, n_tiles),
            in_specs=in_specs, out_specs=o_spec, scratch_shapes=scratch),
        out_shape=jax.ShapeDtypeStruct((rows, n_out), out_dtype),
        compiler_params=_params(("arbitrary", "arbitrary"), vmem_mib),
        name=name,
    )(*schedule, *args)


GLA_ROW_BLOCK = 512


def _gate_kernel(x_ref, wr_ref, wgk_ref, bgk_ref, g_ref):
    r = jnp.dot(x_ref[...], wr_ref[...].astype(BF16), preferred_element_type=F32)
    r_hi = r.astype(BF16)
    r_lo = (r - r_hi.astype(F32)).astype(BF16)
    w = wgk_ref[...]
    w_hi = w.astype(BF16)
    w_lo = (w - w_hi.astype(F32)).astype(BF16)
    z = (jnp.dot(r_hi, w_hi, preferred_element_type=F32) + jnp.dot(r_hi, w_lo, preferred_element_type=F32)
         + jnp.dot(r_lo, w_hi, preferred_element_type=F32) + bgk_ref[...])
    g_ref[...] = (jnp.minimum(z, 0.0) - jnp.log(1.0 + jnp.exp(-jnp.abs(z)))) / GLA_GATE_NORM


def gate_rows(xn, w_r_pad, wgk_pad, bgk):
    rows, d = xn.shape
    width = wgk_pad.shape[1]
    return pl.pallas_call(
        _gate_kernel,
        grid=(rows // ROW_TILE,),
        in_specs=[pl.BlockSpec((ROW_TILE, d), lambda i: (i, 0)),
                  pl.BlockSpec((d, LANES), lambda i: (0, 0)),
                  pl.BlockSpec((LANES, width), lambda i: (0, 0)),
                  pl.BlockSpec((1, width), lambda i: (0, 0))],
        out_specs=pl.BlockSpec((ROW_TILE, width), lambda i: (i, 0)),
        out_shape=jax.ShapeDtypeStruct((rows, width), F32),
        compiler_params=_params(("parallel",), 32),
        name="gate_rows",
    )(xn, w_r_pad, wgk_pad, bgk)


def _gla_kernel(q_ref, k_ref, v_ref, go_ref, g_ref, gn_ref, s0_ref,
                o_ref, sfin_ref, st_ref, *, n_chunks, n_valid):
    rb = pl.program_id(1)

    @pl.when(rb == 0)
    def _():
        for h in range(GLA_HEADS):
            st_ref[h] = s0_ref[h].T

    ri = lax.broadcasted_iota(jnp.int32, (CHUNK, CHUNK), 0)
    ci = lax.broadcasted_iota(jnp.int32, (CHUNK, CHUNK), 1)
    causal = ci <= ri
    tril = jnp.where(causal, 1.0, 0.0).astype(BF16)
    row_in_chunk = lax.broadcasted_iota(jnp.int32, (CHUNK, 1), 0)
    gn = gn_ref[...]

    def body(c, carry):
        rows = pl.ds(pl.multiple_of(c * CHUNK, CHUNK), CHUNK)
        row_valid = (rb * n_chunks + c) * CHUNK + row_in_chunk < n_valid
        g = jnp.where(row_valid, g_ref[rows, :], 0.0)
        g_hi = g.astype(BF16)
        rest = g - g_hi.astype(F32)
        g_mid = rest.astype(BF16)
        g_lo = (rest - g_mid.astype(F32)).astype(BF16)
        b = (jnp.dot(tril, g_hi, preferred_element_type=F32) + jnp.dot(tril, g_mid, preferred_element_type=F32)
             + jnp.dot(tril, g_lo, preferred_element_type=F32))
        b_last = b[CHUNK - 1:CHUNK, :]
        q = q_ref[rows, :].astype(F32) * (GLA_DK ** -0.5)
        k = jnp.where(row_valid, k_ref[rows, :].astype(F32), 0.0)
        q_dec_all = (q * jnp.exp(b)).astype(BF16)
        k_dec_all = (k * jnp.exp(-b)).astype(BF16)
        k_last_all = (k * jnp.exp(b_last - b)).astype(BF16)
        decay = jnp.exp(b_last)
        for h in range(GLA_HEADS):
            ks = slice(h * GLA_DK, (h + 1) * GLA_DK)
            vs = slice(h * GLA_DV, (h + 1) * GLA_DV)
            q_dec = q_dec_all[:, ks]
            v = v_ref[rows, vs]
            att = lax.dot_general(q_dec, k_dec_all[:, ks], (((1,), (1,)), ((), ())),
                                  preferred_element_type=F32)
            att = jnp.where(causal, att, 0.0).astype(BF16)
            st = st_ref[h]
            o = jnp.dot(att, v, preferred_element_type=F32)
            o = o + lax.dot_general(q_dec, st.astype(BF16), (((1,), (1,)), ((), ())),
                                    preferred_element_type=F32)
            st_ref[h] = st * decay[:, ks] + lax.dot_general(
                v, k_last_all[:, ks], (((0,), (0,)), ((), ())), preferred_element_type=F32)
            on = o * lax.rsqrt(jnp.mean(o * o, axis=-1, keepdims=True) + EPS) * gn
            go = go_ref[rows, vs].astype(F32)
            o_ref[rows, vs] = (on * (go * jax.nn.sigmoid(go))).astype(o_ref.dtype)
        return carry

    lax.fori_loop(0, n_chunks, body, 0, unroll=4 if n_chunks % 4 == 0 else 1)

    @pl.when(rb == pl.num_programs(1) - 1)
    def _():
        for h in range(GLA_HEADS):
            sfin_ref[h] = st_ref[h].T


def gla_scan(proj, gates, gnorm, s0, *, n_seq, seq_len, row0, n_valid, s0_per_seq):
    blk = min(seq_len, GLA_ROW_BLOCK)
    n_rb = seq_len // blk
    hk = GLA_HEADS * GLA_DK
    hv = GLA_HEADS * GLA_DV
    rb0 = row0 // blk
    row = lambda b, r: rb0 + b * n_rb + r
    s0_map = (lambda b, r: (b, 0, 0, 0)) if s0_per_seq else (lambda b, r: (0, 0, 0, 0))
    kern = functools.partial(_gla_kernel, n_chunks=blk // CHUNK, n_valid=n_valid)
    return pl.pallas_call(
        kern,
        grid=(n_seq, n_rb),
        in_specs=[
            pl.BlockSpec((blk, hk), lambda b, r: (row(b, r), 0)),
            pl.BlockSpec((blk, hk), lambda b, r: (row(b, r), 1)),
            pl.BlockSpec((blk, hv), lambda b, r: (row(b, r), 2 * hk // hv)),
            pl.BlockSpec((blk, hv), lambda b, r: (row(b, r), 2 * hk // hv + 1)),
            pl.BlockSpec((blk, hk), lambda b, r: (row(b, r), 0)),
            pl.BlockSpec((1, GLA_DV), lambda b, r: (0, 0)),
            pl.BlockSpec((None, GLA_HEADS, GLA_DK, GLA_DV), s0_map),
        ],
        out_specs=[
            pl.BlockSpec((blk, hv), lambda b, r: (b * n_rb + r, 0)),
            pl.BlockSpec((None, GLA_HEADS, GLA_DK, GLA_DV), lambda b, r: (b, 0, 0, 0)),
        ],
        out_shape=[jax.ShapeDtypeStruct((n_seq * seq_len, hv), BF16),
                   jax.ShapeDtypeStruct((n_seq, GLA_HEADS, GLA_DK, GLA_DV), F32)],
        scratch_shapes=[pltpu.VMEM((GLA_HEADS, GLA_DV, GLA_DK), F32)],
        compiler_params=_params(("parallel", "arbitrary"), 48),
        name="gla_scan",
    )(proj, proj, proj, proj, gates, gnorm, s0)


WIN_KEYS = (WINDOW_CHUNKS + 1) * CHUNK
ATTN_KEYS = 2 * LANES
ATTN_ROW_BLOCK = 512
PAIRS = SWA_GROUP // 2


def _lane_halves(x2, head_in_pair):
    lane = lax.broadcasted_iota(jnp.int32, x2.shape, 1)
    low = lane < HEAD_DIM
    swapped = pltpu.roll(x2, HEAD_DIM, 1)
    if head_in_pair == 0:
        lo, hi = jnp.where(low, x2, 0.0), jnp.where(low, 0.0, swapped)
    else:
        lo, hi = jnp.where(low, swapped, 0.0), jnp.where(low, 0.0, x2)
    return lo.astype(BF16), hi.astype(BF16)


def _attn_kernel(q_ref, k_ref, v_ref, km_ref, vm_ref, sink_ref, o_ref, kb_ref, vb_ref, kmb_ref, vmb_ref,
                 *, n_chunks, chunk_offset):
    step = pl.program_id(1)

    @pl.when(step == 0)
    def _():
        for h in range(SWA_KV_HEADS):
            cols = slice((h // 2) * LANES, (h // 2 + 1) * LANES)
            for src, dst in ((k_ref, kb_ref), (v_ref, vb_ref), (km_ref, kmb_ref), (vm_ref, vmb_ref)):
                lo, hi = _lane_halves(src[:, cols], h % 2)
                dst[h, 0] = lo
                dst[h, 1] = hi

    j = lax.broadcasted_iota(jnp.int32, (1, ATTN_KEYS), 1)
    rel_chunk = (j >= N_META + CHUNK).astype(jnp.int32) + (j >= N_META + 2 * CHUNK).astype(jnp.int32)
    in_window = (j >= N_META) & (j < N_META + WIN_KEYS)
    is_sink_slot = j == N_META + WIN_KEYS
    zpad = jnp.zeros((ATTN_KEYS - N_META - WIN_KEYS, LANES), BF16)
    key_row = lax.broadcasted_iota(jnp.int32, (2 * ATTN_KEYS, LANES), 0)
    key_lane = lax.broadcasted_iota(jnp.int32, (2 * ATTN_KEYS, LANES), 1)
    ones_cols = ((key_row < ATTN_KEYS) == (key_lane < HEAD_DIM)).astype(BF16)
    nt = (((1,), (1,)), ((), ()))

    def body(ci, carry):
        c = step * n_chunks + ci + chunk_offset
        wc = jnp.maximum(c - WINDOW_CHUNKS, 0)
        win = pl.ds(pl.multiple_of(wc * CHUNK, CHUNK), WIN_KEYS)
        rows = pl.ds(pl.multiple_of(ci * CHUNK, CHUNK), CHUNK)
        valid = (j < N_META) | (in_window & (wc + rel_chunk <= c))
        for h in range(SWA_KV_HEADS):
            kb = jnp.concatenate([kmb_ref[h, 0], kb_ref[h, 0, win, :], zpad,
                                  kmb_ref[h, 1], kb_ref[h, 1, win, :], zpad], axis=0)
            vb = jnp.concatenate([vmb_ref[h, 0], vb_ref[h, 0, win, :], zpad,
                                  vmb_ref[h, 1], vb_ref[h, 1, win, :], zpad], axis=0)
            col = lambda p: slice((h * PAIRS + p) * LANES, (h * PAIRS + p + 1) * LANES)
            qs = jnp.concatenate([q_ref[rows, col(p)] for p in range(PAIRS)], axis=0)
            s = lax.dot_general(qs, kb, nt, preferred_element_type=F32)
            probs = []
            for half in range(2):
                sink = sink_ref[h, half]
                sh = jnp.where(valid, s[:, half * ATTN_KEYS:(half + 1) * ATTN_KEYS], NEG_INF)
                sh = jnp.where(is_sink_slot, jnp.concatenate([sink] * (ATTN_KEYS // LANES), axis=1), sh)
                m = jnp.max(sh, axis=-1, keepdims=True)
                probs.append(jnp.exp(sh - m).astype(BF16))
            ov = jnp.dot(jnp.concatenate(probs, axis=1), jnp.concatenate([vb, ones_cols], axis=1),
                         preferred_element_type=F32)
            o = ov[:, :LANES] / ov[:, LANES:]
            for p in range(PAIRS):
                o_ref[rows, col(p)] = o[p * CHUNK:(p + 1) * CHUNK].astype(o_ref.dtype)
        return carry

    lax.fori_loop(0, n_chunks, body, 0, unroll=4 if n_chunks % 4 == 0 else 1)


def window_attention(q, k_arr, v_arr, km_arr, vm_arr, sink_tab, *, n_seq, seq_len, key_len, row0,
                     k_map, v_map, km_map, vm_map, chunk_offset):
    blk = min(seq_len, ATTN_ROW_BLOCK)
    steps = seq_len // blk
    q_spec = pl.BlockSpec((blk, q.shape[1]), lambda b, s: (row0 // blk + b * steps + s, 0))
    o_spec = pl.BlockSpec((blk, q.shape[1]), lambda b, s: (b * steps + s, 0))
    width = SWA_KV_HEADS * HEAD_DIM
    kern = functools.partial(_attn_kernel, n_chunks=blk // CHUNK, chunk_offset=chunk_offset)
    return pl.pallas_call(
        kern,
        grid=(n_seq, steps),
        in_specs=[q_spec,
                  pl.BlockSpec((key_len, width), k_map),
                  pl.BlockSpec((key_len, width), v_map),
                  pl.BlockSpec((N_META, width), km_map),
                  pl.BlockSpec((N_META, width), vm_map),
                  pl.BlockSpec((SWA_KV_HEADS, 2, PAIRS * CHUNK, LANES), lambda b, s: (0, 0, 0, 0))],
        out_specs=o_spec,
        out_shape=jax.ShapeDtypeStruct((n_seq * seq_len, q.shape[1]), q.dtype),
        scratch_shapes=[pltpu.VMEM((SWA_KV_HEADS, 2, key_len, LANES), BF16),
                        pltpu.VMEM((SWA_KV_HEADS, 2, key_len, LANES), BF16),
                        pltpu.VMEM((SWA_KV_HEADS, 2, N_META, LANES), BF16),
                        pltpu.VMEM((SWA_KV_HEADS, 2, N_META, LANES), BF16)],
        compiler_params=_params(("parallel", "arbitrary"), 48),
        name="window_attention",
    )(q, k_arr, v_arr, km_arr, vm_arr, sink_tab)


def _route_tile(xn, wr_ref, xp_ref, rt_ref, cnt_ref, carry_ref):
    bits = pltpu.bitcast(xn.astype(BF16).astype(F32), jnp.uint32)
    half = D_MODEL // 2
    xp_ref[...] = (bits[:, :half] >> 16) | (bits[:, half:] & jnp.uint32(0xFFFF0000))

    tm = xn.shape[0]
    hi = xn.astype(BF16)
    lo = (xn - hi.astype(F32)).astype(BF16)
    wide = jnp.dot(hi, wr_ref[...], preferred_element_type=F32)
    logits = (wide[:, :LANES] + wide[:, LANES:]
              + jnp.dot(lo, wr_ref[:, :LANES], preferred_element_type=F32))
    lane = lax.broadcasted_iota(jnp.int32, (tm, LANES), 1).astype(F32)
    neg = jnp.float32(-jnp.inf)
    lg = jnp.where(lane < N_EXPERTS, logits, neg)
    m1 = jnp.max(lg, axis=-1, keepdims=True)
    i1 = jnp.min(jnp.where(lg == m1, lane, float(LANES)), axis=-1, keepdims=True)
    lg2 = jnp.where(lane == i1, neg, lg)
    m2 = jnp.max(lg2, axis=-1, keepdims=True)
    i2 = jnp.min(jnp.where(lg2 == m2, lane, float(LANES)), axis=-1, keepdims=True)
    e21 = jnp.exp(m2 - m1)
    g1 = 1.0 / (1.0 + e21)
    g2 = e21 / (1.0 + e21)
    oh1 = (lane == i1).astype(F32)
    oh2 = (lane == i2).astype(F32)
    oh = oh1 + oh2
    ri = lax.broadcasted_iota(jnp.int32, (tm, tm), 0)
    ci = lax.broadcasted_iota(jnp.int32, (tm, tm), 1)
    before = (ci < ri).astype(BF16)
    prefix = jnp.dot(before, oh.astype(BF16), preferred_element_type=F32) + carry_ref[0:1, :]
    rank1 = jnp.sum(prefix * oh1, axis=-1, keepdims=True)
    rank2 = jnp.sum(prefix * oh2, axis=-1, keepdims=True)
    total = carry_ref[0:1, :] + jnp.sum(oh, axis=0, keepdims=True)
    carry_ref[...] = jnp.broadcast_to(total, carry_ref.shape)
    cnt_ref[...] = jnp.broadcast_to(total, cnt_ref.shape)
    rt = jnp.where(lane == 0, i1, 0.0)
    rt = jnp.where(lane == 1, i2, rt)
    rt = jnp.where(lane == 2, rank1, rt)
    rt = jnp.where(lane == 3, rank2, rt)
    rt = jnp.where(lane == 4, g1, rt)
    rt = jnp.where(lane == 5, g2, rt)
    rt_ref[...] = rt


def _rowmm_kernel(*refs, router, a_bounds, res_bounds):
    na, nr = len(a_bounds), len(res_bounds)
    a_refs, w_ref, res_refs, g_ref = refs[:na], refs[na], refs[na + 1:na + 1 + nr], refs[na + 1 + nr]
    rest = refs[na + 2 + nr:]
    t = pl.program_id(0)
    if router:
        wr_ref, h_ref, xp_ref, rt_ref, cnt_ref, wb_ref, carry_ref, wcat_ref = rest
    else:
        h_ref, xn_ref, wb_ref = rest

    @pl.when(t == 0)
    def _():
        wb_ref[...] = w_ref[...].astype(BF16)
        if router:
            carry_ref[...] = jnp.zeros_like(carry_ref)
            wr = wr_ref[...]
            wr_hi = wr.astype(BF16)
            wcat_ref[:, :LANES] = wr_hi
            wcat_ref[:, LANES:] = (wr - wr_hi.astype(F32)).astype(BF16)

    h = _read_part(res_refs, res_bounds, t) + jnp.dot(_read_part(a_refs, a_bounds, t), wb_ref[...],
                                                     preferred_element_type=F32)
    h_ref[...] = h
    xn = h * lax.rsqrt(jnp.mean(h * h, axis=-1, keepdims=True) + EPS) * g_ref[...]
    if router:
        _route_tile(xn, wcat_ref, xp_ref, rt_ref, cnt_ref, carry_ref)
    else:
        xn_ref[...] = xn.astype(xn_ref.dtype)


def row_matmul(a_parts, w, res_parts, gain, *, rows, tm, w_router_pad=None, name="row_matmul"):
    k, d = w.shape[1], w.shape[2]
    router = w_router_pad is not None
    row_spec = lambda width: pl.BlockSpec((tm, width), lambda t: (t, 0))
    a_specs, a_bounds = _part_specs(a_parts, tm)
    res_specs, res_bounds = _part_specs(res_parts, tm)
    assert min(a_bounds[-1], res_bounds[-1]) * tm >= rows
    in_specs = (a_specs + [pl.BlockSpec((None, k, d), lambda t: (0, 0, 0), pipeline_mode=pl.Buffered(1))]
                + res_specs + [pl.BlockSpec((1, d), lambda t: (0, 0))])
    args = [*a_parts, w, *res_parts, gain]
    out_specs = [row_spec(d)]
    out_shape = [jax.ShapeDtypeStruct((rows, d), F32)]
    scratch = [pltpu.VMEM((k, d), BF16)]
    if router:
        in_specs.append(pl.BlockSpec((d, LANES), lambda t: (0, 0)))
        args.append(w_router_pad)
        out_specs += [row_spec(d // 2), row_spec(LANES), pl.BlockSpec((8, LANES), lambda t: (0, 0))]
        out_shape += [jax.ShapeDtypeStruct((rows, d // 2), jnp.uint32),
                      jax.ShapeDtypeStruct((rows, LANES), F32),
                      jax.ShapeDtypeStruct((8, LANES), F32)]
        scratch += [pltpu.VMEM((8, LANES), F32), pltpu.VMEM((d, 2 * LANES), BF16)]
    else:
        out_specs.append(row_spec(d))
        out_shape.append(jax.ShapeDtypeStruct((rows, d), BF16))
    return pl.pallas_call(
        functools.partial(_rowmm_kernel, router=router, a_bounds=a_bounds, res_bounds=res_bounds),
        grid=(rows // tm,),
        in_specs=in_specs, out_specs=out_specs, out_shape=out_shape, scratch_shapes=scratch,
        compiler_params=_params(("arbitrary",), 56),
        name=name,
    )(*args)


def _scatter_kernel(pos_ref, xp_ref, xs_in_ref, xs_ref, sem):
    del xs_in_ref
    tm = xp_ref.shape[0]

    def row_copy(r, slot):
        return pltpu.make_async_copy(xp_ref.at[pl.ds(r, 1), :], xs_ref.at[pl.ds(slot, 1), :], sem)

    def issue(r, carry):
        row_copy(r, pos_ref[2 * r]).start()
        row_copy(r, pos_ref[2 * r + 1]).start()
        return carry

    lax.fori_loop(0, tm, issue, 0, unroll=8)
    for _ in range(2):
        pltpu.make_async_copy(xp_ref, xs_ref.at[pl.ds(0, tm), :], sem).wait()


def scatter_rows(pos_flat, xp, xs_init):
    rows = xp.shape[0]
    return pl.pallas_call(
        _scatter_kernel,
        grid=(rows // ROW_TILE,),
        in_specs=[pl.BlockSpec((2 * ROW_TILE,), lambda i: (i,), memory_space=pltpu.SMEM),
                  pl.BlockSpec((ROW_TILE, xp.shape[1]), lambda i: (i, 0)),
                  pl.BlockSpec(memory_space=pl.ANY)],
        out_specs=pl.BlockSpec(memory_space=pl.ANY),
        out_shape=jax.ShapeDtypeStruct(xs_init.shape, xs_init.dtype),
        scratch_shapes=[pltpu.SemaphoreType.DMA(())],
        input_output_aliases={2: 0},
        compiler_params=_params(("arbitrary",), 32),
        name="scatter_rows",
    )(pos_flat, xp, xs_init)


def _combine_kernel(pos_ref, rt_ref, h_ref, g_ref, ys_ref, yp_ref, ysm_ref, buf_ref, sem, *, n_prompt_tiles):
    i = pl.program_id(0)
    tm = h_ref.shape[0]

    def row_copy(r, k, slot):
        return pltpu.make_async_copy(ys_ref.at[pl.ds(slot, 1), :], buf_ref.at[k, pl.ds(r, 1), :], sem)

    def issue(r, carry):
        row_copy(r, 0, pos_ref[2 * r]).start()
        row_copy(r, 1, pos_ref[2 * r + 1]).start()
        return carry

    lax.fori_loop(0, tm, issue, 0, unroll=8)
    for k in range(2):
        pltpu.make_async_copy(ys_ref.at[pl.ds(0, tm), :], buf_ref.at[k], sem).wait()
    rt = rt_ref[...]
    y = rt[:, 4:5] * buf_ref[0] + rt[:, 5:6] * buf_ref[1]
    x = h_ref[...] + y
    out = x * lax.rsqrt(jnp.mean(x * x, axis=-1, keepdims=True) + EPS) * g_ref[...]

    @pl.when(i < n_prompt_tiles)
    def _():
        yp_ref[...] = out

    @pl.when(i >= n_prompt_tiles)
    def _():
        ysm_ref[...] = out


def combine_rows(pos_flat, rt, h, gain, ys, *, n_prompt_rows):
    rows = h.shape[0]
    npt = n_prompt_rows // ROW_TILE
    kern = functools.partial(_combine_kernel, n_prompt_tiles=npt)
    return pl.pallas_call(
        kern,
        grid=(rows // ROW_TILE,),
        in_specs=[pl.BlockSpec((2 * ROW_TILE,), lambda i: (i,), memory_space=pltpu.SMEM),
                  pl.BlockSpec((ROW_TILE, LANES), lambda i: (i, 0)),
                  pl.BlockSpec((ROW_TILE, D_MODEL), lambda i: (i, 0)),
                  pl.BlockSpec((1, D_MODEL), lambda i: (0, 0)),
                  pl.BlockSpec(memory_space=pl.ANY)],
        out_specs=[pl.BlockSpec((ROW_TILE, D_MODEL), lambda i: (jnp.minimum(i, npt - 1), 0)),
                   pl.BlockSpec((ROW_TILE, D_MODEL), lambda i: (jnp.maximum(i - npt, 0), 0))],
        out_shape=[jax.ShapeDtypeStruct((n_prompt_rows, D_MODEL), F32),
                   jax.ShapeDtypeStruct((rows - n_prompt_rows, D_MODEL), F32)],
        scratch_shapes=[pltpu.VMEM((2, ROW_TILE, D_MODEL), F32), pltpu.SemaphoreType.DMA(())],
        compiler_params=_params(("arbitrary",), 48),
        name="combine_rows",
    )(pos_flat, rt, h, gain, ys)


def _rope_tables(pos):
    half = HEAD_DIM // 2
    inv_freq = ROPE_THETA ** (-jnp.arange(half, dtype=F32) / half)
    ang = jnp.asarray(pos, jnp.int32).astype(F32)[:, None] * inv_freq[None, :]
    cos = jnp.cos(ang)
    sin = jnp.sin(ang)
    reps = LANES // HEAD_DIM
    return (jnp.tile(jnp.concatenate([cos, cos], axis=1), (1, reps)),
            jnp.tile(jnp.concatenate([-sin, sin], axis=1), (1, reps)))


def kernel(x_prompt, x_sample, state_gla, cache_k_meta, cache_v_meta, cache_k_win, cache_v_win, meta_tokens,
           norm_mix, norm_ffn, norm_kv, norm_final, gla_w_in, gla_w_gk, gla_b_gk, gla_norm, gla_w_out, kv_w,
           attn_w_q, attn_sinks, attn_w_out, ffn_w_gate_up, ffn_w_down, moe_w_router, moe_w_gate_up,
           moe_w_down):
    bsz, seq, d = x_prompt.shape
    dbsz, t_new, _ = x_sample.shape
    n_p = bsz * seq
    n_s = dbsz * t_new
    r1 = n_p + n_s
    assert r1 % ROW_TILE == 0 and seq % CHUNK == 0 and t_new == CHUNK
    r0 = -(-(r1 + CHUNK) // ROW_TILE) * ROW_TILE
    meta_row = r1
    hk = GLA_HEADS * GLA_DK
    hv = GLA_HEADS * GLA_DV
    ffn_dense = ffn_w_down.shape[1]
    ffn_expert = moe_w_down.shape[2]
    kv_width = SWA_KV_HEADS * HEAD_DIM

    x_parts = [x_prompt.reshape(n_p, d),
               jnp.concatenate([x_sample.reshape(n_s, d), meta_tokens.astype(F32),
                                jnp.zeros((r0 - r1 - N_META, d), F32)], axis=0)]
    tail_rows = r0 - r1

    tm0 = r0 // 16
    tm1 = r1 // 16
    (xn0,) = rms_rows(x_parts, norm_mix[0:1])
    proj = matmul(xn0, gla_w_in, n_out=2 * hk + 2 * hv, out_dtype=BF16, tm=tm0, tn=2 * COL_TILE,
                  name="gla_in_proj")
    w_r_pad = jnp.pad(gla_w_in[0, :, 2 * hk + 2 * hv:], ((0, 0), (0, LANES - GLA_RANK)))
    wgk_pad = jnp.pad(gla_w_gk[0], ((0, LANES - GLA_RANK), (0, 0)))
    gnorm = gla_norm[0][None, :]
    gla = functools.partial(gla_scan, proj, gate_rows(xn0, w_r_pad, wgk_pad, gla_b_gk[0][None, :]), gnorm)
    og_meta, s_meta = gla(jnp.zeros((1, GLA_HEADS, GLA_DK, GLA_DV), F32), n_seq=1, seq_len=tail_rows,
                          row0=meta_row, n_valid=N_META, s0_per_seq=False)
    og_prompt, s_prompt = gla(s_meta, n_seq=bsz, seq_len=seq, row0=0, n_valid=seq, s0_per_seq=False)
    og_sample, s_sample = gla(state_gla[0].astype(F32), n_seq=dbsz, seq_len=t_new, row0=n_p,
                              n_valid=t_new, s0_per_seq=True)
    h1, hn1 = row_matmul([og_prompt, jnp.concatenate([og_sample, og_meta], axis=0)], gla_w_out, x_parts,
                         norm_ffn[0:1], rows=r0, tm=r0 // 68, name="gla_out_proj")

    act = matmul(hn1, ffn_w_gate_up, n_out=ffn_dense, out_dtype=BF16, mode="swiglu", up_col0=ffn_dense,
                 tm=tm0, name="ffn_gate_up")
    h2 = matmul(act, ffn_w_down, n_out=d, out_dtype=F32, mode="residual", residual=h1, tm=tm0 // 2,
                vmem_mib=56, name="ffn_down")

    pos = np.concatenate([np.tile(N_META + np.arange(seq), bsz),
                          np.tile(N_META + PAST_LEN + np.arange(t_new), dbsz),
                          np.arange(N_META), np.zeros(r0 - r1 - N_META, np.int64)])
    rope_tabs = _rope_tables(pos)
    xkv, xq = rms_rows([h2], jnp.stack([norm_kv, norm_mix[1]]))
    kvf = matmul(xkv, kv_w[None], n_out=2 * kv_width, out_dtype=F32, mode="rope", rope_tabs=rope_tabs,
                 rope_cols=kv_width, tm=tm0, name="shared_kv")
    q = matmul(xq, attn_w_q, n_out=d, out_dtype=BF16, mode="rope", rope_tabs=rope_tabs, rope_cols=d,
               scale=HEAD_DIM ** -0.5, rows=r1, tm=tm1, tn=2 * COL_TILE, name="attn_q")

    sink_tab = jnp.broadcast_to(
        jnp.repeat(attn_sinks[0].astype(F32).reshape(SWA_KV_HEADS, PAIRS, 2).transpose(0, 2, 1), CHUNK, axis=2)
        [..., None], (SWA_KV_HEADS, 2, PAIRS * CHUNK, LANES))
    meta_blk = meta_row // N_META
    o_prompt = window_attention(
        q, kvf, kvf, kvf, kvf, sink_tab, n_seq=bsz, seq_len=seq, key_len=seq, row0=0,
        k_map=lambda b, s: (b, 0), v_map=lambda b, s: (b, 1),
        km_map=lambda b, s: (meta_blk, 0), vm_map=lambda b, s: (meta_blk, 1), chunk_offset=0)
    k_new = kvf[n_p:r1, :kv_width]
    v_new = kvf[n_p:r1, kv_width:]
    win = cache_k_win.shape[1]
    ks = jnp.concatenate([cache_k_win.reshape(dbsz, win, kv_width).astype(F32),
                          k_new.reshape(dbsz, t_new, kv_width)], axis=1).reshape(dbsz * (win + t_new), kv_width)
    vs = jnp.concatenate([cache_v_win.reshape(dbsz, win, kv_width).astype(F32),
                          v_new.reshape(dbsz, t_new, kv_width)], axis=1).reshape(dbsz * (win + t_new), kv_width)
    assert win + t_new == (WINDOW_CHUNKS + 1) * CHUNK
    o_sample = window_attention(
        q, ks, vs, cache_k_meta.reshape(dbsz * N_META, kv_width).astype(F32),
        cache_v_meta.reshape(dbsz * N_META, kv_width).astype(F32), sink_tab,
        n_seq=dbsz, seq_len=t_new, key_len=win + t_new, row0=n_p,
        k_map=lambda b, s: (b, 0), v_map=lambda b, s: (b, 0),
        km_map=lambda b, s: (b, 0), vm_map=lambda b, s: (b, 0), chunk_offset=WINDOW_CHUNKS)
    w_router_pad = jnp.pad(moe_w_router[0], ((0, 0), (0, LANES - N_EXPERTS)))
    h3, xp, rt, cnt = row_matmul([o_prompt, o_sample], attn_w_out, [h2], norm_ffn[1:2], rows=r1, tm=r1 // 66,
                                 w_router_pad=w_router_pad, name="attn_out_route")
    unit, per_tile = MOE_UNIT, MOE_TILE // MOE_UNIT
    n_tiles_max = 2 * r1 // MOE_TILE + N_EXPERTS
    counts = cnt[0, :N_EXPERTS].astype(jnp.int32)
    units_per = (counts + unit - 1) // unit
    tiles_per = (units_per + per_tile - 1) // per_tile
    tile_end = jnp.cumsum(tiles_per)
    tile_start = tile_end - tiles_per
    experts = rt[:, 0:2].astype(jnp.int32)
    ranks = rt[:, 2:4].astype(jnp.int32)
    pos_flat = (tile_start[experts] * MOE_TILE + ranks).reshape(-1)
    n_used = tile_end[-1:].astype(jnp.int32)
    tile_id = jnp.arange(n_tiles_max, dtype=jnp.int32)
    tile_expert = jnp.minimum(jnp.sum(tile_id[:, None] >= tile_end[None, :], axis=1), N_EXPERTS - 1).astype(jnp.int32)
    tile_units = jnp.clip(units_per[tile_expert] - per_tile * (tile_id - tile_start[tile_expert]), 0, per_tile)
    tile_units = jnp.where(tile_id < n_used[0], tile_units, 0).astype(jnp.int32)
    xs = scatter_rows(pos_flat, xp, jnp.zeros((n_tiles_max * MOE_TILE, d // 2), jnp.uint32))

    def short_tile_first(step, start, count, has_short, n_steps_used):
        local = step - start
        rotated = start + jnp.where(local == 0, count - 1, local - 1)
        return jnp.where(has_short & (count >= 2) & (step < n_steps_used), rotated, step).astype(jnp.int32)

    short_last = (units_per % per_tile) != 0
    tile_order = short_tile_first(tile_id, tile_start[tile_expert], tiles_per[tile_expert],
                                  short_last[tile_expert], n_used[0])
    act_e = matmul(xs, moe_w_gate_up.reshape(N_EXPERTS, d, 2 * ffn_expert), n_out=ffn_expert, out_dtype=BF16,
                   mode="swiglu", up_col0=ffn_expert,
                   schedule=(tile_expert, tile_units[tile_order], n_used, tile_order),
                   sub_tiles=per_tile, tm=MOE_TILE, a_packed=True, vmem_mib=56, name="moe_gate_up")
    split = MOE_TILE // MOE_DOWN_TILE
    down_id = jnp.arange(n_tiles_max * split, dtype=jnp.int32)
    down_expert = jnp.repeat(tile_expert, split)
    down_valid = ((down_id % split) * MOE_DOWN_TILE // unit < jnp.repeat(tile_units, split)).astype(jnp.int32)
    down_order = short_tile_first(down_id, split * tile_start[down_expert], split * tiles_per[down_expert],
                                  short_last[down_expert], split * n_used[0])
    ys = matmul(act_e, moe_w_down.reshape(N_EXPERTS, ffn_expert, d), n_out=d, out_dtype=F32,
                schedule=(down_expert, down_valid[down_order], split * n_used, down_order), tm=MOE_DOWN_TILE,
                vmem_mib=56, name="moe_down")
    y_prompt, y_sample = combine_rows(pos_flat, rt, h3, norm_final[None, :], ys, n_prompt_rows=n_p)

    kv_meta = kvf[meta_row:meta_row + N_META]
    k_meta_p = jnp.broadcast_to(kv_meta[None, :, :kv_width], (bsz, N_META, kv_width))
    v_meta_p = jnp.broadcast_to(kv_meta[None, :, kv_width:], (bsz, N_META, kv_width))
    win_p = min(WINDOW_CHUNKS * CHUNK, seq)
    kv_win = kvf[:n_p].reshape(bsz, seq, 2 * kv_width)[:, seq - win_p:]
    shape4 = lambda a: a.reshape(a.shape[0], a.shape[1], SWA_KV_HEADS, HEAD_DIM)
    return (y_prompt.reshape(bsz, seq, d), y_sample.reshape(dbsz, t_new, d),
            s_prompt[None].astype(state_gla.dtype), s_sample[None].astype(state_gla.dtype),
            shape4(k_meta_p), shape4(v_meta_p),
            shape4(kv_win[:, :, :kv_width]), shape4(kv_win[:, :, kv_width:]),
            shape4(k_new.reshape(dbsz, t_new, kv_width)), shape4(v_new.reshape(dbsz, t_new, kv_width)))
```

```python
import functools

import numpy as np
import jax
import jax.numpy as jnp
from jax import lax
from jax.experimental import pallas as pl
from jax.experimental.pallas import tpu as pltpu

F32 = jnp.float32
BF16 = jnp.bfloat16

D_MODEL = 2048
PAST_LEN = 2048
CHUNK = 64
N_META = 16
GLA_HEADS = 4
GLA_DK = 256
GLA_DV = 512
GLA_RANK = 16
GLA_GATE_NORM = 16.0
HEAD_DIM = 64
SWA_KV_HEADS = 4
SWA_GROUP = 8
WINDOW_CHUNKS = 2
ROPE_THETA = 10000.0
N_EXPERTS = 8
EPS = 1e-5
NEG_INF = -1e30

LANES = 128
ROW_TILE = 512
COL_TILE = 512
MOE_TILE = 1024
MOE_UNIT = 512
MOE_DOWN_TILE = 512
MIB = 2 ** 20


def _params(semantics, vmem_mib):
    return pltpu.CompilerParams(dimension_semantics=semantics, vmem_limit_bytes=vmem_mib * MIB)


def _part_specs(parts, tm):
    specs, bounds, start = [], [], 0
    for p in parts:
        n = p.shape[0] // tm
        assert n * tm == p.shape[0]
        specs.append(pl.BlockSpec((tm, p.shape[1]), lambda t, s=start, n=n: (jnp.clip(t - s, 0, n - 1), 0)))
        start += n
        bounds.append(start)
    return specs, tuple(bounds)


def _read_part(refs, bounds, t):
    x = refs[0][...]
    for ref, lo in zip(refs[1:], bounds[:-1]):
        x = jnp.where(t >= lo, ref[...], x)
    return x


def _norm_kernel(*refs, bounds):
    n_src = len(bounds)
    g_ref = refs[n_src]
    x = _read_part(refs[:n_src], bounds, pl.program_id(0))
    y = x * lax.rsqrt(jnp.mean(x * x, axis=-1, keepdims=True) + EPS)
    for i, o_ref in enumerate(refs[n_src + 1:]):
        o_ref[...] = (y * g_ref[i:i + 1, :]).astype(o_ref.dtype)


def rms_rows(parts, gains):
    d = parts[0].shape[1]
    n = gains.shape[0]
    specs, bounds = _part_specs(parts, ROW_TILE)
    rows = bounds[-1] * ROW_TILE
    return pl.pallas_call(
        functools.partial(_norm_kernel, bounds=bounds),
        grid=(bounds[-1],),
        in_specs=specs + [pl.BlockSpec((n, d), lambda i: (0, 0))],
        out_specs=[pl.BlockSpec((ROW_TILE, d), lambda i: (i, 0)) for _ in range(n)],
        out_shape=[jax.ShapeDtypeStruct((rows, d), BF16) for _ in range(n)],
        compiler_params=_params(("parallel",), 40),
        name="rms_rows",
    )(*parts, gains)


def _swap_halves(x):
    lane = lax.broadcasted_iota(jnp.int32, x.shape, 1)
    first_half = (lane % HEAD_DIM) < (HEAD_DIM // 2)
    return jnp.where(first_half, pltpu.roll(x, LANES - HEAD_DIM // 2, 1), pltpu.roll(x, HEAD_DIM // 2, 1))


def _mm_kernel(te_ref, nv_ref, nu_ref, ph_ref, *refs, mode, a_packed, rope_cols, scale, sub_tiles):
    if mode == "swiglu":
        a_ref, w_ref, w2_ref, o_ref, wb_ref, wb2_ref = refs
    elif mode == "residual":
        a_ref, w_ref, res_ref, o_ref, wb_ref = refs
    elif mode == "rope":
        a_ref, w_ref, cos_ref, sin_ref, o_ref, wb_ref = refs
    else:
        a_ref, w_ref, o_ref, wb_ref = refs
    t = pl.program_id(1)
    tm = o_ref.shape[0]
    sub = tm // sub_tiles

    def compute(n_rows):
        rows = slice(0, n_rows)
        a = a_ref[rows, :]
        if a_packed:
            lo = pltpu.bitcast(a << 16, F32)
            hi = pltpu.bitcast(a & jnp.uint32(0xFFFF0000), F32)
            a = jnp.concatenate([lo, hi], axis=1)
        a = a.astype(BF16)
        acc = jnp.dot(a, wb_ref[...], preferred_element_type=F32)
        if mode == "swiglu":
            up = jnp.dot(a, wb2_ref[...], preferred_element_type=F32)
            o_ref[rows, :] = (acc * jax.nn.sigmoid(acc) * up).astype(o_ref.dtype)
        elif mode == "residual":
            o_ref[rows, :] = (res_ref[rows, :] + acc).astype(o_ref.dtype)
        elif mode == "rope":
            cos = cos_ref[rows, :]
            sin = sin_ref[rows, :]
            for c in range(acc.shape[1] // LANES):
                x = acc[:, c * LANES:(c + 1) * LANES]
                if c * LANES < rope_cols:
                    x = x * cos + _swap_halves(x) * sin
                o_ref[rows, c * LANES:(c + 1) * LANES] = (x * scale).astype(o_ref.dtype)
        else:
            o_ref[rows, :] = acc.astype(o_ref.dtype)
        if n_rows < tm:
            o_ref[n_rows:, :] = jnp.zeros((tm - n_rows, o_ref.shape[1]), o_ref.dtype)

    @pl.when(t < nu_ref[0])
    def _():
        @pl.when((t == 0) | (te_ref[t] != te_ref[jnp.maximum(t - 1, 0)]))
        def _():
            wb_ref[...] = w_ref[...].astype(BF16)
            if mode == "swiglu":
                wb2_ref[...] = w2_ref[...].astype(BF16)

    n_valid = jnp.where(t < nu_ref[0], nv_ref[t], 0)

    @pl.when(n_valid == 0)
    def _():
        o_ref[...] = jnp.zeros_like(o_ref)

    for s in range(1, sub_tiles + 1):
        pl.when(n_valid == s)(functools.partial(compute, s * sub))


def matmul(a, w, *, n_out, out_dtype, mode="plain", schedule=None, sub_tiles=1, residual=None,
           rope_tabs=None, rope_cols=0, scale=1.0, col0=0, up_col0=0, tm=ROW_TILE, tn=COL_TILE,
           a_packed=False, rows=None, vmem_mib=48, name="matmul"):
    rows = a.shape[0] if rows is None else rows
    k = w.shape[1]
    n_tiles = rows // tm
    n_col = n_out // tn
    if schedule is None:
        schedule = (jnp.zeros((n_tiles,), jnp.int32), jnp.full((n_tiles,), sub_tiles, jnp.int32),
                    jnp.full((1,), n_tiles, jnp.int32), jnp.arange(n_tiles, dtype=jnp.int32))
    cb0 = col0 // tn
    ub0 = up_col0 // tn

    def step_tile(t, nu, ph):
        return ph[jnp.where(t < nu[0], t, 0)]

    def step_weight(j, t, te, nu, c0):
        idle = t >= nu[0]
        return te[jnp.where(idle, 0, t)], 0, c0 + jnp.where(idle, jnp.minimum(j + 1, n_col - 1), j)

    a_spec = pl.BlockSpec((tm, a.shape[1]), lambda j, t, te, nv, nu, ph: (step_tile(t, nu, ph), 0))
    w_spec = pl.BlockSpec((None, k, tn), lambda j, t, te, nv, nu, ph: step_weight(j, t, te, nu, cb0))
    o_spec = pl.BlockSpec((tm, tn), lambda j, t, te, nv, nu, ph: (ph[t], j))
    in_specs = [a_spec, w_spec]
    args = [a, w]
    scratch = [pltpu.VMEM((k, tn), BF16)]
    if mode == "swiglu":
        in_specs.append(pl.BlockSpec((None, k, tn), lambda j, t, te, nv, nu, ph: step_weight(j, t, te, nu, ub0)))
        args.append(w)
        scratch.append(pltpu.VMEM((k, tn), BF16))
    elif mode == "residual":
        in_specs.append(o_spec)
        args.append(residual)
    elif mode == "rope":
        tab_spec = pl.BlockSpec((tm, LANES), lambda j, t, te, nv, nu, ph: (step_tile(t, nu, ph), 0))
        in_specs += [tab_spec, tab_spec]
        args += list(rope_tabs)
    kern = functools.partial(_mm_kernel, mode=mode, a_packed=a_packed, rope_cols=rope_cols, scale=scale,
                             sub_tiles=sub_tiles)
    return pl.pallas_call(
        kern,
        grid_spec=pltpu.PrefetchScalarGridSpec(
            num_scalar_prefetch=4, grid=(n_col, n_tiles),
            in_specs=in_specs, out_specs=o_spec, scratch_shapes=scratch),
        out_shape=jax.ShapeDtypeStruct((rows, n_out), out_dtype),
        compiler_params=_params(("arbitrary", "arbitrary"), vmem_mib),
        name=name,
    )(*schedule, *args)


GLA_ROW_BLOCK = 512


def _gate_kernel(x_ref, wr_ref, wgk_ref, bgk_ref, g_ref):
    r = jnp.dot(x_ref[...], wr_ref[...].astype(BF16), preferred_element_type=F32)
    r_hi = r.astype(BF16)
    r_lo = (r - r_hi.astype(F32)).astype(BF16)
    w = wgk_ref[...]
    w_hi = w.astype(BF16)
    w_lo = (w - w_hi.astype(F32)).astype(BF16)
    z = (jnp.dot(r_hi, w_hi, preferred_element_type=F32) + jnp.dot(r_hi, w_lo, preferred_element_type=F32)
         + jnp.dot(r_lo, w_hi, preferred_element_type=F32) + bgk_ref[...])
    g_ref[...] = (jnp.minimum(z, 0.0) - jnp.log(1.0 + jnp.exp(-jnp.abs(z)))) / GLA_GATE_NORM


def gate_rows(xn, w_r_pad, wgk_pad, bgk):
    rows, d = xn.shape
    width = wgk_pad.shape[1]
    return pl.pallas_call(
        _gate_kernel,
        grid=(rows // ROW_TILE,),
        in_specs=[pl.BlockSpec((ROW_TILE, d), lambda i: (i, 0)),
                  pl.BlockSpec((d, LANES), lambda i: (0, 0)),
                  pl.BlockSpec((LANES, width), lambda i: (0, 0)),
                  pl.BlockSpec((1, width), lambda i: (0, 0))],
        out_specs=pl.BlockSpec((ROW_TILE, width), lambda i: (i, 0)),
        out_shape=jax.ShapeDtypeStruct((rows, width), F32),
        compiler_params=_params(("parallel",), 32),
        name="gate_rows",
    )(xn, w_r_pad, wgk_pad, bgk)


def _gla_kernel(q_ref, k_ref, v_ref, go_ref, g_ref, gn_ref, s0_ref,
                o_ref, sfin_ref, st_ref, *, n_chunks, n_valid):
    rb = pl.program_id(1)

    @pl.when(rb == 0)
    def _():
        for h in range(GLA_HEADS):
            st_ref[h] = s0_ref[h].T

    ri = lax.broadcasted_iota(jnp.int32, (CHUNK, CHUNK), 0)
    ci = lax.broadcasted_iota(jnp.int32, (CHUNK, CHUNK), 1)
    causal = ci <= ri
    tril = jnp.where(causal, 1.0, 0.0).astype(BF16)
    row_in_chunk = lax.broadcasted_iota(jnp.int32, (CHUNK, 1), 0)
    gn = gn_ref[...]

    def body(c, carry):
        rows = pl.ds(pl.multiple_of(c * CHUNK, CHUNK), CHUNK)
        row_valid = (rb * n_chunks + c) * CHUNK + row_in_chunk < n_valid
        g = jnp.where(row_valid, g_ref[rows, :], 0.0)
        g_hi = g.astype(BF16)
        rest = g - g_hi.astype(F32)
        g_mid = rest.astype(BF16)
        g_lo = (rest - g_mid.astype(F32)).astype(BF16)
        b = (jnp.dot(tril, g_hi, preferred_element_type=F32) + jnp.dot(tril, g_mid, preferred_element_type=F32)
             + jnp.dot(tril, g_lo, preferred_element_type=F32))
        b_last = b[CHUNK - 1:CHUNK, :]
        q = q_ref[rows, :].astype(F32) * (GLA_DK ** -0.5)
        k = jnp.where(row_valid, k_ref[rows, :].astype(F32), 0.0)
        q_dec_all = (q * jnp.exp(b)).astype(BF16)
        k_dec_all = (k * jnp.exp(-b)).astype(BF16)
        k_last_all = (k * jnp.exp(b_last - b)).astype(BF16)
        decay = jnp.exp(b_last)
        for h in range(GLA_HEADS):
            ks = slice(h * GLA_DK, (h + 1) * GLA_DK)
            vs = slice(h * GLA_DV, (h + 1) * GLA_DV)
            q_dec = q_dec_all[:, ks]
            v = v_ref[rows, vs]
            att = lax.dot_general(q_dec, k_dec_all[:, ks], (((1,), (1,)), ((), ())),
                                  preferred_element_type=F32)
            att = jnp.where(causal, att, 0.0).astype(BF16)
            st = st_ref[h]
            o = jnp.dot(att, v, preferred_element_type=F32)
            o = o + lax.dot_general(q_dec, st.astype(BF16), (((1,), (1,)), ((), ())),
                                    preferred_element_type=F32)
            st_ref[h] = st * decay[:, ks] + lax.dot_general(
                v, k_last_all[:, ks], (((0,), (0,)), ((), ())), preferred_element_type=F32)
            on = o * lax.rsqrt(jnp.mean(o * o, axis=-1, keepdims=True) + EPS) * gn
            go = go_ref[rows, vs].astype(F32)
            o_ref[rows, vs] = (on * (go * jax.nn.sigmoid(go))).astype(o_ref.dtype)
        return carry

    lax.fori_loop(0, n_chunks, body, 0, unroll=4 if n_chunks % 4 == 0 else 1)

    @pl.when(rb == pl.num_programs(1) - 1)
    def _():
        for h in range(GLA_HEADS):
            sfin_ref[h] = st_ref[h].T


def gla_scan(proj, gates, gnorm, s0, *, n_seq, seq_len, row0, n_valid, s0_per_seq):
    blk = min(seq_len, GLA_ROW_BLOCK)
    n_rb = seq_len // blk
    hk = GLA_HEADS * GLA_DK
    hv = GLA_HEADS * GLA_DV
    rb0 = row0 // blk
    row = lambda b, r: rb0 + b * n_rb + r
    s0_map = (lambda b, r: (b, 0, 0, 0)) if s0_per_seq else (lambda b, r: (0, 0, 0, 0))
    kern = functools.partial(_gla_kernel, n_chunks=blk // CHUNK, n_valid=n_valid)
    return pl.pallas_call(
        kern,
        grid=(n_seq, n_rb),
        in_specs=[
            pl.BlockSpec((blk, hk), lambda b, r: (row(b, r), 0)),
            pl.BlockSpec((blk, hk), lambda b, r: (row(b, r), 1)),
            pl.BlockSpec((blk, hv), lambda b, r: (row(b, r), 2 * hk // hv)),
            pl.BlockSpec((blk, hv), lambda b, r: (row(b, r), 2 * hk // hv + 1)),
            pl.BlockSpec((blk, hk), lambda b, r: (row(b, r), 0)),
            pl.BlockSpec((1, GLA_DV), lambda b, r: (0, 0)),
            pl.BlockSpec((None, GLA_HEADS, GLA_DK, GLA_DV), s0_map),
        ],
        out_specs=[
            pl.BlockSpec((blk, hv), lambda b, r: (b * n_rb + r, 0)),
            pl.BlockSpec((None, GLA_HEADS, GLA_DK, GLA_DV), lambda b, r: (b, 0, 0, 0)),
        ],
        out_shape=[jax.ShapeDtypeStruct((n_seq * seq_len, hv), BF16),
                   jax.ShapeDtypeStruct((n_seq, GLA_HEADS, GLA_DK, GLA_DV), F32)],
        scratch_shapes=[pltpu.VMEM((GLA_HEADS, GLA_DV, GLA_DK), F32)],
        compiler_params=_params(("parallel", "arbitrary"), 48),
        name="gla_scan",
    )(proj, proj, proj, proj, gates, gnorm, s0)


WIN_KEYS = (WINDOW_CHUNKS + 1) * CHUNK
ATTN_KEYS = 2 * LANES
ATTN_ROW_BLOCK = 512
PAIRS = SWA_GROUP // 2


def _lane_halves(x2, head_in_pair):
    lane = lax.broadcasted_iota(jnp.int32, x2.shape, 1)
    low = lane < HEAD_DIM
    swapped = pltpu.roll(x2, HEAD_DIM, 1)
    if head_in_pair == 0:
        lo, hi = jnp.where(low, x2, 0.0), jnp.where(low, 0.0, swapped)
    else:
        lo, hi = jnp.where(low, swapped, 0.0), jnp.where(low, 0.0, x2)
    return lo.astype(BF16), hi.astype(BF16)


def _attn_kernel(q_ref, k_ref, v_ref, km_ref, vm_ref, sink_ref, o_ref, kb_ref, vb_ref, kmb_ref, vmb_ref,
                 *, n_chunks, chunk_offset):
    step = pl.program_id(1)

    @pl.when(step == 0)
    def _():
        for h in range(SWA_KV_HEADS):
            cols = slice((h // 2) * LANES, (h // 2 + 1) * LANES)
            for src, dst in ((k_ref, kb_ref), (v_ref, vb_ref), (km_ref, kmb_ref), (vm_ref, vmb_ref)):
                lo, hi = _lane_halves(src[:, cols], h % 2)
                dst[h, 0] = lo
                dst[h, 1] = hi

    j = lax.broadcasted_iota(jnp.int32, (1, ATTN_KEYS), 1)
    rel_chunk = (j >= N_META + CHUNK).astype(jnp.int32) + (j >= N_META + 2 * CHUNK).astype(jnp.int32)
    in_window = (j >= N_META) & (j < N_META + WIN_KEYS)
    is_sink_slot = j == N_META + WIN_KEYS
    zpad = jnp.zeros((ATTN_KEYS - N_META - WIN_KEYS, LANES), BF16)
    key_row = lax.broadcasted_iota(jnp.int32, (2 * ATTN_KEYS, LANES), 0)
    key_lane = lax.broadcasted_iota(jnp.int32, (2 * ATTN_KEYS, LANES), 1)
    ones_cols = ((key_row < ATTN_KEYS) == (key_lane < HEAD_DIM)).astype(BF16)
    nt = (((1,), (1,)), ((), ()))

    def body(ci, carry):
        c = step * n_chunks + ci + chunk_offset
        wc = jnp.maximum(c - WINDOW_CHUNKS, 0)
        win = pl.ds(pl.multiple_of(wc * CHUNK, CHUNK), WIN_KEYS)
        rows = pl.ds(pl.multiple_of(ci * CHUNK, CHUNK), CHUNK)
        valid = (j < N_META) | (in_window & (wc + rel_chunk <= c))
        for h in range(SWA_KV_HEADS):
            kb = jnp.concatenate([kmb_ref[h, 0], kb_ref[h, 0, win, :], zpad,
                                  kmb_ref[h, 1], kb_ref[h, 1, win, :], zpad], axis=0)
            vb = jnp.concatenate([vmb_ref[h, 0], vb_ref[h, 0, win, :], zpad,
                                  vmb_ref[h, 1], vb_ref[h, 1, win, :], zpad], axis=0)
            col = lambda p: slice((h * PAIRS + p) * LANES, (h * PAIRS + p + 1) * LANES)
            qs = jnp.concatenate([q_ref[rows, col(p)] for p in range(PAIRS)], axis=0)
            s = lax.dot_general(qs, kb, nt, preferred_element_type=F32)
            probs = []
            for half in range(2):
                sink = sink_ref[h, half]
                sh = jnp.where(valid, s[:, half * ATTN_KEYS:(half + 1) * ATTN_KEYS], NEG_INF)
                sh = jnp.where(is_sink_slot, jnp.concatenate([sink] * (ATTN_KEYS // LANES), axis=1), sh)
                m = jnp.max(sh, axis=-1, keepdims=True)
                probs.append(jnp.exp(sh - m).astype(BF16))
            ov = jnp.dot(jnp.concatenate(probs, axis=1), jnp.concatenate([vb, ones_cols], axis=1),
                         preferred_element_type=F32)
            o = ov[:, :LANES] / ov[:, LANES:]
            for p in range(PAIRS):
                o_ref[rows, col(p)] = o[p * CHUNK:(p + 1) * CHUNK].astype(o_ref.dtype)
        return carry

    lax.fori_loop(0, n_chunks, body, 0, unroll=4 if n_chunks % 4 == 0 else 1)


def window_attention(q, k_arr, v_arr, km_arr, vm_arr, sink_tab, *, n_seq, seq_len, key_len, row0,
                     k_map, v_map, km_map, vm_map, chunk_offset):
    blk = min(seq_len, ATTN_ROW_BLOCK)
    steps = seq_len // blk
    q_spec = pl.BlockSpec((blk, q.shape[1]), lambda b, s: (row0 // blk + b * steps + s, 0))
    o_spec = pl.BlockSpec((blk, q.shape[1]), lambda b, s: (b * steps + s, 0))
    width = SWA_KV_HEADS * HEAD_DIM
    kern = functools.partial(_attn_kernel, n_chunks=blk // CHUNK, chunk_offset=chunk_offset)
    return pl.pallas_call(
        kern,
        grid=(n_seq, steps),
        in_specs=[q_spec,
                  pl.BlockSpec((key_len, width), k_map),
                  pl.BlockSpec((key_len, width), v_map),
                  pl.BlockSpec((N_META, width), km_map),
                  pl.BlockSpec((N_META, width), vm_map),
                  pl.BlockSpec((SWA_KV_HEADS, 2, PAIRS * CHUNK, LANES), lambda b, s: (0, 0, 0, 0))],
        out_specs=o_spec,
        out_shape=jax.ShapeDtypeStruct((n_seq * seq_len, q.shape[1]), q.dtype),
        scratch_shapes=[pltpu.VMEM((SWA_KV_HEADS, 2, key_len, LANES), BF16),
                        pltpu.VMEM((SWA_KV_HEADS, 2, key_len, LANES), BF16),
                        pltpu.VMEM((SWA_KV_HEADS, 2, N_META, LANES), BF16),
                        pltpu.VMEM((SWA_KV_HEADS, 2, N_META, LANES), BF16)],
        compiler_params=_params(("parallel", "arbitrary"), 48),
        name="window_attention",
    )(q, k_arr, v_arr, km_arr, vm_arr, sink_tab)


def _route_tile(xn, wr_ref, xp_ref, rt_ref, cnt_ref, carry_ref):
    xp_ref[...] = xn

    tm = xn.shape[0]
    hi = xn.astype(BF16)
    lo = (xn - hi.astype(F32)).astype(BF16)
    wide = jnp.dot(hi, wr_ref[...], preferred_element_type=F32)
    logits = (wide[:, :LANES] + wide[:, LANES:]
              + jnp.dot(lo, wr_ref[:, :LANES], preferred_element_type=F32))
    lane = lax.broadcasted_iota(jnp.int32, (tm, LANES), 1).astype(F32)
    neg = jnp.float32(-jnp.inf)
    lg = jnp.where(lane < N_EXPERTS, logits, neg)
    m1 = jnp.max(lg, axis=-1, keepdims=True)
    i1 = jnp.min(jnp.where(lg == m1, lane, float(LANES)), axis=-1, keepdims=True)
    lg2 = jnp.where(lane == i1, neg, lg)
    m2 = jnp.max(lg2, axis=-1, keepdims=True)
    i2 = jnp.min(jnp.where(lg2 == m2, lane, float(LANES)), axis=-1, keepdims=True)
    e21 = jnp.exp(m2 - m1)
    g1 = 1.0 / (1.0 + e21)
    g2 = e21 / (1.0 + e21)
    oh1 = (lane == i1).astype(F32)
    oh2 = (lane == i2).astype(F32)
    oh = oh1 + oh2
    ri = lax.broadcasted_iota(jnp.int32, (tm, tm), 0)
    ci = lax.broadcasted_iota(jnp.int32, (tm, tm), 1)
    before = (ci < ri).astype(BF16)
    prefix = jnp.dot(before, oh.astype(BF16), preferred_element_type=F32) + carry_ref[0:1, :]
    rank1 = jnp.sum(prefix * oh1, axis=-1, keepdims=True)
    rank2 = jnp.sum(prefix * oh2, axis=-1, keepdims=True)
    total = carry_ref[0:1, :] + jnp.sum(oh, axis=0, keepdims=True)
    carry_ref[...] = jnp.broadcast_to(total, carry_ref.shape)
    cnt_ref[...] = jnp.broadcast_to(total, cnt_ref.shape)
    rt = jnp.where(lane == 0, i1, 0.0)
    rt = jnp.where(lane == 1, i2, rt)
    rt = jnp.where(lane == 2, rank1, rt)
    rt = jnp.where(lane == 3, rank2, rt)
    rt = jnp.where(lane == 4, g1, rt)
    rt = jnp.where(lane == 5, g2, rt)
    rt_ref[...] = rt


def _rowmm_kernel(*refs, router, a_bounds, res_bounds):
    na, nr = len(a_bounds), len(res_bounds)
    a_refs, w_ref, res_refs, g_ref = refs[:na], refs[na], refs[na + 1:na + 1 + nr], refs[na + 1 + nr]
    rest = refs[na + 2 + nr:]
    t = pl.program_id(0)
    if router:
        wr_ref, h_ref, xp_ref, rt_ref, cnt_ref, wb_ref, carry_ref, wcat_ref = rest
    else:
        h_ref, xn_ref, wb_ref = rest

    @pl.when(t == 0)
    def _():
        wb_ref[...] = w_ref[...].astype(BF16)
        if router:
            carry_ref[...] = jnp.zeros_like(carry_ref)
            wr = wr_ref[...]
            wr_hi = wr.astype(BF16)
            wcat_ref[:, :LANES] = wr_hi
            wcat_ref[:, LANES:] = (wr - wr_hi.astype(F32)).astype(BF16)

    h = _read_part(res_refs, res_bounds, t) + jnp.dot(_read_part(a_refs, a_bounds, t), wb_ref[...],
                                                     preferred_element_type=F32)
    h_ref[...] = h
    xn = h * lax.rsqrt(jnp.mean(h * h, axis=-1, keepdims=True) + EPS) * g_ref[...]
    if router:
        _route_tile(xn, wcat_ref, xp_ref, rt_ref, cnt_ref, carry_ref)
    else:
        xn_ref[...] = xn.astype(xn_ref.dtype)


def row_matmul(a_parts, w, res_parts, gain, *, rows, tm, w_router_pad=None, name="row_matmul"):
    k, d = w.shape[1], w.shape[2]
    router = w_router_pad is not None
    row_spec = lambda width: pl.BlockSpec((tm, width), lambda t: (t, 0))
    a_specs, a_bounds = _part_specs(a_parts, tm)
    res_specs, res_bounds = _part_specs(res_parts, tm)
    assert min(a_bounds[-1], res_bounds[-1]) * tm >= rows
    in_specs = (a_specs + [pl.BlockSpec((None, k, d), lambda t: (0, 0, 0), pipeline_mode=pl.Buffered(1))]
                + res_specs + [pl.BlockSpec((1, d), lambda t: (0, 0))])
    args = [*a_parts, w, *res_parts, gain]
    out_specs = [row_spec(d)]
    out_shape = [jax.ShapeDtypeStruct((rows, d), F32)]
    scratch = [pltpu.VMEM((k, d), BF16)]
    if router:
        in_specs.append(pl.BlockSpec((d, LANES), lambda t: (0, 0)))
        args.append(w_router_pad)
        out_specs += [row_spec(d), row_spec(LANES), pl.BlockSpec((8, LANES), lambda t: (0, 0))]
        out_shape += [jax.ShapeDtypeStruct((rows, d), F32),
                      jax.ShapeDtypeStruct((rows, LANES), F32),
                      jax.ShapeDtypeStruct((8, LANES), F32)]
        scratch += [pltpu.VMEM((8, LANES), F32), pltpu.VMEM((d, 2 * LANES), BF16)]
    else:
        out_specs.append(row_spec(d))
        out_shape.append(jax.ShapeDtypeStruct((rows, d), BF16))
    return pl.pallas_call(
        functools.partial(_rowmm_kernel, router=router, a_bounds=a_bounds, res_bounds=res_bounds),
        grid=(rows // tm,),
        in_specs=in_specs, out_specs=out_specs, out_shape=out_shape, scratch_shapes=scratch,
        compiler_params=_params(("arbitrary",), 56),
        name=name,
    )(*args)


def _scatter_kernel(pos_ref, xp_ref, xs_in_ref, xs_ref, sem):
    del xs_in_ref
    tm = xp_ref.shape[0]

    def row_copy(r, slot):
        return pltpu.make_async_copy(xp_ref.at[pl.ds(r, 1), :], xs_ref.at[pl.ds(slot, 1), :], sem)

    def issue(r, carry):
        row_copy(r, pos_ref[2 * r]).start()
        row_copy(r, pos_ref[2 * r + 1]).start()
        return carry

    lax.fori_loop(0, tm, issue, 0, unroll=8)
    for _ in range(2):
        pltpu.make_async_copy(xp_ref, xs_ref.at[pl.ds(0, tm), :], sem).wait()


def scatter_rows(pos_flat, xp, xs_init):
    rows = xp.shape[0]
    return pl.pallas_call(
        _scatter_kernel,
        grid=(rows // ROW_TILE,),
        in_specs=[pl.BlockSpec((2 * ROW_TILE,), lambda i: (i,), memory_space=pltpu.SMEM),
                  pl.BlockSpec((ROW_TILE, xp.shape[1]), lambda i: (i, 0)),
                  pl.BlockSpec(memory_space=pl.ANY)],
        out_specs=pl.BlockSpec(memory_space=pl.ANY),
        out_shape=jax.ShapeDtypeStruct(xs_init.shape, xs_init.dtype),
        scratch_shapes=[pltpu.SemaphoreType.DMA(())],
        input_output_aliases={2: 0},
        compiler_params=_params(("arbitrary",), 32),
        name="scatter_rows",
    )(pos_flat, xp, xs_init)


def _cast_kernel(x_ref, o_ref):
    o_ref[...] = x_ref[...].astype(o_ref.dtype)


def cast_rows(x, dtype, tm):
    rows, d = x.shape
    return pl.pallas_call(
        _cast_kernel,
        grid=(rows // tm,),
        in_specs=[pl.BlockSpec((tm, d), lambda i: (i, 0))],
        out_specs=pl.BlockSpec((tm, d), lambda i: (i, 0)),
        out_shape=jax.ShapeDtypeStruct((rows, d), dtype),
        compiler_params=_params(("parallel",), 48),
        name="cast_rows",
    )(x)


def _combine_kernel(pos_ref, rt_ref, h_ref, g_ref, ys_ref, yp_ref, ysm_ref, buf_ref, sem, *, n_prompt_tiles):
    i = pl.program_id(0)
    tm = h_ref.shape[0]

    def row_copy(r, k, slot):
        return pltpu.make_async_copy(ys_ref.at[pl.ds(slot, 1), :], buf_ref.at[k, pl.ds(r, 1), :], sem)

    def issue(r, carry):
        row_copy(r, 0, pos_ref[2 * r]).start()
        row_copy(r, 1, pos_ref[2 * r + 1]).start()
        return carry

    lax.fori_loop(0, tm, issue, 0, unroll=8)
    for k in range(2):
        pltpu.make_async_copy(ys_ref.at[pl.ds(0, tm), :], buf_ref.at[k], sem).wait()
    rt = rt_ref[...]
    y = rt[:, 4:5] * buf_ref[0] + rt[:, 5:6] * buf_ref[1]
    x = h_ref[...] + y
    out = x * lax.rsqrt(jnp.mean(x * x, axis=-1, keepdims=True) + EPS) * g_ref[...]

    @pl.when(i < n_prompt_tiles)
    def _():
        yp_ref[...] = out

    @pl.when(i >= n_prompt_tiles)
    def _():
        ysm_ref[...] = out


def combine_rows(pos_flat, rt, h, gain, ys, *, n_prompt_rows):
    rows = h.shape[0]
    npt = n_prompt_rows // ROW_TILE
    kern = functools.partial(_combine_kernel, n_prompt_tiles=npt)
    return pl.pallas_call(
        kern,
        grid=(rows // ROW_TILE,),
        in_specs=[pl.BlockSpec((2 * ROW_TILE,), lambda i: (i,), memory_space=pltpu.SMEM),
                  pl.BlockSpec((ROW_TILE, LANES), lambda i: (i, 0)),
                  pl.BlockSpec((ROW_TILE, D_MODEL), lambda i: (i, 0)),
                  pl.BlockSpec((1, D_MODEL), lambda i: (0, 0)),
                  pl.BlockSpec(memory_space=pl.ANY)],
        out_specs=[pl.BlockSpec((ROW_TILE, D_MODEL), lambda i: (jnp.minimum(i, npt - 1), 0)),
                   pl.BlockSpec((ROW_TILE, D_MODEL), lambda i: (jnp.maximum(i - npt, 0), 0))],
        out_shape=[jax.ShapeDtypeStruct((n_prompt_rows, D_MODEL), F32),
                   jax.ShapeDtypeStruct((rows - n_prompt_rows, D_MODEL), F32)],
        scratch_shapes=[pltpu.VMEM((2, ROW_TILE, D_MODEL), F32), pltpu.SemaphoreType.DMA(())],
        compiler_params=_params(("arbitrary",), 48),
        name="combine_rows",
    )(pos_flat, rt, h, gain, ys)


def _rope_tables(pos):
    half = HEAD_DIM // 2
    inv_freq = ROPE_THETA ** (-jnp.arange(half, dtype=F32) / half)
    ang = jnp.asarray(pos, jnp.int32).astype(F32)[:, None] * inv_freq[None, :]
    cos = jnp.cos(ang)
    sin = jnp.sin(ang)
    reps = LANES // HEAD_DIM
    return (jnp.tile(jnp.concatenate([cos, cos], axis=1), (1, reps)),
            jnp.tile(jnp.concatenate([-sin, sin], axis=1), (1, reps)))


def kernel(x_prompt, x_sample, state_gla, cache_k_meta, cache_v_meta, cache_k_win, cache_v_win, meta_tokens,
           norm_mix, norm_ffn, norm_kv, norm_final, gla_w_in, gla_w_gk, gla_b_gk, gla_norm, gla_w_out, kv_w,
           attn_w_q, attn_sinks, attn_w_out, ffn_w_gate_up, ffn_w_down, moe_w_router, moe_w_gate_up,
           moe_w_down):
    bsz, seq, d = x_prompt.shape
    dbsz, t_new, _ = x_sample.shape
    n_p = bsz * seq
    n_s = dbsz * t_new
    r1 = n_p + n_s
    assert r1 % ROW_TILE == 0 and seq % CHUNK == 0 and t_new == CHUNK
    r0 = -(-(r1 + CHUNK) // ROW_TILE) * ROW_TILE
    meta_row = r1
    hk = GLA_HEADS * GLA_DK
    hv = GLA_HEADS * GLA_DV
    ffn_dense = ffn_w_down.shape[1]
    ffn_expert = moe_w_down.shape[2]
    kv_width = SWA_KV_HEADS * HEAD_DIM

    x_parts = [x_prompt.reshape(n_p, d),
               jnp.concatenate([x_sample.reshape(n_s, d), meta_tokens.astype(F32),
                                jnp.zeros((r0 - r1 - N_META, d), F32)], axis=0)]
    tail_rows = r0 - r1

    tm0 = r0 // 16
    tm1 = r1 // 16
    (xn0,) = rms_rows(x_parts, norm_mix[0:1])
    proj = matmul(xn0, gla_w_in, n_out=2 * hk + 2 * hv, out_dtype=BF16, tm=tm0, tn=2 * COL_TILE,
                  name="gla_in_proj")
    w_r_pad = jnp.pad(gla_w_in[0, :, 2 * hk + 2 * hv:], ((0, 0), (0, LANES - GLA_RANK)))
    wgk_pad = jnp.pad(gla_w_gk[0], ((0, LANES - GLA_RANK), (0, 0)))
    gnorm = gla_norm[0][None, :]
    gla = functools.partial(gla_scan, proj, gate_rows(xn0, w_r_pad, wgk_pad, gla_b_gk[0][None, :]), gnorm)
    og_meta, s_meta = gla(jnp.zeros((1, GLA_HEADS, GLA_DK, GLA_DV), F32), n_seq=1, seq_len=tail_rows,
                          row0=meta_row, n_valid=N_META, s0_per_seq=False)
    og_prompt, s_prompt = gla(s_meta, n_seq=bsz, seq_len=seq, row0=0, n_valid=seq, s0_per_seq=False)
    og_sample, s_sample = gla(state_gla[0].astype(F32), n_seq=dbsz, seq_len=t_new, row0=n_p,
                              n_valid=t_new, s0_per_seq=True)
    h1, hn1 = row_matmul([og_prompt, jnp.concatenate([og_sample, og_meta], axis=0)], gla_w_out, x_parts,
                         norm_ffn[0:1], rows=r0, tm=r0 // 68, name="gla_out_proj")

    act = matmul(hn1, ffn_w_gate_up, n_out=ffn_dense, out_dtype=BF16, mode="swiglu", up_col0=ffn_dense,
                 tm=tm0, name="ffn_gate_up")
    h2 = matmul(act, ffn_w_down, n_out=d, out_dtype=F32, mode="residual", residual=h1, tm=tm0 // 2,
                vmem_mib=56, name="ffn_down")

    pos = np.concatenate([np.tile(N_META + np.arange(seq), bsz),
                          np.tile(N_META + PAST_LEN + np.arange(t_new), dbsz),
                          np.arange(N_META), np.zeros(r0 - r1 - N_META, np.int64)])
    rope_tabs = _rope_tables(pos)
    xkv, xq = rms_rows([h2], jnp.stack([norm_kv, norm_mix[1]]))
    kvf = matmul(xkv, kv_w[None], n_out=2 * kv_width, out_dtype=F32, mode="rope", rope_tabs=rope_tabs,
                 rope_cols=kv_width, tm=tm0, name="shared_kv")
    q = matmul(xq, attn_w_q, n_out=d, out_dtype=BF16, mode="rope", rope_tabs=rope_tabs, rope_cols=d,
               scale=HEAD_DIM ** -0.5, rows=r1, tm=tm1, tn=2 * COL_TILE, name="attn_q")

    sink_tab = jnp.broadcast_to(
        jnp.repeat(attn_sinks[0].astype(F32).reshape(SWA_KV_HEADS, PAIRS, 2).transpose(0, 2, 1), CHUNK, axis=2)
        [..., None], (SWA_KV_HEADS, 2, PAIRS * CHUNK, LANES))
    meta_blk = meta_row // N_META
    o_prompt = window_attention(
        q, kvf, kvf, kvf, kvf, sink_tab, n_seq=bsz, seq_len=seq, key_len=seq, row0=0,
        k_map=lambda b, s: (b, 0), v_map=lambda b, s: (b, 1),
        km_map=lambda b, s: (meta_blk, 0), vm_map=lambda b, s: (meta_blk, 1), chunk_offset=0)
    k_new = kvf[n_p:r1, :kv_width]
    v_new = kvf[n_p:r1, kv_width:]
    win = cache_k_win.shape[1]
    ks = jnp.concatenate([cache_k_win.reshape(dbsz, win, kv_width).astype(F32),
                          k_new.reshape(dbsz, t_new, kv_width)], axis=1).reshape(dbsz * (win + t_new), kv_width)
    vs = jnp.concatenate([cache_v_win.reshape(dbsz, win, kv_width).astype(F32),
                          v_new.reshape(dbsz, t_new, kv_width)], axis=1).reshape(dbsz * (win + t_new), kv_width)
    assert win + t_new == (WINDOW_CHUNKS + 1) * CHUNK
    o_sample = window_attention(
        q, ks, vs, cache_k_meta.reshape(dbsz * N_META, kv_width).astype(F32),
        cache_v_meta.reshape(dbsz * N_META, kv_width).astype(F32), sink_tab,
        n_seq=dbsz, seq_len=t_new, key_len=win + t_new, row0=n_p,
        k_map=lambda b, s: (b, 0), v_map=lambda b, s: (b, 0),
        km_map=lambda b, s: (b, 0), vm_map=lambda b, s: (b, 0), chunk_offset=WINDOW_CHUNKS)
    w_router_pad = jnp.pad(moe_w_router[0], ((0, 0), (0, LANES - N_EXPERTS)))
    h3, xp, rt, cnt = row_matmul([o_prompt, o_sample], attn_w_out, [h2], norm_ffn[1:2], rows=r1, tm=r1 // 66,
                                 w_router_pad=w_router_pad, name="attn_out_route")
    unit, per_tile = MOE_UNIT, MOE_TILE // MOE_UNIT
    n_tiles_max = 2 * r1 // MOE_TILE + N_EXPERTS
    counts = cnt[0, :N_EXPERTS].astype(jnp.int32)
    units_per = (counts + unit - 1) // unit
    tiles_per = (units_per + per_tile - 1) // per_tile
    tile_end = jnp.cumsum(tiles_per)
    tile_start = tile_end - tiles_per
    experts = rt[:, 0:2].astype(jnp.int32)
    ranks = rt[:, 2:4].astype(jnp.int32)
    pos_flat = (tile_start[experts] * MOE_TILE + ranks).reshape(-1)
    n_used = tile_end[-1:].astype(jnp.int32)
    tile_id = jnp.arange(n_tiles_max, dtype=jnp.int32)
    tile_expert = jnp.minimum(jnp.sum(tile_id[:, None] >= tile_end[None, :], axis=1), N_EXPERTS - 1).astype(jnp.int32)
    tile_units = jnp.clip(units_per[tile_expert] - per_tile * (tile_id - tile_start[tile_expert]), 0, per_tile)
    tile_units = jnp.where(tile_id < n_used[0], tile_units, 0).astype(jnp.int32)
    xs = cast_rows(scatter_rows(pos_flat, xp, jnp.zeros((n_tiles_max * MOE_TILE, d), F32)), BF16, MOE_TILE)

    def short_tile_first(step, start, count, has_short, n_steps_used):
        local = step - start
        rotated = start + jnp.where(local == 0, count - 1, local - 1)
        return jnp.where(has_short & (count >= 2) & (step < n_steps_used), rotated, step).astype(jnp.int32)

    short_last = (units_per % per_tile) != 0
    tile_order = short_tile_first(tile_id, tile_start[tile_expert], tiles_per[tile_expert],
                                  short_last[tile_expert], n_used[0])
    act_e = matmul(xs, moe_w_gate_up.reshape(N_EXPERTS, d, 2 * ffn_expert), n_out=ffn_expert, out_dtype=BF16,
                   mode="swiglu", up_col0=ffn_expert,
                   schedule=(tile_expert, tile_units[tile_order], n_used, tile_order),
                   sub_tiles=per_tile, tm=MOE_TILE, vmem_mib=56, name="moe_gate_up")
    split = MOE_TILE // MOE_DOWN_TILE
    down_id = jnp.arange(n_tiles_max * split, dtype=jnp.int32)
    down_expert = jnp.repeat(tile_expert, split)
    down_valid = ((down_id % split) * MOE_DOWN_TILE // unit < jnp.repeat(tile_units, split)).astype(jnp.int32)
    down_order = short_tile_first(down_id, split * tile_start[down_expert], split * tiles_per[down_expert],
                                  short_last[down_expert], split * n_used[0])
    ys = matmul(act_e, moe_w_down.reshape(N_EXPERTS, ffn_expert, d), n_out=d, out_dtype=F32,
                schedule=(down_expert, down_valid[down_order], split * n_used, down_order), tm=MOE_DOWN_TILE,
                vmem_mib=56, name="moe_down")
    y_prompt, y_sample = combine_rows(pos_flat, rt, h3, norm_final[None, :], ys, n_prompt_rows=n_p)

    kv_meta = kvf[meta_row:meta_row + N_META]
    k_meta_p = jnp.broadcast_to(kv_meta[None, :, :kv_width], (bsz, N_META, kv_width))
    v_meta_p = jnp.broadcast_to(kv_meta[None, :, kv_width:], (bsz, N_META, kv_width))
    win_p = min(WINDOW_CHUNKS * CHUNK, seq)
    kv_win = kvf[:n_p].reshape(bsz, seq, 2 * kv_width)[:, seq - win_p:]
    shape4 = lambda a: a.reshape(a.shape[0], a.shape[1], SWA_KV_HEADS, HEAD_DIM)
    return (y_prompt.reshape(bsz, seq, d), y_sample.reshape(dbsz, t_new, d),
            s_prompt[None].astype(state_gla.dtype), s_sample[None].astype(state_gla.dtype),
            shape4(k_meta_p), shape4(v_meta_p),
            shape4(kv_win[:, :, :kv_width]), shape4(kv_win[:, :, kv_width:]),
            shape4(k_new.reshape(dbsz, t_new, kv_width)), shape4(v_new.reshape(dbsz, t_new, kv_width)))
```

```python
import functools

import numpy as np
import jax
import jax.numpy as jnp
from jax import lax
from jax.experimental import pallas as pl
from jax.experimental.pallas import tpu as pltpu

F32 = jnp.float32
BF16 = jnp.bfloat16

D_MODEL = 2048
PAST_LEN = 2048
CHUNK = 64
N_META = 16
GLA_HEADS = 4
GLA_DK = 256
GLA_DV = 512
GLA_RANK = 16
GLA_GATE_NORM = 16.0
HEAD_DIM = 64
SWA_KV_HEADS = 4
SWA_GROUP = 8
WINDOW_CHUNKS = 2
ROPE_THETA = 10000.0
N_EXPERTS = 8
EPS = 1e-5
NEG_INF = -1e30

LANES = 128
ROW_TILE = 512
COL_TILE = 512
MOE_TILE = 1024
MOE_UNIT = 512
MOE_DOWN_TILE = 512
MIB = 2 ** 20


def _params(semantics, vmem_mib):
    return pltpu.CompilerParams(dimension_semantics=semantics, vmem_limit_bytes=vmem_mib * MIB)


def _part_specs(parts, tm):
    specs, bounds, start = [], [], 0
    for p in parts:
        n = p.shape[0] // tm
        assert n * tm == p.shape[0]
        specs.append(pl.BlockSpec((tm, p.shape[1]), lambda t, s=start, n=n: (jnp.clip(t - s, 0, n - 1), 0)))
        start += n
        bounds.append(start)
    return specs, tuple(bounds)


def _read_part(refs, bounds, t):
    x = refs[0][...]
    for ref, lo in zip(refs[1:], bounds[:-1]):
        x = jnp.where(t >= lo, ref[...], x)
    return x


def _norm_kernel(*refs, bounds):
    n_src = len(bounds)
    g_ref = refs[n_src]
    x = _read_part(refs[:n_src], bounds, pl.program_id(0))
    y = x * lax.rsqrt(jnp.mean(x * x, axis=-1, keepdims=True) + EPS)
    for i, o_ref in enumerate(refs[n_src + 1:]):
        o_ref[...] = (y * g_ref[i:i + 1, :]).astype(o_ref.dtype)


def rms_rows(parts, gains):
    d = parts[0].shape[1]
    n = gains.shape[0]
    specs, bounds = _part_specs(parts, ROW_TILE)
    rows = bounds[-1] * ROW_TILE
    return pl.pallas_call(
        functools.partial(_norm_kernel, bounds=bounds),
        grid=(bounds[-1],),
        in_specs=specs + [pl.BlockSpec((n, d), lambda i: (0, 0))],
        out_specs=[pl.BlockSpec((ROW_TILE, d), lambda i: (i, 0)) for _ in range(n)],
        out_shape=[jax.ShapeDtypeStruct((rows, d), BF16) for _ in range(n)],
        compiler_params=_params(("parallel",), 40),
        name="rms_rows",
    )(*parts, gains)


def _swap_halves(x):
    lane = lax.broadcasted_iota(jnp.int32, x.shape, 1)
    first_half = (lane % HEAD_DIM) < (HEAD_DIM // 2)
    return jnp.where(first_half, pltpu.roll(x, LANES - HEAD_DIM // 2, 1), pltpu.roll(x, HEAD_DIM // 2, 1))


def _mm_kernel(te_ref, nv_ref, nu_ref, ph_ref, *refs, mode, a_packed, rope_cols, scale, sub_tiles):
    if mode == "swiglu":
        a_ref, w_ref, w2_ref, o_ref, wb_ref, wb2_ref = refs
    elif mode == "residual":
        a_ref, w_ref, res_ref, o_ref, wb_ref = refs
    elif mode == "rope":
        a_ref, w_ref, cos_ref, sin_ref, o_ref, wb_ref = refs
    else:
        a_ref, w_ref, o_ref, wb_ref = refs
    t = pl.program_id(1)
    tm = o_ref.shape[0]
    sub = tm // sub_tiles

    def compute(n_rows):
        rows = slice(0, n_rows)
        a = a_ref[rows, :]
        if a_packed:
            lo = pltpu.bitcast(a << 16, F32)
            hi = pltpu.bitcast(a & jnp.uint32(0xFFFF0000), F32)
            a = jnp.concatenate([lo, hi], axis=1)
        a = a.astype(BF16)
        acc = jnp.dot(a, wb_ref[...], preferred_element_type=F32)
        if mode == "swiglu":
            up = jnp.dot(a, wb2_ref[...], preferred_element_type=F32)
            o_ref[rows, :] = (acc * jax.nn.sigmoid(acc) * up).astype(o_ref.dtype)
        elif mode == "residual":
            o_ref[rows, :] = (res_ref[rows, :] + acc).astype(o_ref.dtype)
        elif mode == "rope":
            cos = cos_ref[rows, :]
            sin = sin_ref[rows, :]
            for c in range(acc.shape[1] // LANES):
                x = acc[:, c * LANES:(c + 1) * LANES]
                if c * LANES < rope_cols:
                    x = x * cos + _swap_halves(x) * sin
                o_ref[rows, c * LANES:(c + 1) * LANES] = (x * scale).astype(o_ref.dtype)
        else:
            o_ref[rows, :] = acc.astype(o_ref.dtype)
        if n_rows < tm:
            o_ref[n_rows:, :] = jnp.zeros((tm - n_rows, o_ref.shape[1]), o_ref.dtype)

    @pl.when(t < nu_ref[0])
    def _():
        @pl.when((t == 0) | (te_ref[t] != te_ref[jnp.maximum(t - 1, 0)]))
        def _():
            wb_ref[...] = w_ref[...].astype(BF16)
            if mode == "swiglu":
                wb2_ref[...] = w2_ref[...].astype(BF16)

    n_valid = jnp.where(t < nu_ref[0], nv_ref[t], 0)

    @pl.when(n_valid == 0)
    def _():
        o_ref[...] = jnp.zeros_like(o_ref)

    for s in range(1, sub_tiles + 1):
        pl.when(n_valid == s)(functools.partial(compute, s * sub))


def matmul(a, w, *, n_out, out_dtype, mode="plain", schedule=None, sub_tiles=1, residual=None,
           rope_tabs=None, rope_cols=0, scale=1.0, col0=0, up_col0=0, tm=ROW_TILE, tn=COL_TILE,
           a_packed=False, rows=None, vmem_mib=48, name="matmul"):
    rows = a.shape[0] if rows is None else rows
    k = w.shape[1]
    n_tiles = rows // tm
    n_col = n_out // tn
    if schedule is None:
        schedule = (jnp.zeros((n_tiles,), jnp.int32), jnp.full((n_tiles,), sub_tiles, jnp.int32),
                    jnp.full((1,), n_tiles, jnp.int32), jnp.arange(n_tiles, dtype=jnp.int32))
    cb0 = col0 // tn
    ub0 = up_col0 // tn

    def step_tile(t, nu, ph):
        return ph[jnp.where(t < nu[0], t, 0)]

    def step_weight(j, t, te, nu, c0):
        idle = t >= nu[0]
        return te[jnp.where(idle, 0, t)], 0, c0 + jnp.where(idle, jnp.minimum(j + 1, n_col - 1), j)

    a_spec = pl.BlockSpec((tm, a.shape[1]), lambda j, t, te, nv, nu, ph: (step_tile(t, nu, ph), 0))
    w_spec = pl.BlockSpec((None, k, tn), lambda j, t, te, nv, nu, ph: step_weight(j, t, te, nu, cb0))
    o_spec = pl.BlockSpec((tm, tn), lambda j, t, te, nv, nu, ph: (ph[t], j))
    in_specs = [a_spec, w_spec]
    args = [a, w]
    scratch = [pltpu.VMEM((k, tn), BF16)]
    if mode == "swiglu":
        in_specs.append(pl.BlockSpec((None, k, tn), lambda j, t, te, nv, nu, ph: step_weight(j, t, te, nu, ub0)))
        args.append(w)
        scratch.append(pltpu.VMEM((k, tn), BF16))
    elif mode == "residual":
        in_specs.append(o_spec)
        args.append(residual)
    elif mode == "rope":
        tab_spec = pl.BlockSpec((tm, LANES), lambda j, t, te, nv, nu, ph: (step_tile(t, nu, ph), 0))
        in_specs += [tab_spec, tab_spec]
        args += list(rope_tabs)
    kern = functools.partial(_mm_kernel, mode=mode, a_packed=a_packed, rope_cols=rope_cols, scale=scale,
                             sub_tiles=sub_tiles)
    return pl.pallas_call(
        kern,
        grid_spec=pltpu.PrefetchScalarGridSpec(
            num_scalar_prefetch=4, grid=(n_col, n_tiles),
            in_specs=in_specs, out_specs=o_spec, scratch_shapes=scratch),
        out_shape=jax.ShapeDtypeStruct((rows, n_out), out_dtype),
        compiler_params=_params(("arbitrary", "arbitrary"), vmem_mib),
        name=name,
    )(*schedule, *args)


GLA_ROW_BLOCK = 512


def _gate_kernel(x_ref, wr_ref, wgk_ref, bgk_ref, g_ref):
    r = jnp.dot(x_ref[...], wr_ref[...].astype(BF16), preferred_element_type=F32)
    r_hi = r.astype(BF16)
    r_lo = (r - r_hi.astype(F32)).astype(BF16)
    w = wgk_ref[...]
    w_hi = w.astype(BF16)
    w_lo = (w - w_hi.astype(F32)).astype(BF16)
    z = (jnp.dot(r_hi, w_hi, preferred_element_type=F32) + jnp.dot(r_hi, w_lo, preferred_element_type=F32)
         + jnp.dot(r_lo, w_hi, preferred_element_type=F32) + bgk_ref[...])
    g_ref[...] = (jnp.minimum(z, 0.0) - jnp.log(1.0 + jnp.exp(-jnp.abs(z)))) / GLA_GATE_NORM


def gate_rows(xn, w_r_pad, wgk_pad, bgk):
    rows, d = xn.shape
    width = wgk_pad.shape[1]
    return pl.pallas_call(
        _gate_kernel,
        grid=(rows // ROW_TILE,),
        in_specs=[pl.BlockSpec((ROW_TILE, d), lambda i: (i, 0)),
                  pl.BlockSpec((d, LANES), lambda i: (0, 0)),
                  pl.BlockSpec((LANES, width), lambda i: (0, 0)),
                  pl.BlockSpec((1, width), lambda i: (0, 0))],
        out_specs=pl.BlockSpec((ROW_TILE, width), lambda i: (i, 0)),
        out_shape=jax.ShapeDtypeStruct((rows, width), F32),
        compiler_params=_params(("parallel",), 32),
        name="gate_rows",
    )(xn, w_r_pad, wgk_pad, bgk)


def _gla_kernel(q_ref, k_ref, v_ref, go_ref, g_ref, gn_ref, s0_ref,
                o_ref, sfin_ref, st_ref, *, n_chunks, n_valid):
    rb = pl.program_id(1)

    @pl.when(rb == 0)
    def _():
        for h in range(GLA_HEADS):
            st_ref[h] = s0_ref[h].T

    ri = lax.broadcasted_iota(jnp.int32, (CHUNK, CHUNK), 0)
    ci = lax.broadcasted_iota(jnp.int32, (CHUNK, CHUNK), 1)
    causal = ci <= ri
    tril = jnp.where(causal, 1.0, 0.0).astype(BF16)
    row_in_chunk = lax.broadcasted_iota(jnp.int32, (CHUNK, 1), 0)
    gn = gn_ref[...]

    def body(c, carry):
        rows = pl.ds(pl.multiple_of(c * CHUNK, CHUNK), CHUNK)
        row_valid = (rb * n_chunks + c) * CHUNK + row_in_chunk < n_valid
        g = jnp.where(row_valid, g_ref[rows, :], 0.0)
        g_hi = g.astype(BF16)
        rest = g - g_hi.astype(F32)
        g_mid = rest.astype(BF16)
        g_lo = (rest - g_mid.astype(F32)).astype(BF16)
        b = (jnp.dot(tril, g_hi, preferred_element_type=F32) + jnp.dot(tril, g_mid, preferred_element_type=F32)
             + jnp.dot(tril, g_lo, preferred_element_type=F32))
        b_last = b[CHUNK - 1:CHUNK, :]
        q = q_ref[rows, :].astype(F32) * (GLA_DK ** -0.5)
        k = jnp.where(row_valid, k_ref[rows, :].astype(F32), 0.0)
        q_dec_all = (q * jnp.exp(b)).astype(BF16)
        k_dec_all = (k * jnp.exp(-b)).astype(BF16)
        k_last_all = (k * jnp.exp(b_last - b)).astype(BF16)
        decay = jnp.exp(b_last)
        for h in range(GLA_HEADS):
            ks = slice(h * GLA_DK, (h + 1) * GLA_DK)
            vs = slice(h * GLA_DV, (h + 1) * GLA_DV)
            q_dec = q_dec_all[:, ks]
            v = v_ref[rows, vs]
            att = lax.dot_general(q_dec, k_dec_all[:, ks], (((1,), (1,)), ((), ())),
                                  preferred_element_type=F32)
            att = jnp.where(causal, att, 0.0).astype(BF16)
            st = st_ref[h]
            o = jnp.dot(att, v, preferred_element_type=F32)
            o = o + lax.dot_general(q_dec, st.astype(BF16), (((1,), (1,)), ((), ())),
                                    preferred_element_type=F32)
            st_ref[h] = st * decay[:, ks] + lax.dot_general(
                v, k_last_all[:, ks], (((0,), (0,)), ((), ())), preferred_element_type=F32)
            on = o * lax.rsqrt(jnp.mean(o * o, axis=-1, keepdims=True) + EPS) * gn
            go = go_ref[rows, vs].astype(F32)
            o_ref[rows, vs] = (on * (go * jax.nn.sigmoid(go))).astype(o_ref.dtype)
        return carry

    lax.fori_loop(0, n_chunks, body, 0, unroll=4 if n_chunks % 4 == 0 else 1)

    @pl.when(rb == pl.num_programs(1) - 1)
    def _():
        for h in range(GLA_HEADS):
            sfin_ref[h] = st_ref[h].T


def gla_scan(proj, gates, gnorm, s0, *, n_seq, seq_len, row0, n_valid, s0_per_seq):
    blk = min(seq_len, GLA_ROW_BLOCK)
    n_rb = seq_len // blk
    hk = GLA_HEADS * GLA_DK
    hv = GLA_HEADS * GLA_DV
    rb0 = row0 // blk
    row = lambda b, r: rb0 + b * n_rb + r
    s0_map = (lambda b, r: (b, 0, 0, 0)) if s0_per_seq else (lambda b, r: (0, 0, 0, 0))
    kern = functools.partial(_gla_kernel, n_chunks=blk // CHUNK, n_valid=n_valid)
    return pl.pallas_call(
        kern,
        grid=(n_seq, n_rb),
        in_specs=[
            pl.BlockSpec((blk, hk), lambda b, r: (row(b, r), 0)),
            pl.BlockSpec((blk, hk), lambda b, r: (row(b, r), 1)),
            pl.BlockSpec((blk, hv), lambda b, r: (row(b, r), 2 * hk // hv)),
            pl.BlockSpec((blk, hv), lambda b, r: (row(b, r), 2 * hk // hv + 1)),
            pl.BlockSpec((blk, hk), lambda b, r: (row(b, r), 0)),
            pl.BlockSpec((1, GLA_DV), lambda b, r: (0, 0)),
            pl.BlockSpec((None, GLA_HEADS, GLA_DK, GLA_DV), s0_map),
        ],
        out_specs=[
            pl.BlockSpec((blk, hv), lambda b, r: (b * n_rb + r, 0)),
            pl.BlockSpec((None, GLA_HEADS, GLA_DK, GLA_DV), lambda b, r: (b, 0, 0, 0)),
        ],
        out_shape=[jax.ShapeDtypeStruct((n_seq * seq_len, hv), BF16),
                   jax.ShapeDtypeStruct((n_seq, GLA_HEADS, GLA_DK, GLA_DV), F32)],
        scratch_shapes=[pltpu.VMEM((GLA_HEADS, GLA_DV, GLA_DK), F32)],
        compiler_params=_params(("parallel", "arbitrary"), 48),
        name="gla_scan",
    )(proj, proj, proj, proj, gates, gnorm, s0)


WIN_KEYS = (WINDOW_CHUNKS + 1) * CHUNK
ATTN_KEYS = 2 * LANES
ATTN_ROW_BLOCK = 512
PAIRS = SWA_GROUP // 2


def _lane_halves(x2, head_in_pair):
    lane = lax.broadcasted_iota(jnp.int32, x2.shape, 1)
    low = lane < HEAD_DIM
    swapped = pltpu.roll(x2, HEAD_DIM, 1)
    if head_in_pair == 0:
        lo, hi = jnp.where(low, x2, 0.0), jnp.where(low, 0.0, swapped)
    else:
        lo, hi = jnp.where(low, swapped, 0.0), jnp.where(low, 0.0, x2)
    return lo.astype(BF16), hi.astype(BF16)


def _attn_kernel(q_ref, k_ref, v_ref, km_ref, vm_ref, sink_ref, o_ref, kb_ref, vb_ref, kmb_ref, vmb_ref,
                 *, n_chunks, chunk_offset):
    step = pl.program_id(1)

    @pl.when(step == 0)
    def _():
        for h in range(SWA_KV_HEADS):
            cols = slice((h // 2) * LANES, (h // 2 + 1) * LANES)
            for src, dst in ((k_ref, kb_ref), (v_ref, vb_ref), (km_ref, kmb_ref), (vm_ref, vmb_ref)):
                lo, hi = _lane_halves(src[:, cols], h % 2)
                dst[h, 0] = lo
                dst[h, 1] = hi

    j = lax.broadcasted_iota(jnp.int32, (1, ATTN_KEYS), 1)
    rel_chunk = (j >= N_META + CHUNK).astype(jnp.int32) + (j >= N_META + 2 * CHUNK).astype(jnp.int32)
    in_window = (j >= N_META) & (j < N_META + WIN_KEYS)
    is_sink_slot = j == N_META + WIN_KEYS
    zpad = jnp.zeros((ATTN_KEYS - N_META - WIN_KEYS, LANES), BF16)
    key_row = lax.broadcasted_iota(jnp.int32, (2 * ATTN_KEYS, LANES), 0)
    key_lane = lax.broadcasted_iota(jnp.int32, (2 * ATTN_KEYS, LANES), 1)
    ones_cols = ((key_row < ATTN_KEYS) == (key_lane < HEAD_DIM)).astype(BF16)
    nt = (((1,), (1,)), ((), ()))

    def body(ci, carry):
        c = step * n_chunks + ci + chunk_offset
        wc = jnp.maximum(c - WINDOW_CHUNKS, 0)
        win = pl.ds(pl.multiple_of(wc * CHUNK, CHUNK), WIN_KEYS)
        rows = pl.ds(pl.multiple_of(ci * CHUNK, CHUNK), CHUNK)
        valid = (j < N_META) | (in_window & (wc + rel_chunk <= c))
        for h in range(SWA_KV_HEADS):
            kb = jnp.concatenate([kmb_ref[h, 0], kb_ref[h, 0, win, :], zpad,
                                  kmb_ref[h, 1], kb_ref[h, 1, win, :], zpad], axis=0)
            vb = jnp.concatenate([vmb_ref[h, 0], vb_ref[h, 0, win, :], zpad,
                                  vmb_ref[h, 1], vb_ref[h, 1, win, :], zpad], axis=0)
            col = lambda p: slice((h * PAIRS + p) * LANES, (h * PAIRS + p + 1) * LANES)
            qs = jnp.concatenate([q_ref[rows, col(p)] for p in range(PAIRS)], axis=0)
            s = lax.dot_general(qs, kb, nt, preferred_element_type=F32)
            probs = []
            for half in range(2):
                sink = sink_ref[h, half]
                sh = jnp.where(valid, s[:, half * ATTN_KEYS:(half + 1) * ATTN_KEYS], NEG_INF)
                sh = jnp.where(is_sink_slot, jnp.concatenate([sink] * (ATTN_KEYS // LANES), axis=1), sh)
                m = jnp.max(sh, axis=-1, keepdims=True)
                probs.append(jnp.exp(sh - m).astype(BF16))
            ov = jnp.dot(jnp.concatenate(probs, axis=1), jnp.concatenate([vb, ones_cols], axis=1),
                         preferred_element_type=F32)
            o = ov[:, :LANES] / ov[:, LANES:]
            for p in range(PAIRS):
                o_ref[rows, col(p)] = o[p * CHUNK:(p + 1) * CHUNK].astype(o_ref.dtype)
        return carry

    lax.fori_loop(0, n_chunks, body, 0, unroll=4 if n_chunks % 4 == 0 else 1)


def window_attention(q, k_arr, v_arr, km_arr, vm_arr, sink_tab, *, n_seq, seq_len, key_len, row0,
                     k_map, v_map, km_map, vm_map, chunk_offset):
    blk = min(seq_len, ATTN_ROW_BLOCK)
    steps = seq_len // blk
    q_spec = pl.BlockSpec((blk, q.shape[1]), lambda b, s: (row0 // blk + b * steps + s, 0))
    o_spec = pl.BlockSpec((blk, q.shape[1]), lambda b, s: (b * steps + s, 0))
    width = SWA_KV_HEADS * HEAD_DIM
    kern = functools.partial(_attn_kernel, n_chunks=blk // CHUNK, chunk_offset=chunk_offset)
    return pl.pallas_call(
        kern,
        grid=(n_seq, steps),
        in_specs=[q_spec,
                  pl.BlockSpec((key_len, width), k_map),
                  pl.BlockSpec((key_len, width), v_map),
                  pl.BlockSpec((N_META, width), km_map),
                  pl.BlockSpec((N_META, width), vm_map),
                  pl.BlockSpec((SWA_KV_HEADS, 2, PAIRS * CHUNK, LANES), lambda b, s: (0, 0, 0, 0))],
        out_specs=o_spec,
        out_shape=jax.ShapeDtypeStruct((n_seq * seq_len, q.shape[1]), q.dtype),
        scratch_shapes=[pltpu.VMEM((SWA_KV_HEADS, 2, key_len, LANES), BF16),
                        pltpu.VMEM((SWA_KV_HEADS, 2, key_len, LANES), BF16),
                        pltpu.VMEM((SWA_KV_HEADS, 2, N_META, LANES), BF16),
                        pltpu.VMEM((SWA_KV_HEADS, 2, N_META, LANES), BF16)],
        compiler_params=_params(("parallel", "arbitrary"), 48),
        name="window_attention",
    )(q, k_arr, v_arr, km_arr, vm_arr, sink_tab)


def _route_tile(xn, wr_ref, xp_ref, rt_ref, cnt_ref, carry_ref):
    xp_ref[...] = xn

    tm = xn.shape[0]
    hi = xn.astype(BF16)
    lo = (xn - hi.astype(F32)).astype(BF16)
    wide = jnp.dot(hi, wr_ref[...], preferred_element_type=F32)
    logits = (wide[:, :LANES] + wide[:, LANES:]
              + jnp.dot(lo, wr_ref[:, :LANES], preferred_element_type=F32))
    lane = lax.broadcasted_iota(jnp.int32, (tm, LANES), 1).astype(F32)
    neg = jnp.float32(-jnp.inf)
    lg = jnp.where(lane < N_EXPERTS, logits, neg)
    m1 = jnp.max(lg, axis=-1, keepdims=True)
    i1 = jnp.min(jnp.where(lg == m1, lane, float(LANES)), axis=-1, keepdims=True)
    lg2 = jnp.where(lane == i1, neg, lg)
    m2 = jnp.max(lg2, axis=-1, keepdims=True)
    i2 = jnp.min(jnp.where(lg2 == m2, lane, float(LANES)), axis=-1, keepdims=True)
    e21 = jnp.exp(m2 - m1)
    g1 = 1.0 / (1.0 + e21)
    g2 = e21 / (1.0 + e21)
    oh1 = (lane == i1).astype(F32)
    oh2 = (lane == i2).astype(F32)
    oh = oh1 + oh2
    ri = lax.broadcasted_iota(jnp.int32, (tm, tm), 0)
    ci = lax.broadcasted_iota(jnp.int32, (tm, tm), 1)
    before = (ci < ri).astype(BF16)
    prefix = jnp.dot(before, oh.astype(BF16), preferred_element_type=F32) + carry_ref[0:1, :]
    rank1 = jnp.sum(prefix * oh1, axis=-1, keepdims=True)
    rank2 = jnp.sum(prefix * oh2, axis=-1, keepdims=True)
    total = carry_ref[0:1, :] + jnp.sum(oh, axis=0, keepdims=True)
    carry_ref[...] = jnp.broadcast_to(total, carry_ref.shape)
    cnt_ref[...] = jnp.broadcast_to(total, cnt_ref.shape)
    rt = jnp.where(lane == 0, i1, 0.0)
    rt = jnp.where(lane == 1, i2, rt)
    rt = jnp.where(lane == 2, rank1, rt)
    rt = jnp.where(lane == 3, rank2, rt)
    rt = jnp.where(lane == 4, g1, rt)
    rt = jnp.where(lane == 5, g2, rt)
    rt_ref[...] = rt


def _rowmm_kernel(*refs, router, a_bounds, res_bounds):
    na, nr = len(a_bounds), len(res_bounds)
    a_refs, w_ref, res_refs, g_ref = refs[:na], refs[na], refs[na + 1:na + 1 + nr], refs[na + 1 + nr]
    rest = refs[na + 2 + nr:]
    t = pl.program_id(0)
    if router:
        wr_ref, h_ref, xp_ref, rt_ref, cnt_ref, wb_ref, carry_ref, wcat_ref = rest
    else:
        h_ref, xn_ref, wb_ref = rest

    @pl.when(t == 0)
    def _():
        wb_ref[...] = w_ref[...].astype(BF16)
        if router:
            carry_ref[...] = jnp.zeros_like(carry_ref)
            wr = wr_ref[...]
            wr_hi = wr.astype(BF16)
            wcat_ref[:, :LANES] = wr_hi
            wcat_ref[:, LANES:] = (wr - wr_hi.astype(F32)).astype(BF16)

    h = _read_part(res_refs, res_bounds, t) + jnp.dot(_read_part(a_refs, a_bounds, t), wb_ref[...],
                                                     preferred_element_type=F32)
    h_ref[...] = h
    xn = h * lax.rsqrt(jnp.mean(h * h, axis=-1, keepdims=True) + EPS) * g_ref[...]
    if router:
        _route_tile(xn, wcat_ref, xp_ref, rt_ref, cnt_ref, carry_ref)
    else:
        xn_ref[...] = xn.astype(xn_ref.dtype)


def row_matmul(a_parts, w, res_parts, gain, *, rows, tm, w_router_pad=None, name="row_matmul"):
    k, d = w.shape[1], w.shape[2]
    router = w_router_pad is not None
    row_spec = lambda width: pl.BlockSpec((tm, width), lambda t: (t, 0))
    a_specs, a_bounds = _part_specs(a_parts, tm)
    res_specs, res_bounds = _part_specs(res_parts, tm)
    assert min(a_bounds[-1], res_bounds[-1]) * tm >= rows
    in_specs = (a_specs + [pl.BlockSpec((None, k, d), lambda t: (0, 0, 0), pipeline_mode=pl.Buffered(1))]
                + res_specs + [pl.BlockSpec((1, d), lambda t: (0, 0))])
    args = [*a_parts, w, *res_parts, gain]
    out_specs = [row_spec(d)]
    out_shape = [jax.ShapeDtypeStruct((rows, d), F32)]
    scratch = [pltpu.VMEM((k, d), BF16)]
    if router:
        in_specs.append(pl.BlockSpec((d, LANES), lambda t: (0, 0)))
        args.append(w_router_pad)
        out_specs += [row_spec(d), row_spec(LANES), pl.BlockSpec((8, LANES), lambda t: (0, 0))]
        out_shape += [jax.ShapeDtypeStruct((rows, d), F32),
                      jax.ShapeDtypeStruct((rows, LANES), F32),
                      jax.ShapeDtypeStruct((8, LANES), F32)]
        scratch += [pltpu.VMEM((8, LANES), F32), pltpu.VMEM((d, 2 * LANES), BF16)]
    else:
        out_specs.append(row_spec(d))
        out_shape.append(jax.ShapeDtypeStruct((rows, d), BF16))
    return pl.pallas_call(
        functools.partial(_rowmm_kernel, router=router, a_bounds=a_bounds, res_bounds=res_bounds),
        grid=(rows // tm,),
        in_specs=in_specs, out_specs=out_specs, out_shape=out_shape, scratch_shapes=scratch,
        compiler_params=_params(("arbitrary",), 56),
        name=name,
    )(*args)


def _scatter_kernel(pos_ref, xp_ref, xs_in_ref, xs_ref, sem):
    del xs_in_ref
    tm = xp_ref.shape[0]

    def row_copy(r, slot):
        return pltpu.make_async_copy(xp_ref.at[pl.ds(r, 1), :], xs_ref.at[pl.ds(slot, 1), :], sem)

    def issue(r, carry):
        row_copy(r, pos_ref[2 * r]).start()
        row_copy(r, pos_ref[2 * r + 1]).start()
        return carry

    lax.fori_loop(0, tm, issue, 0, unroll=8)
    for _ in range(2):
        pltpu.make_async_copy(xp_ref, xs_ref.at[pl.ds(0, tm), :], sem).wait()


def scatter_rows(pos_flat, xp, xs_init):
    rows = xp.shape[0]
    return pl.pallas_call(
        _scatter_kernel,
        grid=(rows // ROW_TILE,),
        in_specs=[pl.BlockSpec((2 * ROW_TILE,), lambda i: (i,), memory_space=pltpu.SMEM),
                  pl.BlockSpec((ROW_TILE, xp.shape[1]), lambda i: (i, 0)),
                  pl.BlockSpec(memory_space=pl.ANY)],
        out_specs=pl.BlockSpec(memory_space=pl.ANY),
        out_shape=jax.ShapeDtypeStruct(xs_init.shape, xs_init.dtype),
        scratch_shapes=[pltpu.SemaphoreType.DMA(())],
        input_output_aliases={2: 0},
        compiler_params=_params(("arbitrary",), 32),
        name="scatter_rows",
    )(pos_flat, xp, xs_init)


def _cast_kernel(x_ref, o_ref):
    o_ref[...] = x_ref[...].astype(o_ref.dtype)


def cast_rows(x, dtype, tm):
    rows, d = x.shape
    return pl.pallas_call(
        _cast_kernel,
        grid=(rows // tm,),
        in_specs=[pl.BlockSpec((tm, d), lambda i: (i, 0))],
        out_specs=pl.BlockSpec((tm, d), lambda i: (i, 0)),
        out_shape=jax.ShapeDtypeStruct((rows, d), dtype),
        compiler_params=_params(("parallel",), 48),
        name="cast_rows",
    )(x)


def _combine_kernel(pos_ref, rt_ref, h_ref, g_ref, ys_ref, yp_ref, ysm_ref, buf_ref, sem, *, n_prompt_tiles):
    i = pl.program_id(0)
    tm = h_ref.shape[0]

    def row_copy(r, k, slot):
        return pltpu.make_async_copy(ys_ref.at[pl.ds(slot, 1), :], buf_ref.at[k, pl.ds(r, 1), :], sem)

    def issue(r, carry):
        row_copy(r, 0, pos_ref[2 * r]).start()
        row_copy(r, 1, pos_ref[2 * r + 1]).start()
        return carry

    lax.fori_loop(0, tm, issue, 0, unroll=8)
    for k in range(2):
        pltpu.make_async_copy(ys_ref.at[pl.ds(0, tm), :], buf_ref.at[k], sem).wait()
    rt = rt_ref[...]
    y = rt[:, 4:5] * buf_ref[0] + rt[:, 5:6] * buf_ref[1]
    x = h_ref[...] + y
    out = x * lax.rsqrt(jnp.mean(x * x, axis=-1, keepdims=True) + EPS) * g_ref[...]

    @pl.when(i < n_prompt_tiles)
    def _():
        yp_ref[...] = out

    @pl.when(i >= n_prompt_tiles)
    def _():
        ysm_ref[...] = out


def combine_rows(pos_flat, rt, h, gain, ys, *, n_prompt_rows):
    rows = h.shape[0]
    npt = n_prompt_rows // ROW_TILE
    kern = functools.partial(_combine_kernel, n_prompt_tiles=npt)
    return pl.pallas_call(
        kern,
        grid=(rows // ROW_TILE,),
        in_specs=[pl.BlockSpec((2 * ROW_TILE,), lambda i: (i,), memory_space=pltpu.SMEM),
                  pl.BlockSpec((ROW_TILE, LANES), lambda i: (i, 0)),
                  pl.BlockSpec((ROW_TILE, D_MODEL), lambda i: (i, 0)),
                  pl.BlockSpec((1, D_MODEL), lambda i: (0, 0)),
                  pl.BlockSpec(memory_space=pl.ANY)],
        out_specs=[pl.BlockSpec((ROW_TILE, D_MODEL), lambda i: (jnp.minimum(i, npt - 1), 0)),
                   pl.BlockSpec((ROW_TILE, D_MODEL), lambda i: (jnp.maximum(i - npt, 0), 0))],
        out_shape=[jax.ShapeDtypeStruct((n_prompt_rows, D_MODEL), F32),
                   jax.ShapeDtypeStruct((rows - n_prompt_rows, D_MODEL), F32)],
        scratch_shapes=[pltpu.VMEM((2, ROW_TILE, D_MODEL), F32), pltpu.SemaphoreType.DMA(())],
        compiler_params=_params(("arbitrary",), 48),
        name="combine_rows",
    )(pos_flat, rt, h, gain, ys)


def _rope_tables(pos):
    half = HEAD_DIM // 2
    inv_freq = ROPE_THETA ** (-jnp.arange(half, dtype=F32) / half)
    ang = jnp.asarray(pos, jnp.int32).astype(F32)[:, None] * inv_freq[None, :]
    cos = jnp.cos(ang)
    sin = jnp.sin(ang)
    reps = LANES // HEAD_DIM
    return (jnp.tile(jnp.concatenate([cos, cos], axis=1), (1, reps)),
            jnp.tile(jnp.concatenate([-sin, sin], axis=1), (1, reps)))


def kernel(x_prompt, x_sample, state_gla, cache_k_meta, cache_v_meta, cache_k_win, cache_v_win, meta_tokens,
           norm_mix, norm_ffn, norm_kv, norm_final, gla_w_in, gla_w_gk, gla_b_gk, gla_norm, gla_w_out, kv_w,
           attn_w_q, attn_sinks, attn_w_out, ffn_w_gate_up, ffn_w_down, moe_w_router, moe_w_gate_up,
           moe_w_down):
    bsz, seq, d = x_prompt.shape
    dbsz, t_new, _ = x_sample.shape
    n_p = bsz * seq
    n_s = dbsz * t_new
    r1 = n_p + n_s
    assert r1 % ROW_TILE == 0 and seq % CHUNK == 0 and t_new == CHUNK
    r0 = -(-(r1 + CHUNK) // ROW_TILE) * ROW_TILE
    meta_row = r1
    hk = GLA_HEADS * GLA_DK
    hv = GLA_HEADS * GLA_DV
    ffn_dense = ffn_w_down.shape[1]
    ffn_expert = moe_w_down.shape[2]
    kv_width = SWA_KV_HEADS * HEAD_DIM

    x_parts = [x_prompt.reshape(n_p, d),
               jnp.concatenate([x_sample.reshape(n_s, d), meta_tokens.astype(F32),
                                jnp.zeros((r0 - r1 - N_META, d), F32)], axis=0)]
    tail_rows = r0 - r1

    tm0 = r0 // 16
    tm1 = r1 // 16
    (xn0,) = rms_rows(x_parts, norm_mix[0:1])
    proj = matmul(xn0, gla_w_in, n_out=2 * hk + 2 * hv, out_dtype=BF16, tm=tm0, tn=2 * COL_TILE,
                  name="gla_in_proj")
    w_r_pad = jnp.pad(gla_w_in[0, :, 2 * hk + 2 * hv:], ((0, 0), (0, LANES - GLA_RANK)))
    wgk_pad = jnp.pad(gla_w_gk[0], ((0, LANES - GLA_RANK), (0, 0)))
    gnorm = gla_norm[0][None, :]
    gla = functools.partial(gla_scan, proj, gate_rows(xn0, w_r_pad, wgk_pad, gla_b_gk[0][None, :]), gnorm)
    og_meta, s_meta = gla(jnp.zeros((1, GLA_HEADS, GLA_DK, GLA_DV), F32), n_seq=1, seq_len=tail_rows,
                          row0=meta_row, n_valid=N_META, s0_per_seq=False)
    og_prompt, s_prompt = gla(s_meta, n_seq=bsz, seq_len=seq, row0=0, n_valid=seq, s0_per_seq=False)
    og_sample, s_sample = gla(state_gla[0].astype(F32), n_seq=dbsz, seq_len=t_new, row0=n_p,
                              n_valid=t_new, s0_per_seq=True)
    h1, hn1 = row_matmul([og_prompt, jnp.concatenate([og_sample, og_meta], axis=0)], gla_w_out, x_parts,
                         norm_ffn[0:1], rows=r0, tm=r0 // 68, name="gla_out_proj")

    act = matmul(hn1, ffn_w_gate_up, n_out=ffn_dense, out_dtype=BF16, mode="swiglu", up_col0=ffn_dense,
                 tm=tm0, name="ffn_gate_up")
    h2 = matmul(act, ffn_w_down, n_out=d, out_dtype=F32, mode="residual", residual=h1, tm=tm0 // 2,
                vmem_mib=56, name="ffn_down")

    pos = np.concatenate([np.tile(N_META + np.arange(seq), bsz),
                          np.tile(N_META + PAST_LEN + np.arange(t_new), dbsz),
                          np.arange(N_META), np.zeros(r0 - r1 - N_META, np.int64)])
    rope_tabs = _rope_tables(pos)
    xkv, xq = rms_rows([h2], jnp.stack([norm_kv, norm_mix[1]]))
    kvf = matmul(xkv, kv_w[None], n_out=2 * kv_width, out_dtype=F32, mode="rope", rope_tabs=rope_tabs,
                 rope_cols=kv_width, tm=tm0, name="shared_kv")
    q = matmul(xq, attn_w_q, n_out=d, out_dtype=BF16, mode="rope", rope_tabs=rope_tabs, rope_cols=d,
               scale=HEAD_DIM ** -0.5, rows=r1, tm=tm1, tn=2 * COL_TILE, name="attn_q")

    sink_tab = jnp.broadcast_to(
        jnp.repeat(attn_sinks[0].astype(F32).reshape(SWA_KV_HEADS, PAIRS, 2).transpose(0, 2, 1), CHUNK, axis=2)
        [..., None], (SWA_KV_HEADS, 2, PAIRS * CHUNK, LANES))
    meta_blk = meta_row // N_META
    o_prompt = window_attention(
        q, kvf, kvf, kvf, kvf, sink_tab, n_seq=bsz, seq_len=seq, key_len=seq, row0=0,
        k_map=lambda b, s: (b, 0), v_map=lambda b, s: (b, 1),
        km_map=lambda b, s: (meta_blk, 0), vm_map=lambda b, s: (meta_blk, 1), chunk_offset=0)
    k_new = kvf[n_p:r1, :kv_width]
    v_new = kvf[n_p:r1, kv_width:]
    win = cache_k_win.shape[1]
    ks = jnp.concatenate([cache_k_win.reshape(dbsz, win, kv_width).astype(F32),
                          k_new.reshape(dbsz, t_new, kv_width)], axis=1).reshape(dbsz * (win + t_new), kv_width)
    vs = jnp.concatenate([cache_v_win.reshape(dbsz, win, kv_width).astype(F32),
                          v_new.reshape(dbsz, t_new, kv_width)], axis=1).reshape(dbsz * (win + t_new), kv_width)
    assert win + t_new == (WINDOW_CHUNKS + 1) * CHUNK
    o_sample = window_attention(
        q, ks, vs, cache_k_meta.reshape(dbsz * N_META, kv_width).astype(F32),
        cache_v_meta.reshape(dbsz * N_META, kv_width).astype(F32), sink_tab,
        n_seq=dbsz, seq_len=t_new, key_len=win + t_new, row0=n_p,
        k_map=lambda b, s: (b, 0), v_map=lambda b, s: (b, 0),
        km_map=lambda b, s: (b, 0), vm_map=lambda b, s: (b, 0), chunk_offset=WINDOW_CHUNKS)
    w_router_pad = jnp.pad(moe_w_router[0], ((0, 0), (0, LANES - N_EXPERTS)))
    h3, xp, rt, cnt = row_matmul([o_prompt, o_sample], attn_w_out, [h2], norm_ffn[1:2], rows=r1, tm=r1 // 66,
                                 w_router_pad=w_router_pad, name="attn_out_route")
    unit, per_tile = MOE_UNIT, MOE_TILE // MOE_UNIT
    n_tiles_max = 2 * r1 // MOE_TILE + N_EXPERTS
    counts = cnt[0, :N_EXPERTS].astype(jnp.int32)
    units_per = (counts + unit - 1) // unit
    tiles_per = (units_per + per_tile - 1) // per_tile
    tile_end = jnp.cumsum(tiles_per)
    tile_start = tile_end - tiles_per
    experts = rt[:, 0:2].astype(jnp.int32)
    ranks = rt[:, 2:4].astype(jnp.int32)
    pos_flat = (tile_start[experts] * MOE_TILE + ranks).reshape(-1)
    n_used = tile_end[-1:].astype(jnp.int32)
    tile_id = jnp.arange(n_tiles_max, dtype=jnp.int32)
    tile_expert = jnp.minimum(jnp.sum(tile_id[:, None] >= tile_end[None, :], axis=1), N_EXPERTS - 1).astype(jnp.int32)
    tile_units = jnp.clip(units_per[tile_expert] - per_tile * (tile_id - tile_start[tile_expert]), 0, per_tile)
    tile_units = jnp.where(tile_id < n_used[0], tile_units, 0).astype(jnp.int32)
    xs = scatter_rows(pos_flat, xp, jnp.zeros((n_tiles_max * MOE_TILE, d), F32))

    def short_tile_first(step, start, count, has_short, n_steps_used):
        local = step - start
        rotated = start + jnp.where(local == 0, count - 1, local - 1)
        return jnp.where(has_short & (count >= 2) & (step < n_steps_used), rotated, step).astype(jnp.int32)

    short_last = (units_per % per_tile) != 0
    tile_order = short_tile_first(tile_id, tile_start[tile_expert], tiles_per[tile_expert],
                                  short_last[tile_expert], n_used[0])
    act_e = matmul(xs, moe_w_gate_up.reshape(N_EXPERTS, d, 2 * ffn_expert), n_out=ffn_expert, out_dtype=BF16,
                   mode="swiglu", up_col0=ffn_expert,
                   schedule=(tile_expert, tile_units[tile_order], n_used, tile_order),
                   sub_tiles=per_tile, tm=MOE_TILE, vmem_mib=56, name="moe_gate_up")
    split = MOE_TILE // MOE_DOWN_TILE
    down_id = jnp.arange(n_tiles_max * split, dtype=jnp.int32)
    down_expert = jnp.repeat(tile_expert, split)
    down_valid = ((down_id % split) * MOE_DOWN_TILE // unit < jnp.repeat(tile_units, split)).astype(jnp.int32)
    down_order = short_tile_first(down_id, split * tile_start[down_expert], split * tiles_per[down_expert],
                                  short_last[down_expert], split * n_used[0])
    ys = matmul(act_e, moe_w_down.reshape(N_EXPERTS, ffn_expert, d), n_out=d, out_dtype=F32,
                schedule=(down_expert, down_valid[down_order], split * n_used, down_order), tm=MOE_DOWN_TILE,
                vmem_mib=56, name="moe_down")
    y_prompt, y_sample = combine_rows(pos_flat, rt, h3, norm_final[None, :], ys, n_prompt_rows=n_p)

    kv_meta = kvf[meta_row:meta_row + N_META]
    k_meta_p = jnp.broadcast_to(kv_meta[None, :, :kv_width], (bsz, N_META, kv_width))
    v_meta_p = jnp.broadcast_to(kv_meta[None, :, kv_width:], (bsz, N_META, kv_width))
    win_p = min(WINDOW_CHUNKS * CHUNK, seq)
    kv_win = kvf[:n_p].reshape(bsz, seq, 2 * kv_width)[:, seq - win_p:]
    shape4 = lambda a: a.reshape(a.shape[0], a.shape[1], SWA_KV_HEADS, HEAD_DIM)
    return (y_prompt.reshape(bsz, seq, d), y_sample.reshape(dbsz, t_new, d),
            s_prompt[None].astype(state_gla.dtype), s_sample[None].astype(state_gla.dtype),
            shape4(k_meta_p), shape4(v_meta_p),
            shape4(kv_win[:, :, :kv_width]), shape4(kv_win[:, :, kv_width:]),
            shape4(k_new.reshape(dbsz, t_new, kv_width)), shape4(v_new.reshape(dbsz, t_new, kv_width)))
```
